```python
import math
import jax, jax.numpy as jnp
from jax import lax
import numpy as np

D_MODEL = 1024
BATCH = 8
SEQ = 2048
DEPTH = 1
DEC_BATCH = 128
DEC_SEQ = 8
PAST_LEN = 16384
PAGE_SIZE = 128

HG_W = D_MODEL // 2
HG_DK = 128
HG_DV = 128
HG_H = HG_W // HG_DK
HG_CHUNK = 64
RW_W = D_MODEL - HG_W
RW_N = 64
RW_H = RW_W // RW_N
RW_DECAY_LORA = 64
RW_A_LORA = 64
RW_G_LORA = 128
HG_COLS = 2 * HG_H * HG_DK + 2 * HG_H * HG_DV
RW_COLS = 3 * RW_W + RW_DECAY_LORA + RW_A_LORA + RW_G_LORA
MIX_W = HG_H * HG_DV + RW_W
N_EXPERTS = 32
TOP_K = 4
D_FF = D_MODEL
SWIGLU_LIMIT = 7.0
SWIGLU_ALPHA = 1.702
MOE_BLOCK = 128
NORM_EPS = 1e-6
RW_GN_EPS = 64e-5

kernel_name = "hymba_style_hgrn2_rwkv7_moe_step"

F32 = jnp.float32


def rmsnorm(x, g):
    xf = x.astype(F32)
    y = xf * lax.rsqrt(jnp.mean(xf * xf, axis=-1, keepdims=True) + NORM_EPS)
    return (y * g.astype(F32)).astype(x.dtype)


def hgrn2_mix(q, f_pre, i_in, g_in, lb, s0, gn_w):
    B, T, _ = q.shape
    C = math.gcd(HG_CHUNK, T)
    NC = T // C
    qf = jax.nn.silu(q.astype(F32))
    f = lb + (1.0 - lb) * jax.nn.sigmoid(f_pre.astype(F32))
    log_f = jnp.log(f)
    k = 1.0 - f

    def to_chunks(t, d):
        return t.reshape(B, NC, C, HG_H, d).transpose(1, 0, 3, 2, 4)

    mask = jnp.tril(jnp.ones((C, C), dtype=bool))

    def step(S, inp):
        qc, kc, lfc, ic = inp
        G = jnp.cumsum(lfc, axis=2)
        o_inter = jnp.einsum('bhtk,bhkv->bhtv', qc * jnp.exp(G), S)
        diff = G[:, :, :, None, :] - G[:, :, None, :, :]
        decay = jnp.exp(jnp.where(mask[:, :, None], diff, -jnp.inf))
        A = jnp.einsum('bhtk,bhtsk,bhsk->bhts', qc, decay, kc)
        o = o_inter + jnp.einsum('bhts,bhsv->bhtv', A, ic)
        G_last = G[:, :, -1:, :]
        S = jnp.exp(G_last[:, :, 0, :])[..., None] * S + jnp.einsum(
            'bhsk,bhsv->bhkv', kc * jnp.exp(G_last - G), ic)
        return S, o

    S, o = lax.scan(step, s0.astype(F32),
                    (to_chunks(qf, HG_DK), to_chunks(k, HG_DK), to_chunks(log_f, HG_DK),
                     to_chunks(i_in.astype(F32), HG_DV)))
    o = o.transpose(1, 0, 3, 2, 4).reshape(B, T, HG_H, HG_DV)
    o = o * lax.rsqrt(jnp.mean(o * o, axis=-1, keepdims=True) + NORM_EPS) * gn_w.astype(F32)
    o = o * jax.nn.silu(g_in.astype(F32).reshape(B, T, HG_H, HG_DV))
    return o.reshape(B, T, HG_H * HG_DV), S


def rwkv7_mix(p, p_prev, s0, mu, w0, w2, a0, a2, g2, k_k, k_a, r_k, lnx_w, lnx_b):
    B, T, _ = p.shape
    p = p.astype(F32)
    p_shift = jnp.concatenate([p_prev.astype(F32)[:, None, :], p[:, :-1]], axis=1)
    pm = p + (p_shift - p) * mu
    o1, o2, o3 = RW_W, 2 * RW_W, 3 * RW_W
    o4 = o3 + RW_DECAY_LORA
    o5 = o4 + RW_A_LORA
    r, k, v = pm[..., :o1], pm[..., o1:o2], pm[..., o2:o3]
    w_lo, a_lo, g_lo = pm[..., o3:o4], pm[..., o4:o5], pm[..., o5:]
    log_w = -jax.nn.softplus(-(w0 + jnp.tanh(w_lo) @ w2)) - 0.5
    decay = jnp.exp(-jnp.exp(log_w))
    a = jax.nn.sigmoid(a0 + a_lo @ a2)
    g = jax.nn.sigmoid(g_lo) @ g2

    def heads(t):
        return t.reshape(B, T, RW_H, RW_N)

    kk = heads(k * k_k)
    kk = kk / jnp.maximum(jnp.sqrt(jnp.sum(kk * kk, axis=-1, keepdims=True)), 1e-12)
    k = k * (1.0 + (a - 1.0) * k_a)
    r_h, k_h, v_h, w_h, a_h = heads(r), heads(k), heads(v), heads(decay), heads(a)

    def step(S, inp):
        r_t, w_t, k_t, v_t, kk_t, a_t = inp
        sa = jnp.einsum('bhvk,bhk->bhv', S, -kk_t)
        S = (S * w_t[:, :, None, :] + sa[..., None] * (kk_t * a_t)[:, :, None, :]
             + v_t[..., None] * k_t[:, :, None, :])
        return S, jnp.einsum('bhvk,bhk->bhv', S, r_t)

    def tm(t):
        return jnp.swapaxes(t, 0, 1)

    S, y = lax.scan(step, s0.astype(F32), (tm(r_h), tm(w_h), tm(k_h), tm(v_h), tm(kk), tm(a_h)))
    y = tm(y)
    mean = jnp.mean(y, axis=-1, keepdims=True)
    var = jnp.mean(jnp.square(y - mean), axis=-1, keepdims=True)
    y = ((y - mean) * lax.rsqrt(var + RW_GN_EPS)).reshape(B, T, RW_W) * lnx_w + lnx_b
    bonus = jnp.sum(r_h * k_h * r_k, axis=-1, keepdims=True) * v_h
    y = (y + bonus.reshape(B, T, RW_W)) * g
    return y, S


def moe_ffn(h, w_router, b_router, w_gate_up, b_gate_up, w_down, b_down):
    T = h.shape[0]
    logits = h.astype(F32) @ w_router.astype(F32) + b_router.astype(F32)
    top_val, top_idx = lax.top_k(logits, TOP_K)
    gates = jax.nn.softmax(top_val, axis=-1)
    TK = T * TOP_K
    flat_e = top_idx.reshape(-1)
    order = jnp.argsort(flat_e)
    sorted_e = flat_e[order]
    counts = jnp.bincount(flat_e, length=N_EXPERTS)
    padded = ((counts + MOE_BLOCK - 1) // MOE_BLOCK) * MOE_BLOCK
    ends = jnp.cumsum(padded)
    pad_start = ends - padded
    start = jnp.cumsum(counts) - counts
    rank = jnp.arange(TK, dtype=jnp.int32) - start[sorted_e]
    dest = pad_start[sorted_e] + rank
    n_blocks = -(-TK // MOE_BLOCK) + N_EXPERTS
    n_rows = n_blocks * MOE_BLOCK
    token_of_row = jnp.full((n_rows,), T, dtype=jnp.int32).at[dest].set((order // TOP_K).astype(jnp.int32))
    gate_of_row = jnp.zeros((n_rows,), F32).at[dest].set(gates.reshape(-1)[order])
    block_start = jnp.arange(n_blocks, dtype=jnp.int32) * MOE_BLOCK
    block_expert = jnp.minimum(jnp.searchsorted(ends, block_start, side='right'), N_EXPERTS - 1)
    h_pad = jnp.concatenate([h, jnp.zeros((1, h.shape[1]), h.dtype)], axis=0)
    xb = h_pad[token_of_row].reshape(n_blocks, MOE_BLOCK, h.shape[1])

    def expert_block(args):
        xblk, e = args
        gu = xblk.astype(F32) @ w_gate_up[e].astype(F32) + b_gate_up[e].astype(F32)
        gate, up = gu[:, :D_FF], gu[:, D_FF:]
        gate = jnp.minimum(gate, SWIGLU_LIMIT)
        up = jnp.clip(up, -SWIGLU_LIMIT, SWIGLU_LIMIT)
        glu = gate * jax.nn.sigmoid(SWIGLU_ALPHA * gate)
        return ((up + 1.0) * glu) @ w_down[e].astype(F32) + b_down[e].astype(F32)

    yb = lax.map(expert_block, (xb, block_expert)).reshape(n_rows, -1)
    y = jax.ops.segment_sum(yb * gate_of_row[:, None], token_of_row, num_segments=T + 1)[:T]
    return y.astype(h.dtype)


def trunk(x, st_hg, st_rw, st_shift, norm_mix, w_in, lb_logits, hg_norm_w, rw_mu, rw_w0, rw_w2,
          rw_a0, rw_a2, rw_g2, rw_k_k, rw_k_a, rw_r_k, rw_lnx_w, rw_lnx_b, w_out, norm_ffn,
          w_router, b_router, w_gate_up, b_gate_up, w_down, b_down, norm_final):
    B, T, _ = x.shape
    lbs = jnp.cumsum(jax.nn.softmax(lb_logits.astype(F32), axis=0), axis=0)
    new_hg, new_rw, new_shift = [], [], []
    for l in range(DEPTH):
        h = rmsnorm(x, norm_mix[l])
        proj = h @ w_in[l]
        hq = HG_H * HG_DK
        hv = HG_H * HG_DV
        q = proj[..., :hq]
        f_pre = proj[..., hq:2 * hq]
        i_in = proj[..., 2 * hq:2 * hq + hv]
        g_in = proj[..., 2 * hq + hv:HG_COLS]
        o_hg, s_hg = hgrn2_mix(q, f_pre, i_in, g_in, lbs[l], st_hg[l], hg_norm_w[l])
        p_prev = st_shift[l] @ w_in[l][:, HG_COLS:]
        o_rw, s_rw = rwkv7_mix(proj[..., HG_COLS:], p_prev, st_rw[l], rw_mu[l], rw_w0[l], rw_w2[l],
                               rw_a0[l], rw_a2[l], rw_g2[l], rw_k_k[l], rw_k_a[l], rw_r_k[l],
                               rw_lnx_w[l], rw_lnx_b[l])
        mixed = jnp.concatenate([o_hg, o_rw], axis=-1).astype(x.dtype)
        x = x + mixed @ w_out[l]
        hf = rmsnorm(x, norm_ffn[l]).reshape(B * T, D_MODEL)
        x = x + moe_ffn(hf, w_router[l], b_router[l], w_gate_up[l], b_gate_up[l],
                        w_down[l], b_down[l]).reshape(B, T, D_MODEL)
        new_hg.append(s_hg.astype(st_hg.dtype))
        new_rw.append(s_rw.astype(st_rw.dtype))
        new_shift.append(h[:, -1, :].astype(st_shift.dtype))
    return rmsnorm(x, norm_final), jnp.stack(new_hg), jnp.stack(new_rw), jnp.stack(new_shift)


def setup_inputs(seed: int = 0) -> dict:
    key = jax.random.key(seed)
    keys = jax.random.split(key, 32)
    ks = iter([keys[j] for j in range(32)])

    def nrm(shape, s):
        return jax.random.normal(next(ks), shape, F32) * s

    L = DEPTH
    return {
        "x_prompt": nrm((BATCH, SEQ, D_MODEL), 1.0),
        "x_sample": nrm((DEC_BATCH, DEC_SEQ, D_MODEL), 1.0),
        "state_hgrn": nrm((L, DEC_BATCH, HG_H, HG_DK, HG_DV), 0.5),
        "state_rwkv": nrm((L, DEC_BATCH, RW_H, RW_N, RW_N), 0.2),
        "state_shift": nrm((L, DEC_BATCH, D_MODEL), 1.0),
        "norm_mix": 1.0 + nrm((L, D_MODEL), 0.05),
        "w_in": nrm((L, D_MODEL, HG_COLS + RW_COLS), D_MODEL ** -0.5),
        "lb_logits": nrm((L + 1, HG_H * HG_DK), 1.0),
        "hg_norm_w": 1.0 + nrm((L, HG_DV), 0.05),
        "rw_mu": jax.random.uniform(next(ks), (L, RW_COLS), F32, 0.0, 1.0),
        "rw_w0": nrm((L, RW_W), 0.5),
        "rw_w2": nrm((L, RW_DECAY_LORA, RW_W), 0.1),
        "rw_a0": nrm((L, RW_W), 0.1),
        "rw_a2": nrm((L, RW_A_LORA, RW_W), 0.1),
        "rw_g2": nrm((L, RW_G_LORA, RW_W), RW_G_LORA ** -0.5),
        "rw_k_k": 0.85 + nrm((L, RW_W), 0.05),
        "rw_k_a": 1.0 + nrm((L, RW_W), 0.05),
        "rw_r_k": nrm((L, RW_H, RW_N), 0.1),
        "rw_lnx_w": 1.0 + nrm((L, RW_W), 0.05),
        "rw_lnx_b": nrm((L, RW_W), 0.01),
        "w_out": nrm((L, MIX_W, D_MODEL), MIX_W ** -0.5),
        "norm_ffn": 1.0 + nrm((L, D_MODEL), 0.05),
        "w_router": nrm((L, D_MODEL, N_EXPERTS), D_MODEL ** -0.5),
        "b_router": nrm((L, N_EXPERTS), 0.01),
        "w_gate_up": nrm((L, N_EXPERTS, D_MODEL, 2 * D_FF), D_MODEL ** -0.5),
        "b_gate_up": nrm((L, N_EXPERTS, 2 * D_FF), 0.01),
        "w_down": nrm((L, N_EXPERTS, D_FF, D_MODEL), D_FF ** -0.5),
        "b_down": nrm((L, N_EXPERTS, D_MODEL), 0.01),
        "norm_final": 1.0 + nrm((D_MODEL,), 0.05),
    }


def reference(x_prompt, x_sample, state_hgrn, state_rwkv, state_shift, norm_mix, w_in, lb_logits,
              hg_norm_w, rw_mu, rw_w0, rw_w2, rw_a0, rw_a2, rw_g2, rw_k_k, rw_k_a, rw_r_k,
              rw_lnx_w, rw_lnx_b, w_out, norm_ffn, w_router, b_router, w_gate_up, b_gate_up,
              w_down, b_down, norm_final):
    Bp = x_prompt.shape[0]
    zero_hg = jnp.zeros((DEPTH, Bp, HG_H, HG_DK, HG_DV), state_hgrn.dtype)
    zero_rw = jnp.zeros((DEPTH, Bp, RW_H, RW_N, RW_N), state_rwkv.dtype)
    zero_shift = jnp.zeros((DEPTH, Bp, D_MODEL), state_shift.dtype)
    y_prompt, hgrn_prompt, rwkv_prompt, shift_prompt = trunk(
        x_prompt, zero_hg, zero_rw, zero_shift, norm_mix, w_in, lb_logits, hg_norm_w, rw_mu,
        rw_w0, rw_w2, rw_a0, rw_a2, rw_g2, rw_k_k, rw_k_a, rw_r_k, rw_lnx_w, rw_lnx_b, w_out,
        norm_ffn, w_router, b_router, w_gate_up, b_gate_up, w_down, b_down, norm_final)
    y_sample, hgrn_sample, rwkv_sample, shift_sample = trunk(
        x_sample, state_hgrn, state_rwkv, state_shift, norm_mix, w_in, lb_logits, hg_norm_w, rw_mu,
        rw_w0, rw_w2, rw_a0, rw_a2, rw_g2, rw_k_k, rw_k_a, rw_r_k, rw_lnx_w, rw_lnx_b, w_out,
        norm_ffn, w_router, b_router, w_gate_up, b_gate_up, w_down, b_down, norm_final)
    return (y_prompt, y_sample, hgrn_prompt, rwkv_prompt, shift_prompt,
            hgrn_sample, rwkv_sample, shift_sample)
```

```python
import functools

import jax
import jax.numpy as jnp
from jax import lax
from jax.experimental import pallas as pl
from jax.experimental.pallas import tpu as pltpu

F32 = jnp.float32
BF16 = jnp.bfloat16
I32 = jnp.int32

D_MODEL = 1024
HG_H, HG_DK, HG_DV = 4, 128, 128
HG_CHUNK = 64
HG_SUB = 16
RW_H, RW_N = 8, 64
RW_W = RW_H * RW_N
RW_PAIRS = RW_H // 2
RW_TILE = 64
RW_CHUNK = 16
HG_COLS = 2 * HG_H * HG_DK + 2 * HG_H * HG_DV
RW_COLS = 3 * RW_W + 64 + 64 + 128
N_EXPERTS = 32
TOP_K = 4
D_FF = D_MODEL
SWIGLU_LIMIT = 7.0
SWIGLU_ALPHA = 1.702
NORM_EPS = 1e-6
RW_GN_EPS = 64e-5
MOE_BLOCK = 256
VMEM_LIMIT = 56 * 1024 * 1024


def _dot(a, b):
    return jnp.dot(a, b, preferred_element_type=F32)


def _dot_nt(a, b):
    return lax.dot_general(a, b, (((1,), (1,)), ((), ())), preferred_element_type=F32)


def _dot_tn(a, b):
    return lax.dot_general(a, b, (((0,), (0,)), ((), ())), preferred_element_type=F32)


def _split3(x):
    x1 = x.astype(BF16)
    r1 = x - x1.astype(F32)
    x2 = r1.astype(BF16)
    x3 = (r1 - x2.astype(F32)).astype(BF16)
    return x1, x2, x3


def _dot_sel_l(m_bf16, x):
    x1, x2, x3 = _split3(x)
    return _dot(m_bf16, x1) + _dot(m_bf16, x2) + _dot(m_bf16, x3)


def _dot_sel_r(x, m_bf16):
    x1, x2, x3 = _split3(x)
    return _dot(x1, m_bf16) + _dot(x2, m_bf16) + _dot(x3, m_bf16)


def _sigmoid(x):
    return 1.0 / (1.0 + jnp.exp(-x))


def _rms(x, g):
    return x * lax.rsqrt(jnp.mean(x * x, axis=-1, keepdims=True) + NORM_EPS) * g


def _cparams(sem):
    return pltpu.CompilerParams(dimension_semantics=sem, vmem_limit_bytes=VMEM_LIMIT)


def _proj_kernel(x_ref, g_ref, w_ref, *o_refs, normalize, splits):
    x = x_ref[...]
    h = _rms(x, g_ref[...]) if normalize else x
    hb = h.astype(BF16)
    c0 = 0
    for o_ref, width in zip(o_refs, splits):
        o_ref[...] = _dot(hb, w_ref[:, c0:c0 + width])
        c0 += width


def _proj(x, g, w_bf16, splits, normalize, tm):
    n = x.shape[0]
    kern = functools.partial(_proj_kernel, normalize=normalize, splits=splits)
    return pl.pallas_call(
        kern,
        grid=(n // tm,),
        in_specs=[pl.BlockSpec((tm, D_MODEL), lambda i: (i, 0)),
                  pl.BlockSpec((1, D_MODEL), lambda i: (0, 0)),
                  pl.BlockSpec((D_MODEL, sum(splits)), lambda i: (0, 0))],
        out_specs=[pl.BlockSpec((tm, s), lambda i: (i, 0)) for s in splits],
        out_shape=[jax.ShapeDtypeStruct((n, s), F32) for s in splits],
        compiler_params=_cparams(("parallel",)),
        name="norm_in_proj" if normalize else "shift_proj",
    )(x, g, w_bf16)


def _rmsnorm_rows_kernel(x_ref, g_ref, o_ref):
    o_ref[...] = _rms(x_ref[...], g_ref[...])


def _rmsnorm_rows(x, g):
    return pl.pallas_call(
        _rmsnorm_rows_kernel,
        out_shape=jax.ShapeDtypeStruct(x.shape, F32),
        name="shift_norm",
    )(x, g)


def _hgrn_chunk(q, fp, iv, lb, st, tri, c, sub):
    qf = q * _sigmoid(q)
    f = lb + (1.0 - lb) * _sigmoid(fp)
    lf = jnp.log(f)
    kc = 1.0 - f
    g = _dot_sel_l(tri, lf)
    o_inter = _dot_nt((qf * jnp.exp(g)).astype(BF16), st.astype(BF16))
    ivb = iv.astype(BF16)
    lane = lax.broadcasted_iota(I32, (sub, sub), 1)
    row = lax.broadcasted_iota(I32, (sub, sub), 0)
    outs = []
    for i in range(c // sub):
        lo = i * sub
        gi, qi, ki = g[lo:lo + sub], qf[lo:lo + sub], kc[lo:lo + sub]
        a = jnp.zeros((sub, sub), F32)
        for s in range(sub):
            e = jnp.exp(jnp.minimum(gi - gi[s:s + 1, :], 0.0))
            col = jnp.sum(qi * (ki[s:s + 1, :] * e), axis=-1, keepdims=True)
            a = jnp.where(lane == s, col, a)
        a = jnp.where(row >= lane, a, 0.0)
        oi = o_inter[lo:lo + sub] + _dot(a.astype(BF16), ivb[lo:lo + sub])
        if i > 0:
            gr = g[lo - 1:lo, :]
            qd = qi * jnp.exp(gi - gr)
            kd = kc[:lo] * jnp.exp(gr - g[:lo])
            a_off = _dot_nt(qd.astype(BF16), kd.astype(BF16))
            oi = oi + _dot(a_off.astype(BF16), ivb[:lo])
        outs.append(oi)
    o = outs[0] if len(outs) == 1 else jnp.concatenate(outs, axis=0)
    gl = g[c - 1:c, :]
    kd = kc * jnp.exp(gl - g)
    st_new = st * jnp.exp(gl) + _dot_tn(ivb, kd.astype(BF16))
    return o, st_new


def _hgrn_kernel(q_ref, f_ref, i_ref, g_ref, lbl_ref, gw_ref, tri_ref, s0_ref, o_ref, so_ref, st_ref,
                 *, nseq, tb, c, sub):
    j = pl.program_id(2)
    nt = pl.num_programs(2)

    @pl.when(j == 0)
    def _():
        for s in range(nseq):
            st_ref[s] = s0_ref[s, 0].T

    lbl = lbl_ref[...]
    ex = jnp.exp(lbl - jnp.max(lbl, axis=0, keepdims=True))
    lb = ex[0:1, :] / jnp.sum(ex, axis=0, keepdims=True)
    gw = gw_ref[...]
    tri = tri_ref[...]
    n_chunks = tb // c

    def body(it, carry):
        s = it // n_chunks
        r0 = pl.multiple_of(it * c, c)
        rows = pl.ds(r0, c)
        o, st_new = _hgrn_chunk(q_ref[rows, :], f_ref[rows, :], i_ref[rows, :], lb, st_ref[s], tri, c, sub)
        st_ref[s] = st_new
        o = o * lax.rsqrt(jnp.mean(o * o, axis=-1, keepdims=True) + NORM_EPS) * gw
        gin = g_ref[rows, :]
        o_ref[rows, :] = o * (gin * _sigmoid(gin))
        return carry

    lax.fori_loop(0, nseq * n_chunks, body, 0)

    @pl.when(j == nt - 1)
    def _():
        for s in range(nseq):
            so_ref[s, 0] = st_ref[s].T


def _hgrn(proj_hg, lb_logits, gn_w, s0, o_prev, *, row0, batch, seq, nseq, tb, c, sub):
    n = proj_hg.shape[0]
    rows = nseq * tb
    nt = seq // tb
    blk0 = row0 // rows
    tri = (lax.broadcasted_iota(I32, (c, c), 0) >= lax.broadcasted_iota(I32, (c, c), 1)).astype(BF16)

    def rmap(col):
        return lambda b, h, j: (blk0 + b * nt + j, col * HG_H + h)

    kern = functools.partial(_hgrn_kernel, nseq=nseq, tb=tb, c=c, sub=sub)
    in_specs = [pl.BlockSpec((rows, 128), rmap(0)), pl.BlockSpec((rows, 128), rmap(1)),
                pl.BlockSpec((rows, 128), rmap(2)), pl.BlockSpec((rows, 128), rmap(3)),
                pl.BlockSpec((lb_logits.shape[0], 128), lambda b, h, j: (0, h)),
                pl.BlockSpec((1, 128), lambda b, h, j: (0, 0)),
                pl.BlockSpec((c, c), lambda b, h, j: (0, 0)),
                pl.BlockSpec((nseq, 1, HG_DK, HG_DV), lambda b, h, j: (b, h, 0, 0))]
    args = [proj_hg, proj_hg, proj_hg, proj_hg, lb_logits, gn_w, tri, s0]
    aliases = {}
    if o_prev is not None:
        in_specs.append(pl.BlockSpec(memory_space=pl.ANY))
        args.append(o_prev)
        aliases = {len(args) - 1: 0}
        kern_fn = lambda *refs: kern(*refs[:8], *refs[9:])
    else:
        kern_fn = kern
    return pl.pallas_call(
        kern_fn,
        grid=(batch // nseq, HG_H, nt),
        in_specs=in_specs,
        out_specs=[pl.BlockSpec((rows, 128), lambda b, h, j: (blk0 + b * nt + j, h)),
                   pl.BlockSpec((nseq, 1, HG_DK, HG_DV), lambda b, h, j: (b, h, 0, 0))],
        out_shape=[jax.ShapeDtypeStruct((n, HG_H * HG_DV), F32),
                   jax.ShapeDtypeStruct((batch, HG_H, HG_DK, HG_DV), F32)],
        scratch_shapes=[pltpu.VMEM((nseq, HG_DV, HG_DK), F32)],
        input_output_aliases=aliases,
        compiler_params=_cparams(("parallel", "parallel", "arbitrary")),
        name="hgrn2_mix",
    )(*args)


def _softplus(z):
    return jnp.maximum(z, 0.0) + jnp.log(1.0 + jnp.exp(-jnp.abs(z)))


def _rwkv_kernel(p_ref, prev_ref, s0_ref, mu_ref, w0_ref, a0_ref, wwa_ref, g2_ref, kkw_ref, kaw_ref, rk_ref,
                 lnw_ref, lnb_ref, gones_ref, tri_ref, cones_ref, o_ref, so_ref, st_ref, carry_ref,
                 *, c, chain, nseq, seq):
    tt = RW_TILE
    nc = tt // c
    j = pl.program_id(1)
    nt = pl.num_programs(1)
    lane128 = lax.broadcasted_iota(I32, (1, 128), 1)
    lo_half = lane128 < RW_N

    def block_diag(se, so):
        z = jnp.zeros((RW_N, RW_N), F32)
        return jnp.concatenate([jnp.concatenate([se, z], axis=1), jnp.concatenate([z, so], axis=1)], axis=0)

    p = p_ref[...]
    rolled = pltpu.roll(p, 1, 0)
    rowi = lax.broadcasted_iota(I32, (tt, 1), 0)
    if chain:
        @pl.when(j == 0)
        def _():
            carry_ref[...] = prev_ref[0]
            for pr in range(RW_PAIRS):
                st_ref[pr] = block_diag(s0_ref[0, 2 * pr], s0_ref[0, 2 * pr + 1])

        ps = jnp.where(rowi == 0, carry_ref[...], rolled)
        carry_ref[...] = p[tt - 1:tt, :]
    else:
        ps = rolled
        for s in range(nseq):
            ps = jnp.where(rowi == s * seq, prev_ref[s], ps)
    pm = p + (ps - p) * mu_ref[...]

    r = pm[:, 0:RW_W]
    k = pm[:, RW_W:2 * RW_W]
    v = pm[:, 2 * RW_W:3 * RW_W]
    wa = pm[:, 3 * RW_W:3 * RW_W + 128]
    g_lo = pm[:, 3 * RW_W + 128:3 * RW_W + 256]
    wa = jnp.where(lo_half, jnp.tanh(wa), wa)
    xwa = _dot(wa.astype(BF16), wwa_ref[...])
    log_w = -_softplus(-(w0_ref[...] + xwa[:, :RW_W])) - 0.5
    ld = -jnp.exp(log_w)
    a = _sigmoid(a0_ref[...] + xwa[:, RW_W:])
    gate = _dot(_sigmoid(g_lo).astype(BF16), g2_ref[...])
    gones = gones_ref[...]
    kkv = k * kkw_ref[...]
    kk = kkv / jnp.maximum(jnp.sqrt(_dot_sel_r(kkv * kkv, gones)), 1e-12)
    k2 = k * (1.0 + (a - 1.0) * kaw_ref[...])
    beta = kk * a

    gc = _dot_sel_l(tri_ref[...], ld)
    gtot = _dot_sel_l(cones_ref[...], ld)
    e_g = jnp.exp(gc)
    e_ng = jnp.exp(-gc)
    e_l = jnp.exp(gtot - gc)
    ah = -kk * jnp.exp(gc - ld)
    rh = r * e_g
    bh = beta * e_ng
    kh = k2 * e_ng
    bt = beta * e_l
    kt = k2 * e_l
    e_tot = jnp.exp(gtot)

    ri = lax.broadcasted_iota(I32, (2 * tt, 2 * tt), 0)
    ci = lax.broadcasted_iota(I32, (2 * tt, 2 * tt), 1)
    same = (ri // c) == (ci // c)
    m_strict = jnp.where(same & (ci < ri), 1.0, 0.0)
    m_incl = jnp.where(same & (ci <= ri), 1.0, 0.0)
    eye2 = jnp.where(ri == ci, 1.0, 0.0)
    bd = jnp.where((ri // RW_N) == (ci // RW_N), 1.0, 0.0)

    def stack(z):
        return jnp.concatenate([jnp.where(lo_half, z, 0.0), jnp.where(lo_half, 0.0, z)], axis=0)

    def fold(z):
        return z[:tt] + z[tt:]

    ys = []
    for pr in range(RW_PAIRS):
        sl = slice(pr * 128, (pr + 1) * 128)
        xa, xr, xb, xk, xv = (stack(t[:, sl]) for t in (ah, rh, bh, kh, v))
        xab, xrb, xbb, xkb, xvb = (t.astype(BF16) for t in (xa, xr, xb, xk, xv))
        a_ab = _dot_nt(xab, xbb) * m_strict
        a_ak = _dot_nt(xab, xkb) * m_strict
        a_rb = (_dot_nt(xrb, xbb) * m_incl).astype(BF16)
        a_rk = (_dot_nt(xrb, xkb) * m_incl).astype(BF16)
        tm = eye2 + a_ab
        pw = a_ab
        n = 1
        while 2 * n < c:
            pwb = pw.astype(BF16)
            pw = _dot(pwb, pwb)
            tm = tm + _dot(tm.astype(BF16), pw.astype(BF16))
            n *= 2
        tmb = tm.astype(BF16)
        ta = _dot(tmb, xab)
        tv = _dot(tmb, _dot(a_ak.astype(BF16), xvb).astype(BF16))
        tab, tvb = ta.astype(BF16), tv.astype(BF16)
        rp = fold(xr + _dot(a_rb, tab))
        y0 = fold(_dot(a_rb, tvb) + _dot(a_rk, xvb))
        ap = fold(ta).astype(BF16)
        vp = fold(tv).astype(BF16)
        btp = bt[:, sl].astype(BF16)
        ktp = kt[:, sl].astype(BF16)
        vpl = v[:, sl].astype(BF16)
        rpb = rp.astype(BF16)
        if chain:
            st = st_ref[pr]
        y_chunks = []
        for ch in range(nc):
            rs = slice(ch * c, (ch + 1) * c)
            mt = _dot_tn(ap[rs], btp[rs]) * bd + eye2 * e_tot[ch * c:ch * c + 1, sl]
            ht = (_dot_tn(vp[rs], btp[rs]) + _dot_tn(vpl[rs], ktp[rs])) * bd
            if not chain:
                st = block_diag(s0_ref[ch, 2 * pr], s0_ref[ch, 2 * pr + 1])
            stb = st.astype(BF16)
            y_chunks.append(_dot_nt(rpb[rs], stb) + y0[rs])
            st = _dot(stb, mt.astype(BF16)) + ht
            if not chain:
                so_ref[ch, 2 * pr] = st[:RW_N, :RW_N]
                so_ref[ch, 2 * pr + 1] = st[RW_N:, RW_N:]
        if chain:
            st_ref[pr] = st
        ys.append(jnp.concatenate(y_chunks, axis=0))
    y = jnp.concatenate(ys, axis=1)

    if chain:
        @pl.when(j == nt - 1)
        def _():
            for pr in range(RW_PAIRS):
                st = st_ref[pr]
                so_ref[0, 2 * pr] = st[:RW_N, :RW_N]
                so_ref[0, 2 * pr + 1] = st[RW_N:, RW_N:]

    inv_n = 1.0 / RW_N
    mean = _dot_sel_r(y, gones) * inv_n
    yc = y - mean
    var = _dot_sel_r(yc * yc, gones) * inv_n
    yn = yc * lax.rsqrt(var + RW_GN_EPS) * lnw_ref[...] + lnb_ref[...]
    bonus = _dot_sel_r(r * k2 * rk_ref[...], gones) * v
    o_ref[...] = (yn + bonus) * gate


def _rwkv(proj_rw, prev, s0, wts, o_prev, *, row0, batch, seq, chain):
    n = proj_rw.shape[0]
    tt = RW_TILE
    if chain:
        nseq, c, nt = 1, RW_CHUNK, seq // tt
    else:
        nseq, c, nt = tt // seq, seq, 1
    blk0 = row0 // tt
    ri = lax.broadcasted_iota(I32, (tt, tt), 0)
    ci = lax.broadcasted_iota(I32, (tt, tt), 1)
    same = (ri // c) == (ci // c)
    tri = (same & (ci <= ri)).astype(BF16)
    cones = same.astype(BF16)
    gi = lax.broadcasted_iota(I32, (RW_W, RW_W), 0) // RW_N
    gj = lax.broadcasted_iota(I32, (RW_W, RW_W), 1) // RW_N
    gones = (gi == gj).astype(BF16)
    (mu, w0, a0, wwa, g2, kkw, kaw, rk, lnw, lnb) = wts

    def full(arr):
        return pl.BlockSpec(arr.shape, lambda b, j: (0,) * arr.ndim)

    consts = [mu, w0, a0, wwa, g2, kkw, kaw, rk, lnw, lnb, gones, tri, cones]
    in_specs = [pl.BlockSpec((tt, RW_COLS), lambda b, j: (blk0 + b * nt + j, 0)),
                pl.BlockSpec((nseq, 1, RW_COLS), lambda b, j: (b, 0, 0)),
                pl.BlockSpec((nseq, RW_H, RW_N, RW_N), lambda b, j: (b, 0, 0, 0))]
    in_specs += [full(t) for t in consts]
    args = [proj_rw, prev, s0] + consts
    kern = functools.partial(_rwkv_kernel, c=c, chain=chain, nseq=nseq, seq=seq)
    aliases = {}
    if o_prev is not None:
        in_specs.append(pl.BlockSpec(memory_space=pl.ANY))
        args.append(o_prev)
        aliases = {len(args) - 1: 0}
        n_in = len(args) - 1
        kern_fn = lambda *refs: kern(*refs[:n_in], *refs[n_in + 1:])
    else:
        kern_fn = kern
    return pl.pallas_call(
        kern_fn,
        grid=(batch // nseq, nt),
        in_specs=in_specs,
        out_specs=[pl.BlockSpec((tt, RW_W), lambda b, j: (blk0 + b * nt + j, 0)),
                   pl.BlockSpec((nseq, RW_H, RW_N, RW_N), lambda b, j: (b, 0, 0, 0))],
        out_shape=[jax.ShapeDtypeStruct((n, RW_W), F32),
                   jax.ShapeDtypeStruct((batch, RW_H, RW_N, RW_N), F32)],
        scratch_shapes=[pltpu.VMEM((RW_PAIRS, 128, 128), F32), pltpu.VMEM((1, RW_COLS), F32)],
        input_output_aliases=aliases,
        compiler_params=_cparams(("parallel", "arbitrary")),
        name="rwkv7_mix",
    )(*args)


def _outproj_router_kernel(ohg_ref, orw_ref, x_ref, wo_ref, gf_ref, wr_ref, br_ref, tri_ref,
                           x1_ref, hf_ref, idx_ref, gate_ref, pos_ref, cnt_ref, carry_ref):
    i = pl.program_id(0)

    @pl.when(i == 0)
    def _():
        carry_ref[...] = jnp.zeros_like(carry_ref)

    half = HG_H * HG_DV
    mixed = _dot(ohg_ref[...].astype(BF16), wo_ref[:half, :]) + _dot(orw_ref[...].astype(BF16), wo_ref[half:, :])
    x1 = x_ref[...] + mixed
    x1_ref[...] = x1
    hf = _rms(x1, gf_ref[...])
    hf_ref[...] = hf

    h1, h2, _ = _split3(hf)
    w1, w2, _ = _split3(wr_ref[...])
    logits = _dot(h1, w1) + _dot(h1, w2) + _dot(h2, w1) + br_ref[...]

    tm = logits.shape[0]
    lane = lax.broadcasted_iota(I32, (tm, N_EXPERTS), 1).astype(F32)
    lane4 = lax.broadcasted_iota(I32, (tm, TOP_K), 1)
    vals = logits
    maskf = jnp.zeros((tm, N_EXPERTS), F32)
    sels, tops = [], []
    idx_out = jnp.zeros((tm, TOP_K), F32)
    for kq in range(TOP_K):
        m = jnp.max(vals, axis=-1, keepdims=True)
        idx = jnp.min(jnp.where(vals == m, lane, float(N_EXPERTS)), axis=-1, keepdims=True)
        sel = lane == idx
        sels.append(sel)
        tops.append(m)
        idx_out = jnp.where(lane4 == kq, idx, idx_out)
        vals = jnp.where(sel, -jnp.inf, vals)
        maskf = maskf + jnp.where(sel, 1.0, 0.0)
    es = [jnp.exp(t - tops[0]) for t in tops]
    denom = es[0] + es[1] + es[2] + es[3]
    gate_out = jnp.zeros((tm, TOP_K), F32)
    for kq in range(TOP_K):
        gate_out = jnp.where(lane4 == kq, es[kq] / denom, gate_out)

    pos = _dot(tri_ref[...], maskf.astype(BF16)) + carry_ref[...]
    pos_out = jnp.zeros((tm, TOP_K), F32)
    for kq in range(TOP_K):
        pk = jnp.sum(jnp.where(sels[kq], pos, 0.0), axis=-1, keepdims=True)
        pos_out = jnp.where(lane4 == kq, pk, pos_out)
    carry_ref[...] = carry_ref[...] + jnp.sum(maskf, axis=0, keepdims=True)
    idx_ref[...] = idx_out.astype(I32)
    gate_ref[...] = gate_out
    pos_ref[...] = pos_out.astype(I32)
    cnt_ref[...] = carry_ref[...].astype(I32)


def _outproj_router(o_hg, o_rw, x, wo_bf16, gf, w_router, b_router, tm):
    n = x.shape[0]
    tri = (lax.broadcasted_iota(I32, (tm, tm), 0) > lax.broadcasted_iota(I32, (tm, tm), 1)).astype(BF16)
    half = HG_H * HG_DV
    row = lambda i: (i, 0)
    fix = lambda i: (0, 0)
    return pl.pallas_call(
        _outproj_router_kernel,
        grid=(n // tm,),
        in_specs=[pl.BlockSpec((tm, half), row), pl.BlockSpec((tm, RW_W), row), pl.BlockSpec((tm, D_MODEL), row),
                  pl.BlockSpec((half + RW_W, D_MODEL), fix), pl.BlockSpec((1, D_MODEL), fix),
                  pl.BlockSpec((D_MODEL, N_EXPERTS), fix), pl.BlockSpec((1, N_EXPERTS), fix),
                  pl.BlockSpec((tm, tm), fix)],
        out_specs=[pl.BlockSpec((tm, D_MODEL), row), pl.BlockSpec((tm, D_MODEL), row),
                   pl.BlockSpec((tm, TOP_K), row), pl.BlockSpec((tm, TOP_K), row), pl.BlockSpec((tm, TOP_K), row),
                   pl.BlockSpec((1, N_EXPERTS), fix)],
        out_shape=[jax.ShapeDtypeStruct((n, D_MODEL), F32), jax.ShapeDtypeStruct((n, D_MODEL), F32),
                   jax.ShapeDtypeStruct((n, TOP_K), I32), jax.ShapeDtypeStruct((n, TOP_K), F32),
                   jax.ShapeDtypeStruct((n, TOP_K), I32), jax.ShapeDtypeStruct((1, N_EXPERTS), I32)],
        scratch_shapes=[pltpu.VMEM((1, N_EXPERTS), F32)],
        compiler_params=_cparams(("arbitrary",)),
        name="out_proj_router",
    )(o_hg, o_rw, x, wo_bf16, gf, w_router, b_router, tri)


def _plan_kernel(cnt_ref, idx_ref, pos_ref, dest_ref, bexp_ref, nblk_ref, *, n_blocks):
    cnt = cnt_ref[...].astype(F32)
    padded = jnp.floor((cnt + (MOE_BLOCK - 1)) * (1.0 / MOE_BLOCK)) * MOE_BLOCK
    ei = lax.broadcasted_iota(I32, (N_EXPERTS, N_EXPERTS), 0)
    ej = lax.broadcasted_iota(I32, (N_EXPERTS, N_EXPERTS), 1)
    upper = (ei < ej).astype(BF16)
    upper_incl = (ei <= ej).astype(BF16)
    start = _dot_sel_r(padded, upper)
    ends = _dot_sel_r(padded, upper_incl)
    idx = idx_ref[...]
    dest = pos_ref[...]
    for e in range(N_EXPERTS):
        dest = dest + jnp.where(idx == e, start[:, e:e + 1].astype(I32), 0)
    dest_ref[...] = dest
    bstart = (lax.broadcasted_iota(I32, (n_blocks, 1), 0) * MOE_BLOCK).astype(F32)
    be = jnp.sum(jnp.where(ends <= bstart, 1.0, 0.0), axis=-1, keepdims=True)
    bexp_ref[...] = jnp.minimum(be, N_EXPERTS - 1.0).astype(I32)
    nblk_ref[...] = (ends[:, N_EXPERTS - 1:] * (1.0 / MOE_BLOCK)).astype(I32)


def _plan(cnt, idx, pos, n_blocks, tm):
    n = idx.shape[0]
    row = lambda i: (i, 0)
    fix = lambda i: (0, 0)
    return pl.pallas_call(
        functools.partial(_plan_kernel, n_blocks=n_blocks),
        grid=(n // tm,),
        in_specs=[pl.BlockSpec((1, N_EXPERTS), fix), pl.BlockSpec((tm, TOP_K), row), pl.BlockSpec((tm, TOP_K), row)],
        out_specs=[pl.BlockSpec((tm, TOP_K), row), pl.BlockSpec((n_blocks, 1), fix), pl.BlockSpec((1, 1), fix)],
        out_shape=[jax.ShapeDtypeStruct((n, TOP_K), I32), jax.ShapeDtypeStruct((n_blocks, 1), I32),
                   jax.ShapeDtypeStruct((1, 1), I32)],
        compiler_params=_cparams(("arbitrary",)),
        name="moe_plan",
    )(cnt, idx, pos)


def _scatter_kernel(dest_ref, hf_ref, xs_in_ref, xs_ref, sem, *, ts):
    del xs_in_ref
    i = pl.program_id(0)
    t0 = i * ts

    def issue(t, carry):
        for kq in range(TOP_K):
            d = dest_ref[t * TOP_K + kq]
            pltpu.make_async_copy(hf_ref.at[pl.ds(t0 + t, 1)], xs_ref.at[pl.ds(d, 1)], sem).start()
        return carry

    lax.fori_loop(0, ts, issue, 0)

    def drain(t, carry):
        for kq in range(TOP_K):
            pltpu.make_async_copy(hf_ref.at[pl.ds(0, 1)], xs_ref.at[pl.ds(0, 1)], sem).wait()
        return carry

    lax.fori_loop(0, ts, drain, 0)


def _scatter(dest_flat, hf, xs_init, ts):
    n = hf.shape[0]
    return pl.pallas_call(
        functools.partial(_scatter_kernel, ts=ts),
        grid=(n // ts,),
        in_specs=[pl.BlockSpec((ts * TOP_K,), lambda i: (i,), memory_space=pltpu.SMEM),
                  pl.BlockSpec(memory_space=pl.ANY), pl.BlockSpec(memory_space=pl.ANY)],
        out_specs=pl.BlockSpec(memory_space=pl.ANY),
        out_shape=jax.ShapeDtypeStruct(xs_init.shape, F32),
        scratch_shapes=[pltpu.SemaphoreType.DMA(())],
        input_output_aliases={2: 0},
        compiler_params=_cparams(("arbitrary",)),
        name="moe_scatter",
    )(dest_flat, hf, xs_init)


def _ffn_kernel(bexp_ref, nblk_ref, xs_ref, wgu_ref, bgu_ref, wd_ref, bd_ref, yb_ref, wgu_bf, wd_bf):
    b = pl.program_id(0)
    valid = b < nblk_ref[0]
    prev_e = bexp_ref[jnp.maximum(b, 1) - 1]
    fresh = (b == 0) | (bexp_ref[b] != prev_e)

    @pl.when(valid & fresh)
    def _():
        wgu_bf[...] = wgu_ref[0].astype(BF16)
        wd_bf[...] = wd_ref[0].astype(BF16)

    @pl.when(valid)
    def _():
        x = xs_ref[...].astype(BF16)
        gu = _dot(x, wgu_bf[...]) + bgu_ref[0]
        gate = jnp.minimum(gu[:, :D_FF], SWIGLU_LIMIT)
        up = jnp.clip(gu[:, D_FF:], -SWIGLU_LIMIT, SWIGLU_LIMIT)
        glu = gate * _sigmoid(SWIGLU_ALPHA * gate)
        act = ((up + 1.0) * glu).astype(BF16)
        yb_ref[...] = _dot(act, wd_bf[...]) + bd_ref[0]

    @pl.when(jnp.logical_not(valid))
    def _():
        yb_ref[...] = jnp.zeros_like(yb_ref)


def _ffn(bexp, nblk, xs, w_gate_up, b_gate_up, w_down, b_down):
    n_rows = xs.shape[0]
    n_blocks = n_rows // MOE_BLOCK

    def blk(b, be, nb):
        return (jnp.minimum(b, jnp.maximum(nb[0], 1) - 1), 0)

    def wmap(b, be, nb):
        return (be[jnp.minimum(b, jnp.maximum(nb[0], 1) - 1)], 0, 0)

    grid_spec = pltpu.PrefetchScalarGridSpec(
        num_scalar_prefetch=2,
        grid=(n_blocks,),
        in_specs=[pl.BlockSpec((MOE_BLOCK, D_MODEL), blk),
                  pl.BlockSpec((1, D_MODEL, 2 * D_FF), wmap),
                  pl.BlockSpec((1, 1, 2 * D_FF), wmap),
                  pl.BlockSpec((1, D_FF, D_MODEL), wmap),
                  pl.BlockSpec((1, 1, D_MODEL), wmap)],
        out_specs=pl.BlockSpec((MOE_BLOCK, D_MODEL), lambda b, be, nb: (b, 0)),
        scratch_shapes=[pltpu.VMEM((D_MODEL, 2 * D_FF), BF16), pltpu.VMEM((D_FF, D_MODEL), BF16)],
    )
    return pl.pallas_call(
        _ffn_kernel,
        grid_spec=grid_spec,
        out_shape=jax.ShapeDtypeStruct((n_rows, D_MODEL), F32),
        compiler_params=_cparams(("arbitrary",)),
        name="moe_ffn",
    )(bexp, nblk, xs, w_gate_up, b_gate_up, w_down, b_down)


def _combine_kernel(dest_ref, gate_ref, x1_ref, gn_ref, yb_ref, y_ref, buf, sem, *, tc):
    def issue(t, carry):
        for kq in range(TOP_K):
            d = dest_ref[t * TOP_K + kq]
            pltpu.make_async_copy(yb_ref.at[pl.ds(d, 1)], buf.at[kq, pl.ds(t, 1)], sem).start()
        return carry

    lax.fori_loop(0, tc, issue, 0)

    def drain(t, carry):
        for kq in range(TOP_K):
            pltpu.make_async_copy(yb_ref.at[pl.ds(0, 1)], buf.at[0, pl.ds(0, 1)], sem).wait()
        return carry

    lax.fori_loop(0, tc, drain, 0)
    gate = gate_ref[...]
    acc = x1_ref[...]
    for kq in range(TOP_K):
        acc = acc + gate[:, kq:kq + 1] * buf[kq]
    y_ref[...] = _rms(acc, gn_ref[...])


def _combine(dest_flat, gate, x1, gn, yb, tc):
    n = x1.shape[0]
    return pl.pallas_call(
        functools.partial(_combine_kernel, tc=tc),
        grid=(n // tc,),
        in_specs=[pl.BlockSpec((tc * TOP_K,), lambda i: (i,), memory_space=pltpu.SMEM),
                  pl.BlockSpec((tc, TOP_K), lambda i: (i, 0)),
                  pl.BlockSpec((tc, D_MODEL), lambda i: (i, 0)),
                  pl.BlockSpec((1, D_MODEL), lambda i: (0, 0)),
                  pl.BlockSpec(memory_space=pl.ANY)],
        out_specs=pl.BlockSpec((tc, D_MODEL), lambda i: (i, 0)),
        out_shape=jax.ShapeDtypeStruct((n, D_MODEL), F32),
        scratch_shapes=[pltpu.VMEM((TOP_K, tc, D_MODEL), F32), pltpu.SemaphoreType.DMA(())],
        compiler_params=_cparams(("arbitrary",)),
        name="moe_combine",
    )(dest_flat, gate, x1, gn, yb)


def _pick_tile(n, pref):
    t = pref
    while n % t:
        t //= 2
    return t


def kernel(x_prompt, x_sample, state_hgrn, state_rwkv, state_shift, norm_mix, w_in, lb_logits, hg_norm_w, rw_mu,
           rw_w0, rw_w2, rw_a0, rw_a2, rw_g2, rw_k_k, rw_k_a, rw_r_k, rw_lnx_w, rw_lnx_b, w_out, norm_ffn,
           w_router, b_router, w_gate_up, b_gate_up, w_down, b_down, norm_final):
    bp, tp, d = x_prompt.shape
    bs, tsq, _ = x_sample.shape
    assert norm_mix.shape[0] == 1 and d == D_MODEL
    n_p, n_s = bp * tp, bs * tsq
    n = n_p + n_s
    x = jnp.concatenate([x_prompt.reshape(n_p, d), x_sample.reshape(n_s, d)], axis=0)

    w_in_b = w_in[0].astype(BF16)
    proj_hg, proj_rw = _proj(x, norm_mix, w_in_b, (HG_COLS, RW_COLS), True, _pick_tile(n, 256))
    ones_d = jnp.ones((1, d), F32)
    (prev_s,) = _proj(state_shift[0], ones_d, w_in_b[:, HG_COLS:], (RW_COLS,), False, _pick_tile(bs, 128))
    prev_p = jnp.zeros((bp, 1, RW_COLS), F32)
    x_last = jnp.concatenate([x_prompt[:, -1, :], x_sample[:, -1, :]], axis=0)
    shift = _rmsnorm_rows(x_last, norm_mix)

    zero_hg = jnp.zeros((bp, HG_H, HG_DK, HG_DV), F32)
    o_hg, hg_p = _hgrn(proj_hg, lb_logits, hg_norm_w, zero_hg, jnp.zeros((n, HG_H * HG_DV), F32),
                       row0=0, batch=bp, seq=tp,
                       nseq=1, tb=_pick_tile(tp, 256), c=HG_CHUNK, sub=HG_SUB)
    ns_h = _pick_tile(bs, 16)
    o_hg, hg_s = _hgrn(proj_hg, lb_logits, hg_norm_w, state_hgrn[0], o_hg, row0=n_p, batch=bs, seq=tsq,
                       nseq=ns_h, tb=tsq, c=tsq, sub=tsq)

    zpad = jnp.zeros((64, RW_W), F32)
    wwa = jnp.concatenate([jnp.concatenate([rw_w2[0], zpad], axis=1),
                           jnp.concatenate([zpad, rw_a2[0]], axis=1)], axis=0).astype(BF16)
    wts = (rw_mu, rw_w0, rw_a0, wwa, rw_g2[0].astype(BF16), rw_k_k, rw_k_a, rw_r_k.reshape(1, RW_W),
           rw_lnx_w, rw_lnx_b)
    zero_rw = jnp.zeros((bp, RW_H, RW_N, RW_N), F32)
    o_rw, rw_p = _rwkv(proj_rw, prev_p, zero_rw, wts, jnp.zeros((n, RW_W), F32),
                       row0=0, batch=bp, seq=tp, chain=True)
    o_rw, rw_s = _rwkv(proj_rw, prev_s.reshape(bs, 1, RW_COLS), state_rwkv[0], wts, o_rw,
                       row0=n_p, batch=bs, seq=tsq, chain=False)

    tm = _pick_tile(n, 512)
    x1, hf, idx, gate, pos, cnt = _outproj_router(o_hg, o_rw, x, w_out[0].astype(BF16), norm_ffn,
                                                  w_router[0], b_router, tm)

    n_blocks = -(-(n * TOP_K) // MOE_BLOCK) + N_EXPERTS
    dest, bexp, nblk = _plan(cnt, idx, pos, n_blocks, tm)
    dest_flat = dest.reshape(n * TOP_K)
    xs = _scatter(dest_flat, hf, jnp.zeros((n_blocks * MOE_BLOCK, d), F32), _pick_tile(n, 512))
    yb = _ffn(bexp.reshape(n_blocks), nblk.reshape(1), xs, w_gate_up[0], b_gate_up[0].reshape(N_EXPERTS, 1, 2 * D_FF),
              w_down[0], b_down[0].reshape(N_EXPERTS, 1, d))
    y = _combine(dest_flat, gate, x1, norm_final.reshape(1, d), yb, _pick_tile(n, 256))

    y_p = y[:n_p].reshape(bp, tp, d)
    y_s = y[n_p:].reshape(bs, tsq, d)
    return (y_p, y_s, hg_p[None], rw_p[None], shift[:bp][None], hg_s[None], rw_s[None], shift[bp:][None])
```

```python
import functools

import jax
import jax.numpy as jnp
from jax import lax
from jax.experimental import pallas as pl
from jax.experimental.pallas import tpu as pltpu

F32 = jnp.float32
BF16 = jnp.bfloat16
I32 = jnp.int32

D_MODEL = 1024
HG_H, HG_DK, HG_DV = 4, 128, 128
HG_CHUNK = 64
HG_SUB = 16
RW_H, RW_N = 8, 64
RW_W = RW_H * RW_N
RW_PAIRS = RW_H // 2
RW_TILE = 64
RW_CHUNK = 16
RW_ROWS_CHAIN = 256
RW_ROWS_SHORT = 128
HG_COLS = 2 * HG_H * HG_DK + 2 * HG_H * HG_DV
RW_COLS = 3 * RW_W + 64 + 64 + 128
N_EXPERTS = 32
TOP_K = 4
D_FF = D_MODEL
SWIGLU_LIMIT = 7.0
SWIGLU_ALPHA = 1.702
NORM_EPS = 1e-6
RW_GN_EPS = 64e-5
MOE_BLOCK = 256
VMEM_LIMIT = 56 * 1024 * 1024


def _dot(a, b):
    return jnp.dot(a, b, preferred_element_type=F32)


def _dot_nt(a, b):
    return lax.dot_general(a, b, (((1,), (1,)), ((), ())), preferred_element_type=F32)


def _dot_tn(a, b):
    return lax.dot_general(a, b, (((0,), (0,)), ((), ())), preferred_element_type=F32)


def _split3(x):
    x1 = x.astype(BF16)
    r1 = x - x1.astype(F32)
    x2 = r1.astype(BF16)
    x3 = (r1 - x2.astype(F32)).astype(BF16)
    return x1, x2, x3


def _dot_sel_l(m_bf16, x):
    x1, x2, x3 = _split3(x)
    return _dot(m_bf16, x1) + _dot(m_bf16, x2) + _dot(m_bf16, x3)


def _dot_sel_r(x, m_bf16):
    x1, x2, x3 = _split3(x)
    return _dot(x1, m_bf16) + _dot(x2, m_bf16) + _dot(x3, m_bf16)


def _sigmoid(x):
    return 1.0 / (1.0 + jnp.exp(-x))


def _rms(x, g):
    return x * lax.rsqrt(jnp.mean(x * x, axis=-1, keepdims=True) + NORM_EPS) * g


def _cparams(sem):
    return pltpu.CompilerParams(dimension_semantics=sem, vmem_limit_bytes=VMEM_LIMIT)


def _proj_kernel(x_ref, g_ref, w_ref, *o_refs, normalize, splits):
    x = x_ref[...]
    h = _rms(x, g_ref[...]) if normalize else x
    hb = h.astype(BF16)
    c0 = 0
    for o_ref, width in zip(o_refs, splits):
        o_ref[...] = _dot(hb, w_ref[:, c0:c0 + width])
        c0 += width


def _proj(x, g, w_bf16, splits, normalize, tm):
    n = x.shape[0]
    kern = functools.partial(_proj_kernel, normalize=normalize, splits=splits)
    return pl.pallas_call(
        kern,
        grid=(n // tm,),
        in_specs=[pl.BlockSpec((tm, D_MODEL), lambda i: (i, 0)),
                  pl.BlockSpec((1, D_MODEL), lambda i: (0, 0)),
                  pl.BlockSpec((D_MODEL, sum(splits)), lambda i: (0, 0))],
        out_specs=[pl.BlockSpec((tm, s), lambda i: (i, 0)) for s in splits],
        out_shape=[jax.ShapeDtypeStruct((n, s), F32) for s in splits],
        compiler_params=_cparams(("parallel",)),
        name="norm_in_proj" if normalize else "shift_proj",
    )(x, g, w_bf16)


def _rmsnorm_rows_kernel(x_ref, g_ref, o_ref):
    o_ref[...] = _rms(x_ref[...], g_ref[...])


def _rmsnorm_rows(x, g):
    return pl.pallas_call(
        _rmsnorm_rows_kernel,
        out_shape=jax.ShapeDtypeStruct(x.shape, F32),
        name="shift_norm",
    )(x, g)


def _hgrn_chunk(q, fp, iv, lb, st, tri, c, sub):
    qf = q * _sigmoid(q)
    f = lb + (1.0 - lb) * _sigmoid(fp)
    lf = jnp.log(f)
    kc = 1.0 - f
    g = _dot_sel_l(tri, lf)
    o_inter = _dot_nt((qf * jnp.exp(g)).astype(BF16), st.astype(BF16))
    ivb = iv.astype(BF16)
    lane = lax.broadcasted_iota(I32, (sub, sub), 1)
    row = lax.broadcasted_iota(I32, (sub, sub), 0)
    outs = []
    for i in range(c // sub):
        lo = i * sub
        gi, qi, ki = g[lo:lo + sub], qf[lo:lo + sub], kc[lo:lo + sub]
        a = jnp.zeros((sub, sub), F32)
        for s in range(sub):
            e = jnp.exp(jnp.minimum(gi - gi[s:s + 1, :], 0.0))
            col = jnp.sum(qi * (ki[s:s + 1, :] * e), axis=-1, keepdims=True)
            a = jnp.where(lane == s, col, a)
        a = jnp.where(row >= lane, a, 0.0)
        oi = o_inter[lo:lo + sub] + _dot(a.astype(BF16), ivb[lo:lo + sub])
        if i > 0:
            gr = g[lo - 1:lo, :]
            qd = qi * jnp.exp(gi - gr)
            kd = kc[:lo] * jnp.exp(gr - g[:lo])
            a_off = _dot_nt(qd.astype(BF16), kd.astype(BF16))
            oi = oi + _dot(a_off.astype(BF16), ivb[:lo])
        outs.append(oi)
    o = outs[0] if len(outs) == 1 else jnp.concatenate(outs, axis=0)
    gl = g[c - 1:c, :]
    kd = kc * jnp.exp(gl - g)
    st_new = st * jnp.exp(gl) + _dot_tn(ivb, kd.astype(BF16))
    return o, st_new


def _hgrn_kernel(q_ref, f_ref, i_ref, g_ref, lbl_ref, gw_ref, tri_ref, s0_ref, o_ref, so_ref, st_ref,
                 *, nseq, tb, c, sub):
    j = pl.program_id(1)
    nt = pl.num_programs(1)

    @pl.when(j == 0)
    def _():
        for s in range(nseq):
            for h in range(HG_H):
                st_ref[s * HG_H + h] = s0_ref[s, h].T

    lbl = lbl_ref[...]
    ex = jnp.exp(lbl - jnp.max(lbl, axis=0, keepdims=True))
    lb = ex[0:1, :] / jnp.sum(ex, axis=0, keepdims=True)
    gw = gw_ref[...]
    tri = tri_ref[...]
    n_chunks = tb // c

    def body(it, carry):
        s = it // n_chunks
        r0 = pl.multiple_of(it * c, c)
        rows = pl.ds(r0, c)
        for h in range(HG_H):
            hs = slice(h * HG_DK, (h + 1) * HG_DK)
            o, st_new = _hgrn_chunk(q_ref[rows, hs], f_ref[rows, hs], i_ref[rows, hs], lb[:, hs],
                                    st_ref[s * HG_H + h], tri, c, sub)
            st_ref[s * HG_H + h] = st_new
            o = o * lax.rsqrt(jnp.mean(o * o, axis=-1, keepdims=True) + NORM_EPS) * gw
            gin = g_ref[rows, hs]
            o_ref[rows, hs] = o * (gin * _sigmoid(gin))
        return carry

    lax.fori_loop(0, nseq * n_chunks, body, 0)

    @pl.when(j == nt - 1)
    def _():
        for s in range(nseq):
            for h in range(HG_H):
                so_ref[s, h] = st_ref[s * HG_H + h].T


def _hgrn(proj_hg, lb_logits, gn_w, s0, o_prev, *, row0, batch, seq, nseq, tb, c, sub):
    n = proj_hg.shape[0]
    rows = nseq * tb
    nt = seq // tb
    blk0 = row0 // rows
    hw = HG_H * HG_DK
    tri = (lax.broadcasted_iota(I32, (c, c), 0) >= lax.broadcasted_iota(I32, (c, c), 1)).astype(BF16)

    def rmap(col):
        return lambda b, j: (blk0 + b * nt + j, col)

    kern = functools.partial(_hgrn_kernel, nseq=nseq, tb=tb, c=c, sub=sub)
    in_specs = [pl.BlockSpec((rows, hw), rmap(0)), pl.BlockSpec((rows, hw), rmap(1)),
                pl.BlockSpec((rows, hw), rmap(2)), pl.BlockSpec((rows, hw), rmap(3)),
                pl.BlockSpec((lb_logits.shape[0], hw), lambda b, j: (0, 0)),
                pl.BlockSpec((1, HG_DV), lambda b, j: (0, 0)),
                pl.BlockSpec((c, c), lambda b, j: (0, 0)),
                pl.BlockSpec((nseq, HG_H, HG_DK, HG_DV), lambda b, j: (b, 0, 0, 0)),
                pl.BlockSpec(memory_space=pl.ANY)]
    args = [proj_hg, proj_hg, proj_hg, proj_hg, lb_logits, gn_w, tri, s0, o_prev]
    return pl.pallas_call(
        lambda *refs: kern(*refs[:8], *refs[9:]),
        grid=(batch // nseq, nt),
        in_specs=in_specs,
        out_specs=[pl.BlockSpec((rows, hw), lambda b, j: (blk0 + b * nt + j, 0)),
                   pl.BlockSpec((nseq, HG_H, HG_DK, HG_DV), lambda b, j: (b, 0, 0, 0))],
        out_shape=[jax.ShapeDtypeStruct((n, HG_H * HG_DV), F32),
                   jax.ShapeDtypeStruct((batch, HG_H, HG_DK, HG_DV), F32)],
        scratch_shapes=[pltpu.VMEM((nseq * HG_H, HG_DV, HG_DK), F32)],
        input_output_aliases={8: 0},
        compiler_params=_cparams(("parallel", "arbitrary")),
        name="hgrn2_mix",
    )(*args)


def _softplus(z):
    return jnp.maximum(z, 0.0) + jnp.log(1.0 + jnp.exp(-jnp.abs(z)))


def _rwkv_kernel(p_ref, prev_ref, s0_ref, mu_ref, w0_ref, a0_ref, wwa_ref, g2_ref, kkw_ref, kaw_ref, rk_ref,
                 lnw_ref, lnb_ref, g128_ref, trics_ref, esel_ref, o_ref, so_ref, st_ref, carry_ref,
                 *, c, chain, seq, rows):
    tt = RW_TILE
    ntile = rows // tt
    nc = tt // c
    j = pl.program_id(1)
    nt = pl.num_programs(1)
    lane128 = lax.broadcasted_iota(I32, (1, 128), 1)
    lo_half = lane128 < RW_N

    def block_diag(se, so):
        z = jnp.zeros((RW_N, RW_N), F32)
        return jnp.concatenate([jnp.concatenate([se, z], axis=1), jnp.concatenate([z, so], axis=1)], axis=0)

    def split2(z):
        hi = z.astype(BF16)
        return hi, (z - hi.astype(F32)).astype(BF16)

    p = p_ref[...]
    rolled = pltpu.roll(p, 1, 0)
    rowi = lax.broadcasted_iota(I32, (rows, 1), 0)
    if chain:
        @pl.when(j == 0)
        def _():
            carry_ref[...] = prev_ref[0]
            for pr in range(RW_PAIRS):
                st_ref[pr] = block_diag(s0_ref[0, 2 * pr], s0_ref[0, 2 * pr + 1])

        ps = jnp.where(rowi == 0, carry_ref[...], rolled)
        carry_ref[...] = p[rows - 1:rows, :]
    else:
        first = _dot_sel_l(esel_ref[...], prev_ref[...])
        ps = jnp.where((rowi & (seq - 1)) == 0, first, rolled)
    pm = p + (ps - p) * mu_ref[...]

    r = pm[:, 0:RW_W]
    k = pm[:, RW_W:2 * RW_W]
    v = pm[:, 2 * RW_W:3 * RW_W]
    wa = pm[:, 3 * RW_W:3 * RW_W + 128]
    g_lo = pm[:, 3 * RW_W + 128:3 * RW_W + 256]
    wa = jnp.where(lo_half, jnp.tanh(wa), wa)
    xwa = _dot(wa.astype(BF16), wwa_ref[...])
    log_w = -_softplus(-(w0_ref[...] + xwa[:, :RW_W])) - 0.5
    ld = -jnp.exp(log_w)
    a = _sigmoid(a0_ref[...] + xwa[:, RW_W:])
    gate = _dot(_sigmoid(g_lo).astype(BF16), g2_ref[...])
    g128 = g128_ref[...]

    def gsum(z):
        zs = jnp.concatenate([z[:, i * 128:(i + 1) * 128] for i in range(RW_PAIRS)], axis=0)
        hi, lo = split2(zs)
        s = _dot(hi, g128) + _dot(lo, g128)
        return jnp.concatenate([s[i * rows:(i + 1) * rows] for i in range(RW_PAIRS)], axis=1)

    kkv = k * kkw_ref[...]
    kk = kkv / jnp.maximum(jnp.sqrt(gsum(kkv * kkv)), 1e-12)
    k2 = k * (1.0 + (a - 1.0) * kaw_ref[...])
    beta = kk * a

    ld_hi, ld_lo = split2(ld)
    cs = _dot(trics_ref[...], ld_hi) + _dot(trics_ref[...], ld_lo)
    gc = cs[:rows]
    gtot = cs[rows:]
    e_g = jnp.exp(gc)
    e_ng = jnp.exp(-gc)
    e_l = jnp.exp(gtot - gc)
    ah = -kk * jnp.exp(gc - ld)
    rh = r * e_g
    bh = beta * e_ng
    kh = k2 * e_ng
    bt = beta * e_l
    kt = k2 * e_l
    e_tot = jnp.exp(gtot)

    ri = lax.broadcasted_iota(I32, (2 * tt, 2 * tt), 0)
    ci = lax.broadcasted_iota(I32, (2 * tt, 2 * tt), 1)
    same = (ri // c) == (ci // c)
    m_strict = jnp.where(same & (ci < ri), 1.0, 0.0)
    m_incl = jnp.where(same & (ci <= ri), 1.0, 0.0)
    eye2 = jnp.where(ri == ci, 1.0, 0.0)
    bd = jnp.where((ri // RW_N) == (ci // RW_N), 1.0, 0.0)

    def stack(z):
        return jnp.concatenate([jnp.where(lo_half, z, 0.0), jnp.where(lo_half, 0.0, z)], axis=0)

    def fold(z):
        return z[:tt] + z[tt:]

    rowc = lax.broadcasted_iota(I32, (tt, 1), 0) // c

    def chunk_expand(z):
        return jnp.concatenate([jnp.where(rowc == ch, z, 0.0) for ch in range(nc)], axis=1).astype(BF16)

    sts = [st_ref[pr] for pr in range(RW_PAIRS)] if chain else None
    y_tiles = []
    for t, pr in [(t, pr) for t in range(ntile) for pr in range(RW_PAIRS)]:
        if pr == 0:
            ys = []
        tr = slice(t * tt, (t + 1) * tt)
        sl = slice(pr * 128, (pr + 1) * 128)
        xa, xr, xb, xk, xv = (stack(z[tr, sl]) for z in (ah, rh, bh, kh, v))
        xab, xrb, xbb, xkb, xvb = (z.astype(BF16) for z in (xa, xr, xb, xk, xv))
        a_ab = _dot_nt(xab, xbb) * m_strict
        a_ak = _dot_nt(xab, xkb) * m_strict
        a_rb = (_dot_nt(xrb, xbb) * m_incl).astype(BF16)
        a_rk = (_dot_nt(xrb, xkb) * m_incl).astype(BF16)
        tm = eye2 + a_ab
        pw = a_ab
        n = 1
        while 2 * n < c:
            pwb = pw.astype(BF16)
            pw = _dot(pwb, pwb)
            tm = tm + _dot(tm.astype(BF16), pw.astype(BF16))
            n *= 2
        tmb = tm.astype(BF16)
        ta = _dot(tmb, xab)
        tv = _dot(tmb, _dot(a_ak.astype(BF16), xvb).astype(BF16))
        tab, tvb = ta.astype(BF16), tv.astype(BF16)
        rp = fold(xr + _dot(a_rb, tab))
        y0 = fold(_dot(a_rb, tvb) + _dot(a_rk, xvb))
        ap = fold(ta).astype(BF16)
        vp = fold(tv).astype(BF16)
        bexp = chunk_expand(bt[tr, sl])
        kexp = chunk_expand(kt[tr, sl])
        vpl = v[tr, sl].astype(BF16)
        rpb = rp.astype(BF16)
        mt_all = _dot_tn(ap, bexp)
        ht_all = _dot_tn(jnp.concatenate([vp, vpl], axis=0), jnp.concatenate([bexp, kexp], axis=0))
        if chain:
            st = sts[pr]
        y_chunks = []
        for ch in range(nc):
            rs = slice(ch * c, (ch + 1) * c)
            cl = slice(ch * 128, (ch + 1) * 128)
            r0 = t * tt + ch * c
            mt = mt_all[:, cl] * bd + eye2 * e_tot[r0:r0 + 1, sl]
            ht = ht_all[:, cl] * bd
            if not chain:
                sq = t * nc + ch
                st = block_diag(s0_ref[sq, 2 * pr], s0_ref[sq, 2 * pr + 1])
            stb = st.astype(BF16)
            y_chunks.append(_dot_nt(rpb[rs], stb) + y0[rs])
            st = _dot(stb, mt.astype(BF16)) + ht
            if not chain:
                so_ref[sq, 2 * pr] = st[:RW_N, :RW_N]
                so_ref[sq, 2 * pr + 1] = st[RW_N:, RW_N:]
        if chain:
            sts[pr] = st
        ys.append(jnp.concatenate(y_chunks, axis=0))
        if pr == RW_PAIRS - 1:
            y_tiles.append(jnp.concatenate(ys, axis=1))
    y = y_tiles[0] if ntile == 1 else jnp.concatenate(y_tiles, axis=0)

    if chain:
        for pr in range(RW_PAIRS):
            st_ref[pr] = sts[pr]

        @pl.when(j == nt - 1)
        def _():
            for pr in range(RW_PAIRS):
                st = st_ref[pr]
                so_ref[0, 2 * pr] = st[:RW_N, :RW_N]
                so_ref[0, 2 * pr + 1] = st[RW_N:, RW_N:]

    inv_n = 1.0 / RW_N
    mean = gsum(y) * inv_n
    yc = y - mean
    var = gsum(yc * yc) * inv_n
    yn = yc * lax.rsqrt(var + RW_GN_EPS) * lnw_ref[...] + lnb_ref[...]
    bonus = gsum(r * k2 * rk_ref[...]) * v
    o_ref[...] = (yn + bonus) * gate


def _rwkv(proj_rw, prev, s0, wts, o_prev, *, row0, batch, seq, chain):
    n = proj_rw.shape[0]
    if chain:
        rows = _pick_tile(seq, RW_ROWS_CHAIN)
        nseq, c, nt = 1, RW_CHUNK, seq // rows
        prev_spec = pl.BlockSpec((1, 1, RW_COLS), lambda b, j: (b, 0, 0))
    else:
        assert seq & (seq - 1) == 0 and RW_TILE % seq == 0
        rows = _pick_tile(batch * seq, RW_ROWS_SHORT)
        nseq, c, nt = rows // seq, seq, 1
        prev_spec = pl.BlockSpec((nseq, RW_COLS), lambda b, j: (b, 0))
    blk0 = row0 // rows
    ri = lax.broadcasted_iota(I32, (rows, rows), 0)
    ci = lax.broadcasted_iota(I32, (rows, rows), 1)
    same = (ri // c) == (ci // c)
    trics = jnp.concatenate([same & (ci <= ri), same], axis=0).astype(BF16)
    gi = lax.broadcasted_iota(I32, (128, 128), 0) // RW_N
    gj = lax.broadcasted_iota(I32, (128, 128), 1) // RW_N
    g128 = (gi == gj).astype(BF16)
    esel = (lax.broadcasted_iota(I32, (rows, nseq), 0) // seq
            == lax.broadcasted_iota(I32, (rows, nseq), 1)).astype(BF16)
    (mu, w0, a0, wwa, g2, kkw, kaw, rk, lnw, lnb) = wts

    def full(arr):
        return pl.BlockSpec(arr.shape, lambda b, j: (0,) * arr.ndim)

    consts = [mu, w0, a0, wwa, g2, kkw, kaw, rk, lnw, lnb, g128, trics, esel]
    in_specs = [pl.BlockSpec((rows, RW_COLS), lambda b, j: (blk0 + b * nt + j, 0)),
                prev_spec,
                pl.BlockSpec((nseq, RW_H, RW_N, RW_N), lambda b, j: (b, 0, 0, 0))]
    in_specs += [full(t) for t in consts] + [pl.BlockSpec(memory_space=pl.ANY)]
    args = [proj_rw, prev, s0] + consts + [o_prev]
    kern = functools.partial(_rwkv_kernel, c=c, chain=chain, seq=seq, rows=rows)
    n_in = len(args) - 1
    return pl.pallas_call(
        lambda *refs: kern(*refs[:n_in], *refs[n_in + 1:]),
        grid=(batch // nseq, nt),
        in_specs=in_specs,
        out_specs=[pl.BlockSpec((rows, RW_W), lambda b, j: (blk0 + b * nt + j, 0)),
                   pl.BlockSpec((nseq, RW_H, RW_N, RW_N), lambda b, j: (b, 0, 0, 0))],
        out_shape=[jax.ShapeDtypeStruct((n, RW_W), F32),
                   jax.ShapeDtypeStruct((batch, RW_H, RW_N, RW_N), F32)],
        scratch_shapes=[pltpu.VMEM((RW_PAIRS, 128, 128), F32), pltpu.VMEM((1, RW_COLS), F32)],
        input_output_aliases={n_in: 0},
        compiler_params=_cparams(("parallel", "arbitrary")),
        name="rwkv7_mix",
    )(*args)


def _outproj_router_kernel(ohg_ref, orw_ref, x_ref, wo_ref, gf_ref, wr_ref, br_ref, tri_ref,
                           x1_ref, hf_ref, idx_ref, gate_ref, pos_ref, cnt_ref, carry_ref):
    i = pl.program_id(0)

    @pl.when(i == 0)
    def _():
        carry_ref[...] = jnp.zeros_like(carry_ref)

    half = HG_H * HG_DV
    mixed = _dot(ohg_ref[...].astype(BF16), wo_ref[:half, :]) + _dot(orw_ref[...].astype(BF16), wo_ref[half:, :])
    x1 = x_ref[...] + mixed
    x1_ref[...] = x1
    hf = _rms(x1, gf_ref[...])
    hf_ref[...] = hf

    h1, h2, _ = _split3(hf)
    w1, w2, _ = _split3(wr_ref[...])
    logits = _dot(h1, w1) + _dot(h1, w2) + _dot(h2, w1) + br_ref[...]

    tm = logits.shape[0]
    lane = lax.broadcasted_iota(I32, (tm, N_EXPERTS), 1).astype(F32)
    lane4 = lax.broadcasted_iota(I32, (tm, TOP_K), 1)
    vals = logits
    maskf = jnp.zeros((tm, N_EXPERTS), F32)
    sels, tops = [], []
    idx_out = jnp.zeros((tm, TOP_K), F32)
    for kq in range(TOP_K):
        m = jnp.max(vals, axis=-1, keepdims=True)
        idx = jnp.min(jnp.where(vals == m, lane, float(N_EXPERTS)), axis=-1, keepdims=True)
        sel = lane == idx
        sels.append(sel)
        tops.append(m)
        idx_out = jnp.where(lane4 == kq, idx, idx_out)
        vals = jnp.where(sel, -jnp.inf, vals)
        maskf = maskf + jnp.where(sel, 1.0, 0.0)
    es = [jnp.exp(t - tops[0]) for t in tops]
    denom = es[0] + es[1] + es[2] + es[3]
    gate_out = jnp.zeros((tm, TOP_K), F32)
    for kq in range(TOP_K):
        gate_out = jnp.where(lane4 == kq, es[kq] / denom, gate_out)

    pos = _dot(tri_ref[...], maskf.astype(BF16)) + carry_ref[...]
    pos_out = jnp.zeros((tm, TOP_K), F32)
    for kq in range(TOP_K):
        pk = jnp.sum(jnp.where(sels[kq], pos, 0.0), axis=-1, keepdims=True)
        pos_out = jnp.where(lane4 == kq, pk, pos_out)
    carry_ref[...] = carry_ref[...] + jnp.sum(maskf, axis=0, keepdims=True)
    idx_ref[...] = idx_out.astype(I32)
    gate_ref[...] = gate_out
    pos_ref[...] = pos_out.astype(I32)
    cnt_ref[...] = carry_ref[...].astype(I32)


def _outproj_router(o_hg, o_rw, x, wo_bf16, gf, w_router, b_router, tm):
    n = x.shape[0]
    tri = (lax.broadcasted_iota(I32, (tm, tm), 0) > lax.broadcasted_iota(I32, (tm, tm), 1)).astype(BF16)
    half = HG_H * HG_DV
    row = lambda i: (i, 0)
    fix = lambda i: (0, 0)
    return pl.pallas_call(
        _outproj_router_kernel,
        grid=(n // tm,),
        in_specs=[pl.BlockSpec((tm, half), row), pl.BlockSpec((tm, RW_W), row), pl.BlockSpec((tm, D_MODEL), row),
                  pl.BlockSpec((half + RW_W, D_MODEL), fix), pl.BlockSpec((1, D_MODEL), fix),
                  pl.BlockSpec((D_MODEL, N_EXPERTS), fix), pl.BlockSpec((1, N_EXPERTS), fix),
                  pl.BlockSpec((tm, tm), fix)],
        out_specs=[pl.BlockSpec((tm, D_MODEL), row), pl.BlockSpec((tm, D_MODEL), row),
                   pl.BlockSpec((tm, TOP_K), row), pl.BlockSpec((tm, TOP_K), row), pl.BlockSpec((tm, TOP_K), row),
                   pl.BlockSpec((1, N_EXPERTS), fix)],
        out_shape=[jax.ShapeDtypeStruct((n, D_MODEL), F32), jax.ShapeDtypeStruct((n, D_MODEL), F32),
                   jax.ShapeDtypeStruct((n, TOP_K), I32), jax.ShapeDtypeStruct((n, TOP_K), F32),
                   jax.ShapeDtypeStruct((n, TOP_K), I32), jax.ShapeDtypeStruct((1, N_EXPERTS), I32)],
        scratch_shapes=[pltpu.VMEM((1, N_EXPERTS), F32)],
        compiler_params=_cparams(("arbitrary",)),
        name="out_proj_router",
    )(o_hg, o_rw, x, wo_bf16, gf, w_router, b_router, tri)


def _plan_kernel(cnt_ref, idx_ref, pos_ref, dest_ref, bexp_ref, nblk_ref, *, n_blocks):
    cnt = cnt_ref[...].astype(F32)
    padded = jnp.floor((cnt + (MOE_BLOCK - 1)) * (1.0 / MOE_BLOCK)) * MOE_BLOCK
    ei = lax.broadcasted_iota(I32, (N_EXPERTS, N_EXPERTS), 0)
    ej = lax.broadcasted_iota(I32, (N_EXPERTS, N_EXPERTS), 1)
    upper = (ei < ej).astype(BF16)
    upper_incl = (ei <= ej).astype(BF16)
    start = _dot_sel_r(padded, upper)
    ends = _dot_sel_r(padded, upper_incl)
    idx = idx_ref[...]
    dest = pos_ref[...]
    for e in range(N_EXPERTS):
        dest = dest + jnp.where(idx == e, start[:, e:e + 1].astype(I32), 0)
    dest_ref[...] = dest
    bstart = (lax.broadcasted_iota(I32, (n_blocks, 1), 0) * MOE_BLOCK).astype(F32)
    be = jnp.sum(jnp.where(ends <= bstart, 1.0, 0.0), axis=-1, keepdims=True)
    bexp_ref[...] = jnp.minimum(be, N_EXPERTS - 1.0).astype(I32)
    nblk_ref[...] = (ends[:, N_EXPERTS - 1:] * (1.0 / MOE_BLOCK)).astype(I32)


def _plan(cnt, idx, pos, n_blocks, tm):
    n = idx.shape[0]
    row = lambda i: (i, 0)
    fix = lambda i: (0, 0)
    return pl.pallas_call(
        functools.partial(_plan_kernel, n_blocks=n_blocks),
        grid=(n // tm,),
        in_specs=[pl.BlockSpec((1, N_EXPERTS), fix), pl.BlockSpec((tm, TOP_K), row), pl.BlockSpec((tm, TOP_K), row)],
        out_specs=[pl.BlockSpec((tm, TOP_K), row), pl.BlockSpec((n_blocks, 1), fix), pl.BlockSpec((1, 1), fix)],
        out_shape=[jax.ShapeDtypeStruct((n, TOP_K), I32), jax.ShapeDtypeStruct((n_blocks, 1), I32),
                   jax.ShapeDtypeStruct((1, 1), I32)],
        compiler_params=_cparams(("arbitrary",)),
        name="moe_plan",
    )(cnt, idx, pos)


def _scatter_kernel(dest_ref, hf_ref, xs_in_ref, xs_ref, sem, *, ts):
    del xs_in_ref

    def issue(t, carry):
        for kq in range(TOP_K):
            d = dest_ref[t * TOP_K + kq]
            pltpu.make_async_copy(hf_ref.at[pl.ds(t, 1)], xs_ref.at[pl.ds(d, 1)], sem).start()
        return carry

    lax.fori_loop(0, ts, issue, 0)

    def drain(t, carry):
        for kq in range(TOP_K):
            pltpu.make_async_copy(hf_ref.at[pl.ds(0, 1)], xs_ref.at[pl.ds(0, 1)], sem).wait()
        return carry

    lax.fori_loop(0, ts, drain, 0)


def _scatter(dest_flat, hf, xs_init, ts):
    n = hf.shape[0]
    return pl.pallas_call(
        functools.partial(_scatter_kernel, ts=ts),
        grid=(n // ts,),
        in_specs=[pl.BlockSpec((ts * TOP_K,), lambda i: (i,), memory_space=pltpu.SMEM),
                  pl.BlockSpec((ts, D_MODEL), lambda i: (i, 0)), pl.BlockSpec(memory_space=pl.ANY)],
        out_specs=pl.BlockSpec(memory_space=pl.ANY),
        out_shape=jax.ShapeDtypeStruct(xs_init.shape, F32),
        scratch_shapes=[pltpu.SemaphoreType.DMA(())],
        input_output_aliases={2: 0},
        compiler_params=_cparams(("arbitrary",)),
        name="moe_scatter",
    )(dest_flat, hf, xs_init)


def _ffn_kernel(bexp_ref, nblk_ref, xs_ref, wgu_ref, bgu_ref, wd_ref, bd_ref, yb_ref, wgu_bf, wd_bf):
    b = pl.program_id(0)
    valid = b < nblk_ref[0]
    prev_e = bexp_ref[jnp.maximum(b, 1) - 1]
    fresh = (b == 0) | (bexp_ref[b] != prev_e)

    @pl.when(valid & fresh)
    def _():
        wgu_bf[...] = wgu_ref[0].astype(BF16)
        wd_bf[...] = wd_ref[0].astype(BF16)

    @pl.when(valid)
    def _():
        x = xs_ref[...].astype(BF16)
        gu = _dot(x, wgu_bf[...]) + bgu_ref[0]
        gate = jnp.minimum(gu[:, :D_FF], SWIGLU_LIMIT)
        up = jnp.clip(gu[:, D_FF:], -SWIGLU_LIMIT, SWIGLU_LIMIT)
        glu = gate * _sigmoid(SWIGLU_ALPHA * gate)
        act = ((up + 1.0) * glu).astype(BF16)
        yb_ref[...] = _dot(act, wd_bf[...]) + bd_ref[0]

    @pl.when(jnp.logical_not(valid))
    def _():
        yb_ref[...] = jnp.zeros_like(yb_ref)


def _ffn(bexp, nblk, xs, w_gate_up, b_gate_up, w_down, b_down):
    n_rows = xs.shape[0]
    n_blocks = n_rows // MOE_BLOCK

    def blk(b, be, nb):
        return (jnp.minimum(b, jnp.maximum(nb[0], 1) - 1), 0)

    def wmap(b, be, nb):
        return (be[jnp.minimum(b, jnp.maximum(nb[0], 1) - 1)], 0, 0)

    grid_spec = pltpu.PrefetchScalarGridSpec(
        num_scalar_prefetch=2,
        grid=(n_blocks,),
        in_specs=[pl.BlockSpec((MOE_BLOCK, D_MODEL), blk),
                  pl.BlockSpec((1, D_MODEL, 2 * D_FF), wmap),
                  pl.BlockSpec((1, 1, 2 * D_FF), wmap),
                  pl.BlockSpec((1, D_FF, D_MODEL), wmap),
                  pl.BlockSpec((1, 1, D_MODEL), wmap)],
        out_specs=pl.BlockSpec((MOE_BLOCK, D_MODEL), lambda b, be, nb: (b, 0)),
        scratch_shapes=[pltpu.VMEM((D_MODEL, 2 * D_FF), BF16), pltpu.VMEM((D_FF, D_MODEL), BF16)],
    )
    return pl.pallas_call(
        _ffn_kernel,
        grid_spec=grid_spec,
        out_shape=jax.ShapeDtypeStruct((n_rows, D_MODEL), F32),
        compiler_params=_cparams(("arbitrary",)),
        name="moe_ffn",
    )(bexp, nblk, xs, w_gate_up, b_gate_up, w_down, b_down)


def _combine_kernel(dest_ref, gate_ref, x1_ref, gn_ref, yb_ref, y_ref, buf, sem, *, tc):
    def issue(t, carry):
        for kq in range(TOP_K):
            d = dest_ref[t * TOP_K + kq]
            pltpu.make_async_copy(yb_ref.at[pl.ds(d, 1)], buf.at[kq, pl.ds(t, 1)], sem).start()
        return carry

    lax.fori_loop(0, tc, issue, 0)

    def drain(t, carry):
        for kq in range(TOP_K):
            pltpu.make_async_copy(yb_ref.at[pl.ds(0, 1)], buf.at[0, pl.ds(0, 1)], sem).wait()
        return carry

    lax.fori_loop(0, tc, drain, 0)
    gate = gate_ref[...]
    acc = x1_ref[...]
    for kq in range(TOP_K):
        acc = acc + gate[:, kq:kq + 1] * buf[kq]
    y_ref[...] = _rms(acc, gn_ref[...])


def _combine(dest_flat, gate, x1, gn, yb, tc):
    n = x1.shape[0]
    return pl.pallas_call(
        functools.partial(_combine_kernel, tc=tc),
        grid=(n // tc,),
        in_specs=[pl.BlockSpec((tc * TOP_K,), lambda i: (i,), memory_space=pltpu.SMEM),
                  pl.BlockSpec((tc, TOP_K), lambda i: (i, 0)),
                  pl.BlockSpec((tc, D_MODEL), lambda i: (i, 0)),
                  pl.BlockSpec((1, D_MODEL), lambda i: (0, 0)),
                  pl.BlockSpec(memory_space=pl.ANY)],
        out_specs=pl.BlockSpec((tc, D_MODEL), lambda i: (i, 0)),
        out_shape=jax.ShapeDtypeStruct((n, D_MODEL), F32),
        scratch_shapes=[pltpu.VMEM((TOP_K, tc, D_MODEL), F32), pltpu.SemaphoreType.DMA(())],
        compiler_params=_cparams(("arbitrary",)),
        name="moe_combine",
    )(dest_flat, gate, x1, gn, yb)


def _pick_tile(n, pref):
    t = pref
    while n % t:
        t //= 2
    return t


def kernel(x_prompt, x_sample, state_hgrn, state_rwkv, state_shift, norm_mix, w_in, lb_logits, hg_norm_w, rw_mu,
           rw_w0, rw_w2, rw_a0, rw_a2, rw_g2, rw_k_k, rw_k_a, rw_r_k, rw_lnx_w, rw_lnx_b, w_out, norm_ffn,
           w_router, b_router, w_gate_up, b_gate_up, w_down, b_down, norm_final):
    bp, tp, d = x_prompt.shape
    bs, tsq, _ = x_sample.shape
    assert norm_mix.shape[0] == 1 and d == D_MODEL
    n_p, n_s = bp * tp, bs * tsq
    n = n_p + n_s
    x = jnp.concatenate([x_prompt.reshape(n_p, d), x_sample.reshape(n_s, d)], axis=0)

    w_in_b = w_in[0].astype(BF16)
    proj_hg, proj_rw = _proj(x, norm_mix, w_in_b, (HG_COLS, RW_COLS), True, _pick_tile(n, 256))
    ones_d = jnp.ones((1, d), F32)
    (prev_s,) = _proj(state_shift[0], ones_d, w_in_b[:, HG_COLS:], (RW_COLS,), False, _pick_tile(bs, 128))
    prev_p = jnp.zeros((bp, 1, RW_COLS), F32)
    x_last = jnp.concatenate([x_prompt[:, -1, :], x_sample[:, -1, :]], axis=0)
    shift = _rmsnorm_rows(x_last, norm_mix)

    zero_hg = jnp.zeros((bp, HG_H, HG_DK, HG_DV), F32)
    o_hg, hg_p = _hgrn(proj_hg, lb_logits, hg_norm_w, zero_hg, jnp.zeros((n, HG_H * HG_DV), F32),
                       row0=0, batch=bp, seq=tp,
                       nseq=1, tb=_pick_tile(tp, 256), c=HG_CHUNK, sub=HG_SUB)
    ns_h = _pick_tile(bs, 16)
    o_hg, hg_s = _hgrn(proj_hg, lb_logits, hg_norm_w, state_hgrn[0], o_hg, row0=n_p, batch=bs, seq=tsq,
                       nseq=ns_h, tb=tsq, c=tsq, sub=tsq)

    zpad = jnp.zeros((64, RW_W), F32)
    wwa = jnp.concatenate([jnp.concatenate([rw_w2[0], zpad], axis=1),
                           jnp.concatenate([zpad, rw_a2[0]], axis=1)], axis=0).astype(BF16)
    wts = (rw_mu, rw_w0, rw_a0, wwa, rw_g2[0].astype(BF16), rw_k_k, rw_k_a, rw_r_k.reshape(1, RW_W),
           rw_lnx_w, rw_lnx_b)
    zero_rw = jnp.zeros((bp, RW_H, RW_N, RW_N), F32)
    o_rw, rw_p = _rwkv(proj_rw, prev_p, zero_rw, wts, jnp.zeros((n, RW_W), F32),
                       row0=0, batch=bp, seq=tp, chain=True)
    o_rw, rw_s = _rwkv(proj_rw, prev_s, state_rwkv[0], wts, o_rw,
                       row0=n_p, batch=bs, seq=tsq, chain=False)

    tm = _pick_tile(n, 512)
    x1, hf, idx, gate, pos, cnt = _outproj_router(o_hg, o_rw, x, w_out[0].astype(BF16), norm_ffn,
                                                  w_router[0], b_router, tm)

    n_blocks = -(-(n * TOP_K) // MOE_BLOCK) + N_EXPERTS
    dest, bexp, nblk = _plan(cnt, idx, pos, n_blocks, tm)
    dest_flat = dest.reshape(n * TOP_K)
    xs = _scatter(dest_flat, hf, jnp.zeros((n_blocks * MOE_BLOCK, d), F32), _pick_tile(n, 512))
    yb = _ffn(bexp.reshape(n_blocks), nblk.reshape(1), xs, w_gate_up[0], b_gate_up[0].reshape(N_EXPERTS, 1, 2 * D_FF),
              w_down[0], b_down[0].reshape(N_EXPERTS, 1, d))
    y = _combine(dest_flat, gate, x1, norm_final.reshape(1, d), yb, _pick_tile(n, 256))

    y_p = y[:n_p].reshape(bp, tp, d)
    y_s = y[n_p:].reshape(bs, tsq, d)
    return (y_p, y_s, hg_p[None], rw_p[None], shift[:bp][None], hg_s[None], rw_s[None], shift[bp:][None])
```

```python
import functools

import jax
import jax.numpy as jnp
from jax import lax
from jax.experimental import pallas as pl
from jax.experimental.pallas import tpu as pltpu

F32 = jnp.float32
BF16 = jnp.bfloat16
I32 = jnp.int32

D_MODEL = 1024
HG_H, HG_DK, HG_DV = 4, 128, 128
HG_CHUNK = 64
HG_SUB = 16
RW_H, RW_N = 8, 64
RW_W = RW_H * RW_N
RW_PAIRS = RW_H // 2
RW_TILE = 64
RW_CHUNK = 16
RW_ROWS_CHAIN = 256
RW_ROWS_SHORT = 128
HG_COLS = 2 * HG_H * HG_DK + 2 * HG_H * HG_DV
RW_COLS = 3 * RW_W + 64 + 64 + 128
N_EXPERTS = 32
TOP_K = 4
D_FF = D_MODEL
SWIGLU_LIMIT = 7.0
SWIGLU_ALPHA = 1.702
NORM_EPS = 1e-6
RW_GN_EPS = 64e-5
MOE_BLOCK = 256
VMEM_LIMIT = 56 * 1024 * 1024


def _dot(a, b):
    return jnp.dot(a, b, preferred_element_type=F32)


def _dot_nt(a, b):
    return lax.dot_general(a, b, (((1,), (1,)), ((), ())), preferred_element_type=F32)


def _dot_tn(a, b):
    return lax.dot_general(a, b, (((0,), (0,)), ((), ())), preferred_element_type=F32)


def _split3(x):
    x1 = x.astype(BF16)
    r1 = x - x1.astype(F32)
    x2 = r1.astype(BF16)
    x3 = (r1 - x2.astype(F32)).astype(BF16)
    return x1, x2, x3


def _dot_sel_l(m_bf16, x):
    x1, x2, x3 = _split3(x)
    return _dot(m_bf16, x1) + _dot(m_bf16, x2) + _dot(m_bf16, x3)


def _dot_sel_r(x, m_bf16):
    x1, x2, x3 = _split3(x)
    return _dot(x1, m_bf16) + _dot(x2, m_bf16) + _dot(x3, m_bf16)


def _sigmoid(x):
    return 1.0 / (1.0 + jnp.exp(-x))


def _rms(x, g):
    return x * lax.rsqrt(jnp.mean(x * x, axis=-1, keepdims=True) + NORM_EPS) * g


def _cparams(sem):
    return pltpu.CompilerParams(dimension_semantics=sem, vmem_limit_bytes=VMEM_LIMIT)


def _proj_kernel(x_ref, g_ref, w_ref, *o_refs, normalize, splits):
    x = x_ref[...]
    h = _rms(x, g_ref[...]) if normalize else x
    hb = h.astype(BF16)
    c0 = 0
    for o_ref, width in zip(o_refs, splits):
        o_ref[...] = _dot(hb, w_ref[:, c0:c0 + width])
        c0 += width


def _proj(x, g, w_bf16, splits, normalize, tm):
    n = x.shape[0]
    kern = functools.partial(_proj_kernel, normalize=normalize, splits=splits)
    return pl.pallas_call(
        kern,
        grid=(n // tm,),
        in_specs=[pl.BlockSpec((tm, D_MODEL), lambda i: (i, 0)),
                  pl.BlockSpec((1, D_MODEL), lambda i: (0, 0)),
                  pl.BlockSpec((D_MODEL, sum(splits)), lambda i: (0, 0))],
        out_specs=[pl.BlockSpec((tm, s), lambda i: (i, 0)) for s in splits],
        out_shape=[jax.ShapeDtypeStruct((n, s), F32) for s in splits],
        compiler_params=_cparams(("parallel",)),
        name="norm_in_proj" if normalize else "shift_proj",
    )(x, g, w_bf16)


def _rmsnorm_rows_kernel(x_ref, g_ref, o_ref):
    o_ref[...] = _rms(x_ref[...], g_ref[...])


def _rmsnorm_rows(x, g):
    return pl.pallas_call(
        _rmsnorm_rows_kernel,
        out_shape=jax.ShapeDtypeStruct(x.shape, F32),
        name="shift_norm",
    )(x, g)


def _hgrn_chunk(qs, fps, ivs, lbs, sts, tri, c, sub):
    nh = len(qs)
    hs = range(nh)
    qf = [q * _sigmoid(q) for q in qs]
    f = [lb + (1.0 - lb) * _sigmoid(fp) for lb, fp in zip(lbs, fps)]
    lf = [jnp.log(z) for z in f]
    kc = [1.0 - z for z in f]
    g = [_dot_sel_l(tri, z) for z in lf]
    o_inter = [_dot_nt((qf[h] * jnp.exp(g[h])).astype(BF16), sts[h].astype(BF16)) for h in hs]
    ivb = [z.astype(BF16) for z in ivs]
    lane = lax.broadcasted_iota(I32, (sub, sub), 1)
    row = lax.broadcasted_iota(I32, (sub, sub), 0)
    outs = [[] for _ in hs]
    for i in range(c // sub):
        lo = i * sub
        blk = slice(lo, lo + sub)
        a = [jnp.zeros((sub, sub), F32) for _ in hs]
        for s in range(sub):
            for h in hs:
                gi = g[h][blk]
                e = jnp.exp(jnp.minimum(gi - gi[s:s + 1, :], 0.0))
                col = jnp.sum(qf[h][blk] * (kc[h][lo + s:lo + s + 1, :] * e), axis=-1, keepdims=True)
                a[h] = jnp.where(lane == s, col, a[h])
        a = [jnp.where(row >= lane, z, 0.0) for z in a]
        oi = [o_inter[h][blk] + _dot(a[h].astype(BF16), ivb[h][blk]) for h in hs]
        if i > 0:
            gr = [g[h][lo - 1:lo, :] for h in hs]
            qd = [(qf[h][blk] * jnp.exp(g[h][blk] - gr[h])).astype(BF16) for h in hs]
            kd = [(kc[h][:lo] * jnp.exp(gr[h] - g[h][:lo])).astype(BF16) for h in hs]
            a_off = [_dot_nt(qd[h], kd[h]).astype(BF16) for h in hs]
            oi = [oi[h] + _dot(a_off[h], ivb[h][:lo]) for h in hs]
        for h in hs:
            outs[h].append(oi[h])
    o = [z[0] if len(z) == 1 else jnp.concatenate(z, axis=0) for z in outs]
    gl = [g[h][c - 1:c, :] for h in hs]
    kd = [(kc[h] * jnp.exp(gl[h] - g[h])).astype(BF16) for h in hs]
    st_new = [sts[h] * jnp.exp(gl[h]) + _dot_tn(ivb[h], kd[h]) for h in hs]
    return o, st_new


def _hgrn_kernel(q_ref, f_ref, i_ref, g_ref, lbl_ref, gw_ref, tri_ref, s0_ref, o_ref, so_ref, st_ref,
                 *, nseq, tb, c, sub):
    j = pl.program_id(1)
    nt = pl.num_programs(1)

    @pl.when(j == 0)
    def _():
        for s in range(nseq):
            for h in range(HG_H):
                st_ref[s * HG_H + h] = s0_ref[s, h].T

    lbl = lbl_ref[...]
    ex = jnp.exp(lbl - jnp.max(lbl, axis=0, keepdims=True))
    lb = ex[0:1, :] / jnp.sum(ex, axis=0, keepdims=True)
    gw = gw_ref[...]
    tri = tri_ref[...]
    n_chunks = tb // c

    def body(it, carry):
        s = it // n_chunks
        r0 = pl.multiple_of(it * c, c)
        rows = pl.ds(r0, c)
        hsl = [slice(h * HG_DK, (h + 1) * HG_DK) for h in range(HG_H)]
        o, st_new = _hgrn_chunk([q_ref[rows, z] for z in hsl], [f_ref[rows, z] for z in hsl],
                                [i_ref[rows, z] for z in hsl], [lb[:, z] for z in hsl],
                                [st_ref[s * HG_H + h] for h in range(HG_H)], tri, c, sub)
        for h in range(HG_H):
            st_ref[s * HG_H + h] = st_new[h]
            on = o[h] * lax.rsqrt(jnp.mean(o[h] * o[h], axis=-1, keepdims=True) + NORM_EPS) * gw
            gin = g_ref[rows, hsl[h]]
            o_ref[rows, hsl[h]] = on * (gin * _sigmoid(gin))
        return carry

    lax.fori_loop(0, nseq * n_chunks, body, 0)

    @pl.when(j == nt - 1)
    def _():
        for s in range(nseq):
            for h in range(HG_H):
                so_ref[s, h] = st_ref[s * HG_H + h].T


def _hgrn(proj_hg, lb_logits, gn_w, s0, o_prev, *, row0, batch, seq, nseq, tb, c, sub):
    n = proj_hg.shape[0]
    rows = nseq * tb
    nt = seq // tb
    blk0 = row0 // rows
    hw = HG_H * HG_DK
    tri = (lax.broadcasted_iota(I32, (c, c), 0) >= lax.broadcasted_iota(I32, (c, c), 1)).astype(BF16)

    def rmap(col):
        return lambda b, j: (blk0 + b * nt + j, col)

    kern = functools.partial(_hgrn_kernel, nseq=nseq, tb=tb, c=c, sub=sub)
    in_specs = [pl.BlockSpec((rows, hw), rmap(0)), pl.BlockSpec((rows, hw), rmap(1)),
                pl.BlockSpec((rows, hw), rmap(2)), pl.BlockSpec((rows, hw), rmap(3)),
                pl.BlockSpec((lb_logits.shape[0], hw), lambda b, j: (0, 0)),
                pl.BlockSpec((1, HG_DV), lambda b, j: (0, 0)),
                pl.BlockSpec((c, c), lambda b, j: (0, 0)),
                pl.BlockSpec((nseq, HG_H, HG_DK, HG_DV), lambda b, j: (b, 0, 0, 0)),
                pl.BlockSpec(memory_space=pl.ANY)]
    args = [proj_hg, proj_hg, proj_hg, proj_hg, lb_logits, gn_w, tri, s0, o_prev]
    return pl.pallas_call(
        lambda *refs: kern(*refs[:8], *refs[9:]),
        grid=(batch // nseq, nt),
        in_specs=in_specs,
        out_specs=[pl.BlockSpec((rows, hw), lambda b, j: (blk0 + b * nt + j, 0)),
                   pl.BlockSpec((nseq, HG_H, HG_DK, HG_DV), lambda b, j: (b, 0, 0, 0))],
        out_shape=[jax.ShapeDtypeStruct((n, HG_H * HG_DV), F32),
                   jax.ShapeDtypeStruct((batch, HG_H, HG_DK, HG_DV), F32)],
        scratch_shapes=[pltpu.VMEM((nseq * HG_H, HG_DV, HG_DK), F32)],
        input_output_aliases={8: 0},
        compiler_params=_cparams(("parallel", "arbitrary")),
        name="hgrn2_mix",
    )(*args)


def _softplus(z):
    return jnp.maximum(z, 0.0) + jnp.log(1.0 + jnp.exp(-jnp.abs(z)))


def _rwkv_kernel(p_ref, prev_ref, s0_ref, mu_ref, w0_ref, a0_ref, wwa_ref, g2_ref, kkw_ref, kaw_ref, rk_ref,
                 lnw_ref, lnb_ref, g128_ref, trics_ref, esel_ref, o_ref, so_ref, st_ref, carry_ref,
                 *, c, chain, seq, rows):
    tt = RW_TILE
    ntile = rows // tt
    nc = tt // c
    j = pl.program_id(1)
    nt = pl.num_programs(1)
    lane128 = lax.broadcasted_iota(I32, (1, 128), 1)
    lo_half = lane128 < RW_N

    def block_diag(se, so):
        z = jnp.zeros((RW_N, RW_N), F32)
        return jnp.concatenate([jnp.concatenate([se, z], axis=1), jnp.concatenate([z, so], axis=1)], axis=0)

    def split2(z):
        hi = z.astype(BF16)
        return hi, (z - hi.astype(F32)).astype(BF16)

    p = p_ref[...]
    rolled = pltpu.roll(p, 1, 0)
    rowi = lax.broadcasted_iota(I32, (rows, 1), 0)
    if chain:
        @pl.when(j == 0)
        def _():
            carry_ref[...] = prev_ref[0]
            for pr in range(RW_PAIRS):
                st_ref[pr] = block_diag(s0_ref[0, 2 * pr], s0_ref[0, 2 * pr + 1])

        ps = jnp.where(rowi == 0, carry_ref[...], rolled)
        carry_ref[...] = p[rows - 1:rows, :]
    else:
        first = _dot_sel_l(esel_ref[...], prev_ref[...])
        ps = jnp.where((rowi & (seq - 1)) == 0, first, rolled)
    pm = p + (ps - p) * mu_ref[...]

    r = pm[:, 0:RW_W]
    k = pm[:, RW_W:2 * RW_W]
    v = pm[:, 2 * RW_W:3 * RW_W]
    wa = pm[:, 3 * RW_W:3 * RW_W + 128]
    g_lo = pm[:, 3 * RW_W + 128:3 * RW_W + 256]
    wa = jnp.where(lo_half, jnp.tanh(wa), wa)
    xwa = _dot(wa.astype(BF16), wwa_ref[...])
    log_w = -_softplus(-(w0_ref[...] + xwa[:, :RW_W])) - 0.5
    ld = -jnp.exp(log_w)
    a = _sigmoid(a0_ref[...] + xwa[:, RW_W:])
    gate = _dot(_sigmoid(g_lo).astype(BF16), g2_ref[...])
    g128 = g128_ref[...]

    def gsum(z):
        zs = jnp.concatenate([z[:, i * 128:(i + 1) * 128] for i in range(RW_PAIRS)], axis=0)
        hi, lo = split2(zs)
        s = _dot(hi, g128) + _dot(lo, g128)
        return jnp.concatenate([s[i * rows:(i + 1) * rows] for i in range(RW_PAIRS)], axis=1)

    kkv = k * kkw_ref[...]
    kk = kkv / jnp.maximum(jnp.sqrt(gsum(kkv * kkv)), 1e-12)
    k2 = k * (1.0 + (a - 1.0) * kaw_ref[...])
    beta = kk * a

    ld_hi, ld_lo = split2(ld)
    cs = _dot(trics_ref[...], ld_hi) + _dot(trics_ref[...], ld_lo)
    gc = cs[:rows]
    gtot = cs[rows:]
    e_g = jnp.exp(gc)
    e_ng = jnp.exp(-gc)
    e_l = jnp.exp(gtot - gc)
    ah = -kk * jnp.exp(gc - ld)
    rh = r * e_g
    bh = beta * e_ng
    kh = k2 * e_ng
    bt = beta * e_l
    kt = k2 * e_l
    e_tot = jnp.exp(gtot)

    ri = lax.broadcasted_iota(I32, (2 * tt, 2 * tt), 0)
    ci = lax.broadcasted_iota(I32, (2 * tt, 2 * tt), 1)
    same = (ri // c) == (ci // c)
    m_strict = jnp.where(same & (ci < ri), 1.0, 0.0)
    m_incl = jnp.where(same & (ci <= ri), 1.0, 0.0)
    eye2 = jnp.where(ri == ci, 1.0, 0.0)
    bd = jnp.where((ri // RW_N) == (ci // RW_N), 1.0, 0.0)

    def stack(z):
        return jnp.concatenate([jnp.where(lo_half, z, 0.0), jnp.where(lo_half, 0.0, z)], axis=0)

    def fold(z):
        return z[:tt] + z[tt:]

    rowc = lax.broadcasted_iota(I32, (tt, 1), 0) // c

    def chunk_expand(z):
        return jnp.concatenate([jnp.where(rowc == ch, z, 0.0) for ch in range(nc)], axis=1).astype(BF16)

    units = [(t, pr) for t in range(ntile) for pr in range(RW_PAIRS)]
    hw = 2 * tt

    def usl(z, u):
        return z[u[0] * tt:(u[0] + 1) * tt, u[1] * 128:(u[1] + 1) * 128]

    def bf(zs):
        return [z.astype(BF16) for z in zs]

    def rows2(a, b):
        return jnp.concatenate([a, b], axis=0)

    xr = [stack(usl(rh, u)) for u in units]
    xab, xrb, xbb, xkb, xvb = (bf([stack(usl(z, u)) for u in units]) for z in (ah, rh, bh, kh, v))
    lhs_ar = [rows2(a_, r_) for a_, r_ in zip(xab, xrb)]
    ab = [_dot_nt(l_, b_) for l_, b_ in zip(lhs_ar, xbb)]
    ak = [_dot_nt(l_, k_) for l_, k_ in zip(lhs_ar, xkb)]
    a_ab = [z[:hw] * m_strict for z in ab]
    a_rb = bf([z[hw:] * m_incl for z in ab])
    a_ak = bf([z[:hw] * m_strict for z in ak])
    a_rk = bf([z[hw:] * m_incl for z in ak])
    tm = [eye2 + z for z in a_ab]
    if c > 2:
        pw = [_dot(z, z) for z in bf(a_ab)]
    n = 2
    while n < c:
        pwb = bf(pw)
        if 2 * n < c:
            both = [_dot(rows2(t_.astype(BF16), p_), p_) for t_, p_ in zip(tm, pwb)]
            tm = [t_ + z[:hw] for t_, z in zip(tm, both)]
            pw = [z[hw:] for z in both]
        else:
            tm = [t_ + _dot(t_.astype(BF16), p_) for t_, p_ in zip(tm, pwb)]
        n *= 2
    tmb = bf(tm)
    akv = [_dot(rows2(p_, q_), x_) for p_, q_, x_ in zip(a_ak, a_rk, xvb)]
    tav = [_dot(t_, jnp.concatenate([x_, z[:hw].astype(BF16)], axis=1))
           for t_, x_, z in zip(tmb, xab, akv)]
    rbt = [_dot(r_, z) for r_, z in zip(a_rb, bf(tav))]
    rpb = bf([fold(x_ + z[:, :128]) for x_, z in zip(xr, rbt)])
    y0 = [fold(z[:, 128:] + w[hw:]) for z, w in zip(rbt, akv)]
    ap = bf([fold(z[:, :128]) for z in tav])
    vp = bf([fold(z[:, 128:]) for z in tav])
    bexp = [chunk_expand(usl(bt, u)) for u in units]
    kexp = [chunk_expand(usl(kt, u)) for u in units]
    vpl = bf([usl(v, u) for u in units])
    mt_all = [_dot_tn(a_, b_) for a_, b_ in zip(ap, bexp)]
    ht_all = [_dot_tn(rows2(p_, q_), rows2(b_, k_)) for p_, q_, b_, k_ in zip(vp, vpl, bexp, kexp)]

    sts = [st_ref[pr] for pr in range(RW_PAIRS)] if chain else None
    y_parts = [[] for _ in units]
    for t in range(ntile):
        for ch in range(nc):
            rs = slice(ch * c, (ch + 1) * c)
            cl = slice(ch * 128, (ch + 1) * 128)
            r0 = t * tt + ch * c
            for pr in range(RW_PAIRS):
                u = t * RW_PAIRS + pr
                mt = mt_all[u][:, cl] * bd + eye2 * e_tot[r0:r0 + 1, pr * 128:(pr + 1) * 128]
                ht = ht_all[u][:, cl] * bd
                if chain:
                    st = sts[pr]
                else:
                    sq = t * nc + ch
                    st = block_diag(s0_ref[sq, 2 * pr], s0_ref[sq, 2 * pr + 1])
                stb = st.astype(BF16)
                y_parts[u].append(_dot_nt(rpb[u][rs], stb) + y0[u][rs])
                st = _dot(stb, mt.astype(BF16)) + ht
                if chain:
                    sts[pr] = st
                else:
                    so_ref[sq, 2 * pr] = st[:RW_N, :RW_N]
                    so_ref[sq, 2 * pr + 1] = st[RW_N:, RW_N:]
    y_tiles = [jnp.concatenate([jnp.concatenate(y_parts[t * RW_PAIRS + pr], axis=0) for pr in range(RW_PAIRS)],
                               axis=1) for t in range(ntile)]
    y = y_tiles[0] if ntile == 1 else jnp.concatenate(y_tiles, axis=0)

    if chain:
        for pr in range(RW_PAIRS):
            st_ref[pr] = sts[pr]

        @pl.when(j == nt - 1)
        def _():
            for pr in range(RW_PAIRS):
                st = st_ref[pr]
                so_ref[0, 2 * pr] = st[:RW_N, :RW_N]
                so_ref[0, 2 * pr + 1] = st[RW_N:, RW_N:]

    inv_n = 1.0 / RW_N
    mean = gsum(y) * inv_n
    yc = y - mean
    var = gsum(yc * yc) * inv_n
    yn = yc * lax.rsqrt(var + RW_GN_EPS) * lnw_ref[...] + lnb_ref[...]
    bonus = gsum(r * k2 * rk_ref[...]) * v
    o_ref[...] = (yn + bonus) * gate


def _rwkv(proj_rw, prev, s0, wts, o_prev, *, row0, batch, seq, chain):
    n = proj_rw.shape[0]
    if chain:
        rows = _pick_tile(seq, RW_ROWS_CHAIN)
        nseq, c, nt = 1, RW_CHUNK, seq // rows
        prev_spec = pl.BlockSpec((1, 1, RW_COLS), lambda b, j: (b, 0, 0))
    else:
        assert seq & (seq - 1) == 0 and RW_TILE % seq == 0
        rows = _pick_tile(batch * seq, RW_ROWS_SHORT)
        nseq, c, nt = rows // seq, seq, 1
        prev_spec = pl.BlockSpec((nseq, RW_COLS), lambda b, j: (b, 0))
    blk0 = row0 // rows
    ri = lax.broadcasted_iota(I32, (rows, rows), 0)
    ci = lax.broadcasted_iota(I32, (rows, rows), 1)
    same = (ri // c) == (ci // c)
    trics = jnp.concatenate([same & (ci <= ri), same], axis=0).astype(BF16)
    gi = lax.broadcasted_iota(I32, (128, 128), 0) // RW_N
    gj = lax.broadcasted_iota(I32, (128, 128), 1) // RW_N
    g128 = (gi == gj).astype(BF16)
    esel = (lax.broadcasted_iota(I32, (rows, nseq), 0) // seq
            == lax.broadcasted_iota(I32, (rows, nseq), 1)).astype(BF16)
    (mu, w0, a0, wwa, g2, kkw, kaw, rk, lnw, lnb) = wts

    def full(arr):
        return pl.BlockSpec(arr.shape, lambda b, j: (0,) * arr.ndim)

    consts = [mu, w0, a0, wwa, g2, kkw, kaw, rk, lnw, lnb, g128, trics, esel]
    in_specs = [pl.BlockSpec((rows, RW_COLS), lambda b, j: (blk0 + b * nt + j, 0)),
                prev_spec,
                pl.BlockSpec((nseq, RW_H, RW_N, RW_N), lambda b, j: (b, 0, 0, 0))]
    in_specs += [full(t) for t in consts] + [pl.BlockSpec(memory_space=pl.ANY)]
    args = [proj_rw, prev, s0] + consts + [o_prev]
    kern = functools.partial(_rwkv_kernel, c=c, chain=chain, seq=seq, rows=rows)
    n_in = len(args) - 1
    return pl.pallas_call(
        lambda *refs: kern(*refs[:n_in], *refs[n_in + 1:]),
        grid=(batch // nseq, nt),
        in_specs=in_specs,
        out_specs=[pl.BlockSpec((rows, RW_W), lambda b, j: (blk0 + b * nt + j, 0)),
                   pl.BlockSpec((nseq, RW_H, RW_N, RW_N), lambda b, j: (b, 0, 0, 0))],
        out_shape=[jax.ShapeDtypeStruct((n, RW_W), F32),
                   jax.ShapeDtypeStruct((batch, RW_H, RW_N, RW_N), F32)],
        scratch_shapes=[pltpu.VMEM((RW_PAIRS, 128, 128), F32), pltpu.VMEM((1, RW_COLS), F32)],
        input_output_aliases={n_in: 0},
        compiler_params=_cparams(("parallel", "arbitrary")),
        name="rwkv7_mix",
    )(*args)


def _outproj_router_kernel(ohg_ref, orw_ref, x_ref, wo_ref, gf_ref, wr_ref, br_ref, tri_ref,
                           x1_ref, hf_ref, idx_ref, gate_ref, pos_ref, cnt_ref, carry_ref):
    i = pl.program_id(0)

    @pl.when(i == 0)
    def _():
        carry_ref[...] = jnp.zeros_like(carry_ref)

    half = HG_H * HG_DV
    mixed = _dot(ohg_ref[...].astype(BF16), wo_ref[:half, :]) + _dot(orw_ref[...].astype(BF16), wo_ref[half:, :])
    x1 = x_ref[...] + mixed
    x1_ref[...] = x1
    hf = _rms(x1, gf_ref[...])
    hf_ref[...] = hf

    h1, h2, _ = _split3(hf)
    w1, w2, _ = _split3(wr_ref[...])
    logits = _dot(h1, w1) + _dot(h1, w2) + _dot(h2, w1) + br_ref[...]

    tm = logits.shape[0]
    lane = lax.broadcasted_iota(I32, (tm, N_EXPERTS), 1).astype(F32)
    lane4 = lax.broadcasted_iota(I32, (tm, TOP_K), 1)
    vals = logits
    maskf = jnp.zeros((tm, N_EXPERTS), F32)
    sels, tops = [], []
    idx_out = jnp.zeros((tm, TOP_K), F32)
    for kq in range(TOP_K):
        m = jnp.max(vals, axis=-1, keepdims=True)
        idx = jnp.min(jnp.where(vals == m, lane, float(N_EXPERTS)), axis=-1, keepdims=True)
        sel = lane == idx
        sels.append(sel)
        tops.append(m)
        idx_out = jnp.where(lane4 == kq, idx, idx_out)
        vals = jnp.where(sel, -jnp.inf, vals)
        maskf = maskf + jnp.where(sel, 1.0, 0.0)
    es = [jnp.exp(t - tops[0]) for t in tops]
    denom = es[0] + es[1] + es[2] + es[3]
    gate_out = jnp.zeros((tm, TOP_K), F32)
    for kq in range(TOP_K):
        gate_out = jnp.where(lane4 == kq, es[kq] / denom, gate_out)

    pos = _dot(tri_ref[...], maskf.astype(BF16)) + carry_ref[...]
    pos_out = jnp.zeros((tm, TOP_K), F32)
    for kq in range(TOP_K):
        pk = jnp.sum(jnp.where(sels[kq], pos, 0.0), axis=-1, keepdims=True)
        pos_out = jnp.where(lane4 == kq, pk, pos_out)
    carry_ref[...] = carry_ref[...] + jnp.sum(maskf, axis=0, keepdims=True)
    idx_ref[...] = idx_out.astype(I32)
    gate_ref[...] = gate_out
    pos_ref[...] = pos_out.astype(I32)
    cnt_ref[...] = carry_ref[...].astype(I32)


def _outproj_router(o_hg, o_rw, x, wo_bf16, gf, w_router, b_router, tm):
    n = x.shape[0]
    tri = (lax.broadcasted_iota(I32, (tm, tm), 0) > lax.broadcasted_iota(I32, (tm, tm), 1)).astype(BF16)
    half = HG_H * HG_DV
    row = lambda i: (i, 0)
    fix = lambda i: (0, 0)
    return pl.pallas_call(
        _outproj_router_kernel,
        grid=(n // tm,),
        in_specs=[pl.BlockSpec((tm, half), row), pl.BlockSpec((tm, RW_W), row), pl.BlockSpec((tm, D_MODEL), row),
                  pl.BlockSpec((half + RW_W, D_MODEL), fix), pl.BlockSpec((1, D_MODEL), fix),
                  pl.BlockSpec((D_MODEL, N_EXPERTS), fix), pl.BlockSpec((1, N_EXPERTS), fix),
                  pl.BlockSpec((tm, tm), fix)],
        out_specs=[pl.BlockSpec((tm, D_MODEL), row), pl.BlockSpec((tm, D_MODEL), row),
                   pl.BlockSpec((tm, TOP_K), row), pl.BlockSpec((tm, TOP_K), row), pl.BlockSpec((tm, TOP_K), row),
                   pl.BlockSpec((1, N_EXPERTS), fix)],
        out_shape=[jax.ShapeDtypeStruct((n, D_MODEL), F32), jax.ShapeDtypeStruct((n, D_MODEL), F32),
                   jax.ShapeDtypeStruct((n, TOP_K), I32), jax.ShapeDtypeStruct((n, TOP_K), F32),
                   jax.ShapeDtypeStruct((n, TOP_K), I32), jax.ShapeDtypeStruct((1, N_EXPERTS), I32)],
        scratch_shapes=[pltpu.VMEM((1, N_EXPERTS), F32)],
        compiler_params=_cparams(("arbitrary",)),
        name="out_proj_router",
    )(o_hg, o_rw, x, wo_bf16, gf, w_router, b_router, tri)


def _plan_kernel(cnt_ref, idx_ref, pos_ref, dest_ref, bexp_ref, nblk_ref, *, n_blocks):
    cnt = cnt_ref[...].astype(F32)
    padded = jnp.floor((cnt + (MOE_BLOCK - 1)) * (1.0 / MOE_BLOCK)) * MOE_BLOCK
    ei = lax.broadcasted_iota(I32, (N_EXPERTS, N_EXPERTS), 0)
    ej = lax.broadcasted_iota(I32, (N_EXPERTS, N_EXPERTS), 1)
    upper = (ei < ej).astype(BF16)
    upper_incl = (ei <= ej).astype(BF16)
    start = _dot_sel_r(padded, upper)
    ends = _dot_sel_r(padded, upper_incl)
    idx = idx_ref[...]
    dest = pos_ref[...]
    for e in range(N_EXPERTS):
        dest = dest + jnp.where(idx == e, start[:, e:e + 1].astype(I32), 0)
    dest_ref[...] = dest
    bstart = (lax.broadcasted_iota(I32, (n_blocks, 1), 0) * MOE_BLOCK).astype(F32)
    be = jnp.sum(jnp.where(ends <= bstart, 1.0, 0.0), axis=-1, keepdims=True)
    bexp_ref[...] = jnp.minimum(be, N_EXPERTS - 1.0).astype(I32)
    nblk_ref[...] = (ends[:, N_EXPERTS - 1:] * (1.0 / MOE_BLOCK)).astype(I32)


def _plan(cnt, idx, pos, n_blocks, tm):
    n = idx.shape[0]
    row = lambda i: (i, 0)
    fix = lambda i: (0, 0)
    return pl.pallas_call(
        functools.partial(_plan_kernel, n_blocks=n_blocks),
        grid=(n // tm,),
        in_specs=[pl.BlockSpec((1, N_EXPERTS), fix), pl.BlockSpec((tm, TOP_K), row), pl.BlockSpec((tm, TOP_K), row)],
        out_specs=[pl.BlockSpec((tm, TOP_K), row), pl.BlockSpec((n_blocks, 1), fix), pl.BlockSpec((1, 1), fix)],
        out_shape=[jax.ShapeDtypeStruct((n, TOP_K), I32), jax.ShapeDtypeStruct((n_blocks, 1), I32),
                   jax.ShapeDtypeStruct((1, 1), I32)],
        compiler_params=_cparams(("arbitrary",)),
        name="moe_plan",
    )(cnt, idx, pos)


def _scatter_kernel(dest_ref, hf_ref, xs_in_ref, xs_ref, sem, *, ts):
    del xs_in_ref

    def issue(t, carry):
        for kq in range(TOP_K):
            d = dest_ref[t * TOP_K + kq]
            pltpu.make_async_copy(hf_ref.at[pl.ds(t, 1)], xs_ref.at[pl.ds(d, 1)], sem).start()
        return carry

    lax.fori_loop(0, ts, issue, 0)

    def drain(t, carry):
        for kq in range(TOP_K):
            pltpu.make_async_copy(hf_ref.at[pl.ds(0, 1)], xs_ref.at[pl.ds(0, 1)], sem).wait()
        return carry

    lax.fori_loop(0, ts, drain, 0)


def _scatter(dest_flat, hf, xs_init, ts):
    n = hf.shape[0]
    return pl.pallas_call(
        functools.partial(_scatter_kernel, ts=ts),
        grid=(n // ts,),
        in_specs=[pl.BlockSpec((ts * TOP_K,), lambda i: (i,), memory_space=pltpu.SMEM),
                  pl.BlockSpec((ts, D_MODEL), lambda i: (i, 0)), pl.BlockSpec(memory_space=pl.ANY)],
        out_specs=pl.BlockSpec(memory_space=pl.ANY),
        out_shape=jax.ShapeDtypeStruct(xs_init.shape, F32),
        scratch_shapes=[pltpu.SemaphoreType.DMA(())],
        input_output_aliases={2: 0},
        compiler_params=_cparams(("arbitrary",)),
        name="moe_scatter",
    )(dest_flat, hf, xs_init)


def _ffn_kernel(bexp_ref, nblk_ref, xs_ref, wgu_ref, bgu_ref, wd_ref, bd_ref, yb_ref, wgu_bf, wd_bf):
    b = pl.program_id(0)
    valid = b < nblk_ref[0]
    prev_e = bexp_ref[jnp.maximum(b, 1) - 1]
    fresh = (b == 0) | (bexp_ref[b] != prev_e)

    @pl.when(valid & fresh)
    def _():
        wgu_bf[...] = wgu_ref[0].astype(BF16)
        wd_bf[...] = wd_ref[0].astype(BF16)

    @pl.when(valid)
    def _():
        x = xs_ref[...].astype(BF16)
        gu = _dot(x, wgu_bf[...]) + bgu_ref[0]
        gate = jnp.minimum(gu[:, :D_FF], SWIGLU_LIMIT)
        up = jnp.clip(gu[:, D_FF:], -SWIGLU_LIMIT, SWIGLU_LIMIT)
        glu = gate * _sigmoid(SWIGLU_ALPHA * gate)
        act = ((up + 1.0) * glu).astype(BF16)
        yb_ref[...] = _dot(act, wd_bf[...]) + bd_ref[0]

    @pl.when(jnp.logical_not(valid))
    def _():
        yb_ref[...] = jnp.zeros_like(yb_ref)


def _ffn(bexp, nblk, xs, w_gate_up, b_gate_up, w_down, b_down):
    n_rows = xs.shape[0]
    n_blocks = n_rows // MOE_BLOCK

    def blk(b, be, nb):
        return (jnp.minimum(b, jnp.maximum(nb[0], 1) - 1), 0)

    def wmap(b, be, nb):
        return (be[jnp.minimum(b, jnp.maximum(nb[0], 1) - 1)], 0, 0)

    grid_spec = pltpu.PrefetchScalarGridSpec(
        num_scalar_prefetch=2,
        grid=(n_blocks,),
        in_specs=[pl.BlockSpec((MOE_BLOCK, D_MODEL), blk),
                  pl.BlockSpec((1, D_MODEL, 2 * D_FF), wmap),
                  pl.BlockSpec((1, 1, 2 * D_FF), wmap),
                  pl.BlockSpec((1, D_FF, D_MODEL), wmap),
                  pl.BlockSpec((1, 1, D_MODEL), wmap)],
        out_specs=pl.BlockSpec((MOE_BLOCK, D_MODEL), lambda b, be, nb: (b, 0)),
        scratch_shapes=[pltpu.VMEM((D_MODEL, 2 * D_FF), BF16), pltpu.VMEM((D_FF, D_MODEL), BF16)],
    )
    return pl.pallas_call(
        _ffn_kernel,
        grid_spec=grid_spec,
        out_shape=jax.ShapeDtypeStruct((n_rows, D_MODEL), F32),
        compiler_params=_cparams(("arbitrary",)),
        name="moe_ffn",
    )(bexp, nblk, xs, w_gate_up, b_gate_up, w_down, b_down)


def _combine_kernel(dest_ref, gate_ref, x1_ref, gn_ref, yb_ref, y_ref, buf, sem, *, tc):
    def issue(t, carry):
        for kq in range(TOP_K):
            d = dest_ref[t * TOP_K + kq]
            pltpu.make_async_copy(yb_ref.at[pl.ds(d, 1)], buf.at[kq, pl.ds(t, 1)], sem).start()
        return carry

    lax.fori_loop(0, tc, issue, 0)

    def drain(t, carry):
        for kq in range(TOP_K):
            pltpu.make_async_copy(yb_ref.at[pl.ds(0, 1)], buf.at[0, pl.ds(0, 1)], sem).wait()
        return carry

    lax.fori_loop(0, tc, drain, 0)
    gate = gate_ref[...]
    acc = x1_ref[...]
    for kq in range(TOP_K):
        acc = acc + gate[:, kq:kq + 1] * buf[kq]
    y_ref[...] = _rms(acc, gn_ref[...])


def _combine(dest_flat, gate, x1, gn, yb, tc):
    n = x1.shape[0]
    return pl.pallas_call(
        functools.partial(_combine_kernel, tc=tc),
        grid=(n // tc,),
        in_specs=[pl.BlockSpec((tc * TOP_K,), lambda i: (i,), memory_space=pltpu.SMEM),
                  pl.BlockSpec((tc, TOP_K), lambda i: (i, 0)),
                  pl.BlockSpec((tc, D_MODEL), lambda i: (i, 0)),
                  pl.BlockSpec((1, D_MODEL), lambda i: (0, 0)),
                  pl.BlockSpec(memory_space=pl.ANY)],
        out_specs=pl.BlockSpec((tc, D_MODEL), lambda i: (i, 0)),
        out_shape=jax.ShapeDtypeStruct((n, D_MODEL), F32),
        scratch_shapes=[pltpu.VMEM((TOP_K, tc, D_MODEL), F32), pltpu.SemaphoreType.DMA(())],
        compiler_params=_cparams(("arbitrary",)),
        name="moe_combine",
    )(dest_flat, gate, x1, gn, yb)


def _pick_tile(n, pref):
    t = pref
    while n % t:
        t //= 2
    return t


def kernel(x_prompt, x_sample, state_hgrn, state_rwkv, state_shift, norm_mix, w_in, lb_logits, hg_norm_w, rw_mu,
           rw_w0, rw_w2, rw_a0, rw_a2, rw_g2, rw_k_k, rw_k_a, rw_r_k, rw_lnx_w, rw_lnx_b, w_out, norm_ffn,
           w_router, b_router, w_gate_up, b_gate_up, w_down, b_down, norm_final):
    bp, tp, d = x_prompt.shape
    bs, tsq, _ = x_sample.shape
    assert norm_mix.shape[0] == 1 and d == D_MODEL
    n_p, n_s = bp * tp, bs * tsq
    n = n_p + n_s
    x = jnp.concatenate([x_prompt.reshape(n_p, d), x_sample.reshape(n_s, d)], axis=0)

    w_in_b = w_in[0].astype(BF16)
    proj_hg, proj_rw = _proj(x, norm_mix, w_in_b, (HG_COLS, RW_COLS), True, _pick_tile(n, 256))
    ones_d = jnp.ones((1, d), F32)
    (prev_s,) = _proj(state_shift[0], ones_d, w_in_b[:, HG_COLS:], (RW_COLS,), False, _pick_tile(bs, 128))
    prev_p = jnp.zeros((bp, 1, RW_COLS), F32)
    x_last = jnp.concatenate([x_prompt[:, -1, :], x_sample[:, -1, :]], axis=0)
    shift = _rmsnorm_rows(x_last, norm_mix)

    zero_hg = jnp.zeros((bp, HG_H, HG_DK, HG_DV), F32)
    o_hg, hg_p = _hgrn(proj_hg, lb_logits, hg_norm_w, zero_hg, jnp.zeros((n, HG_H * HG_DV), F32),
                       row0=0, batch=bp, seq=tp,
                       nseq=1, tb=_pick_tile(tp, 256), c=HG_CHUNK, sub=HG_SUB)
    ns_h = _pick_tile(bs, 16)
    o_hg, hg_s = _hgrn(proj_hg, lb_logits, hg_norm_w, state_hgrn[0], o_hg, row0=n_p, batch=bs, seq=tsq,
                       nseq=ns_h, tb=tsq, c=tsq, sub=tsq)

    zpad = jnp.zeros((64, RW_W), F32)
    wwa = jnp.concatenate([jnp.concatenate([rw_w2[0], zpad], axis=1),
                           jnp.concatenate([zpad, rw_a2[0]], axis=1)], axis=0).astype(BF16)
    wts = (rw_mu, rw_w0, rw_a0, wwa, rw_g2[0].astype(BF16), rw_k_k, rw_k_a, rw_r_k.reshape(1, RW_W),
           rw_lnx_w, rw_lnx_b)
    zero_rw = jnp.zeros((bp, RW_H, RW_N, RW_N), F32)
    o_rw, rw_p = _rwkv(proj_rw, prev_p, zero_rw, wts, jnp.zeros((n, RW_W), F32),
                       row0=0, batch=bp, seq=tp, chain=True)
    o_rw, rw_s = _rwkv(proj_rw, prev_s, state_rwkv[0], wts, o_rw,
                       row0=n_p, batch=bs, seq=tsq, chain=False)

    tm = _pick_tile(n, 512)
    x1, hf, idx, gate, pos, cnt = _outproj_router(o_hg, o_rw, x, w_out[0].astype(BF16), norm_ffn,
                                                  w_router[0], b_router, tm)

    n_blocks = -(-(n * TOP_K) // MOE_BLOCK) + N_EXPERTS
    dest, bexp, nblk = _plan(cnt, idx, pos, n_blocks, tm)
    dest_flat = dest.reshape(n * TOP_K)
    xs = _scatter(dest_flat, hf, jnp.zeros((n_blocks * MOE_BLOCK, d), F32), _pick_tile(n, 512))
    yb = _ffn(bexp.reshape(n_blocks), nblk.reshape(1), xs, w_gate_up[0], b_gate_up[0].reshape(N_EXPERTS, 1, 2 * D_FF),
              w_down[0], b_down[0].reshape(N_EXPERTS, 1, d))
    y = _combine(dest_flat, gate, x1, norm_final.reshape(1, d), yb, _pick_tile(n, 256))

    y_p = y[:n_p].reshape(bp, tp, d)
    y_s = y[n_p:].reshape(bs, tsq, d)
    return (y_p, y_s, hg_p[None], rw_p[None], shift[:bp][None], hg_s[None], rw_s[None], shift[bp:][None])
```

```python
import functools
import math

import jax
import jax.numpy as jnp
from jax import lax
from jax.experimental import pallas as pl
from jax.experimental.pallas import tpu as pltpu

F32 = jnp.float32
BF16 = jnp.bfloat16
I32 = jnp.int32

D_MODEL = 1024
HG_H, HG_DK, HG_DV = 4, 128, 128
HG_CHUNK = 64
HG_SUB = 16
RW_H, RW_N = 8, 64
RW_W = RW_H * RW_N
RW_PAIRS = RW_H // 2
RW_TILE = 64
RW_CHUNK = 16
RW_ROWS_CHAIN = 256
RW_ROWS_SHORT = 128
HG_COLS = 2 * HG_H * HG_DK + 2 * HG_H * HG_DV
RW_COLS = 3 * RW_W + 64 + 64 + 128
N_EXPERTS = 32
TOP_K = 4
D_FF = D_MODEL
SWIGLU_LIMIT = 7.0
SWIGLU_ALPHA = 1.702
NORM_EPS = 1e-6
RW_GN_EPS = 64e-5
MOE_BLOCK = 256
VMEM_LIMIT = 56 * 1024 * 1024


def _dot(a, b):
    return jnp.dot(a, b, preferred_element_type=F32)


def _dot_nt(a, b):
    return lax.dot_general(a, b, (((1,), (1,)), ((), ())), preferred_element_type=F32)


def _dot_tn(a, b):
    return lax.dot_general(a, b, (((0,), (0,)), ((), ())), preferred_element_type=F32)


def _split3(x):
    x1 = x.astype(BF16)
    r1 = x - x1.astype(F32)
    x2 = r1.astype(BF16)
    x3 = (r1 - x2.astype(F32)).astype(BF16)
    return x1, x2, x3


def _dot_sel_l(m_bf16, x):
    x1, x2, x3 = _split3(x)
    return _dot(m_bf16, x1) + _dot(m_bf16, x2) + _dot(m_bf16, x3)


def _dot_sel_r(x, m_bf16):
    x1, x2, x3 = _split3(x)
    return _dot(x1, m_bf16) + _dot(x2, m_bf16) + _dot(x3, m_bf16)


def _sigmoid(x):
    return 1.0 / (1.0 + jnp.exp(-x))


def _rms(x, g):
    return x * lax.rsqrt(jnp.mean(x * x, axis=-1, keepdims=True) + NORM_EPS) * g


def _cparams(sem):
    return pltpu.CompilerParams(dimension_semantics=sem, vmem_limit_bytes=VMEM_LIMIT)


def _proj_kernel(x_ref, g_ref, w_ref, *o_refs, normalize, splits):
    x = x_ref[...]
    h = _rms(x, g_ref[...]) if normalize else x
    hb = h.astype(BF16)
    c0 = 0
    for o_ref, width in zip(o_refs, splits):
        o_ref[...] = _dot(hb, w_ref[:, c0:c0 + width])
        c0 += width


def _proj(x, g, w_bf16, splits, normalize, tm):
    n = x.shape[0]
    kern = functools.partial(_proj_kernel, normalize=normalize, splits=splits)
    return pl.pallas_call(
        kern,
        grid=(n // tm,),
        in_specs=[pl.BlockSpec((tm, D_MODEL), lambda i: (i, 0)),
                  pl.BlockSpec((1, D_MODEL), lambda i: (0, 0)),
                  pl.BlockSpec((D_MODEL, sum(splits)), lambda i: (0, 0))],
        out_specs=[pl.BlockSpec((tm, s), lambda i: (i, 0)) for s in splits],
        out_shape=[jax.ShapeDtypeStruct((n, s), F32) for s in splits],
        compiler_params=_cparams(("parallel",)),
        name="norm_in_proj" if normalize else "shift_proj",
    )(x, g, w_bf16)


def _rmsnorm_rows_kernel(x_ref, g_ref, o_ref):
    o_ref[...] = _rms(x_ref[...], g_ref[...])


def _rmsnorm_rows(x, g):
    return pl.pallas_call(
        _rmsnorm_rows_kernel,
        out_shape=jax.ShapeDtypeStruct(x.shape, F32),
        name="shift_norm",
    )(x, g)


def _hgrn_chunk(qs, fps, ivs, lbs, sts, tri, c, sub):
    nh = len(qs)
    hs = range(nh)
    qf = [q * _sigmoid(q) for q in qs]
    f = [lb + (1.0 - lb) * _sigmoid(fp) for lb, fp in zip(lbs, fps)]
    lf = [jnp.log(z) for z in f]
    kc = [1.0 - z for z in f]
    g = [_dot_sel_l(tri, z) for z in lf]
    o_inter = [_dot_nt((qf[h] * jnp.exp(g[h])).astype(BF16), sts[h].astype(BF16)) for h in hs]
    ivb = [z.astype(BF16) for z in ivs]
    lane = lax.broadcasted_iota(I32, (sub, sub), 1)
    row = lax.broadcasted_iota(I32, (sub, sub), 0)
    outs = [[] for _ in hs]
    for i in range(c // sub):
        lo = i * sub
        blk = slice(lo, lo + sub)
        a = [jnp.zeros((sub, sub), F32) for _ in hs]
        for s in range(sub):
            for h in hs:
                gi = g[h][blk]
                e = jnp.exp(jnp.minimum(gi - gi[s:s + 1, :], 0.0))
                col = jnp.sum(qf[h][blk] * (kc[h][lo + s:lo + s + 1, :] * e), axis=-1, keepdims=True)
                a[h] = jnp.where(lane == s, col, a[h])
        a = [jnp.where(row >= lane, z, 0.0) for z in a]
        oi = [o_inter[h][blk] + _dot(a[h].astype(BF16), ivb[h][blk]) for h in hs]
        if i > 0:
            gr = [g[h][lo - 1:lo, :] for h in hs]
            qd = [(qf[h][blk] * jnp.exp(g[h][blk] - gr[h])).astype(BF16) for h in hs]
            kd = [(kc[h][:lo] * jnp.exp(gr[h] - g[h][:lo])).astype(BF16) for h in hs]
            a_off = [_dot_nt(qd[h], kd[h]).astype(BF16) for h in hs]
            oi = [oi[h] + _dot(a_off[h], ivb[h][:lo]) for h in hs]
        for h in hs:
            outs[h].append(oi[h])
    o = [z[0] if len(z) == 1 else jnp.concatenate(z, axis=0) for z in outs]
    gl = [g[h][c - 1:c, :] for h in hs]
    kd = [(kc[h] * jnp.exp(gl[h] - g[h])).astype(BF16) for h in hs]
    st_new = [sts[h] * jnp.exp(gl[h]) + _dot_tn(ivb[h], kd[h]) for h in hs]
    return o, st_new


def _hgrn_kernel(q_ref, f_ref, i_ref, g_ref, lbl_ref, gw_ref, tri_ref, s0_ref, o_ref, so_ref, st_ref,
                 *, nseq, tb, c, sub):
    j = pl.program_id(1)
    nt = pl.num_programs(1)

    @pl.when(j == 0)
    def _():
        for s in range(nseq):
            for h in range(HG_H):
                st_ref[s * HG_H + h] = s0_ref[s, h].T

    lbl = lbl_ref[...]
    ex = jnp.exp(lbl - jnp.max(lbl, axis=0, keepdims=True))
    lb = ex[0:1, :] / jnp.sum(ex, axis=0, keepdims=True)
    gw = gw_ref[...]
    tri = tri_ref[...]
    n_chunks = tb // c

    def body(it, carry):
        s = it // n_chunks
        r0 = pl.multiple_of(it * c, c)
        rows = pl.ds(r0, c)
        hsl = [slice(h * HG_DK, (h + 1) * HG_DK) for h in range(HG_H)]
        o, st_new = _hgrn_chunk([q_ref[rows, z] for z in hsl], [f_ref[rows, z] for z in hsl],
                                [i_ref[rows, z] for z in hsl], [lb[:, z] for z in hsl],
                                [st_ref[s * HG_H + h] for h in range(HG_H)], tri, c, sub)
        for h in range(HG_H):
            st_ref[s * HG_H + h] = st_new[h]
            on = o[h] * lax.rsqrt(jnp.mean(o[h] * o[h], axis=-1, keepdims=True) + NORM_EPS) * gw
            gin = g_ref[rows, hsl[h]]
            o_ref[rows, hsl[h]] = on * (gin * _sigmoid(gin))
        return carry

    lax.fori_loop(0, nseq * n_chunks, body, 0)

    @pl.when(j == nt - 1)
    def _():
        for s in range(nseq):
            for h in range(HG_H):
                so_ref[s, h] = st_ref[s * HG_H + h].T


def _hgrn(proj_hg, lb_logits, gn_w, s0, *, batch, seq, nseq, tb, c, sub):
    n = proj_hg.shape[0]
    rows = nseq * tb
    nt = seq // tb
    hw = HG_H * HG_DK
    tri = (lax.broadcasted_iota(I32, (c, c), 0) >= lax.broadcasted_iota(I32, (c, c), 1)).astype(BF16)

    def rmap(col):
        return lambda b, j: (b * nt + j, col)

    kern = functools.partial(_hgrn_kernel, nseq=nseq, tb=tb, c=c, sub=sub)
    in_specs = [pl.BlockSpec((rows, hw), rmap(0)), pl.BlockSpec((rows, hw), rmap(1)),
                pl.BlockSpec((rows, hw), rmap(2)), pl.BlockSpec((rows, hw), rmap(3)),
                pl.BlockSpec((lb_logits.shape[0], hw), lambda b, j: (0, 0)),
                pl.BlockSpec((1, HG_DV), lambda b, j: (0, 0)),
                pl.BlockSpec((c, c), lambda b, j: (0, 0)),
                pl.BlockSpec((nseq, HG_H, HG_DK, HG_DV), lambda b, j: (b, 0, 0, 0))]
    args = [proj_hg, proj_hg, proj_hg, proj_hg, lb_logits, gn_w, tri, s0]
    return pl.pallas_call(
        kern,
        grid=(batch // nseq, nt),
        in_specs=in_specs,
        out_specs=[pl.BlockSpec((rows, hw), lambda b, j: (b * nt + j, 0)),
                   pl.BlockSpec((nseq, HG_H, HG_DK, HG_DV), lambda b, j: (b, 0, 0, 0))],
        out_shape=[jax.ShapeDtypeStruct((n, HG_H * HG_DV), F32),
                   jax.ShapeDtypeStruct((batch, HG_H, HG_DK, HG_DV), F32)],
        scratch_shapes=[pltpu.VMEM((nseq * HG_H, HG_DV, HG_DK), F32)],
        compiler_params=_cparams(("parallel", "arbitrary")),
        name="hgrn2_mix",
    )(*args)


def _softplus(z):
    return jnp.maximum(z, 0.0) + jnp.log(1.0 + jnp.exp(-jnp.abs(z)))


def _rwkv_kernel(p_ref, prev_ref, s0_ref, mu_ref, w0_ref, a0_ref, wwa_ref, g2_ref, kkw_ref, kaw_ref, rk_ref,
                 lnw_ref, lnb_ref, g128_ref, trics_ref, esel_ref, o_ref, so_ref, st_ref, carry_ref,
                 *, c, chain, seq, rows):
    tt = RW_TILE
    ntile = rows // tt
    nc = tt // c
    j = pl.program_id(1)
    nt = pl.num_programs(1)
    lane128 = lax.broadcasted_iota(I32, (1, 128), 1)
    lo_half = lane128 < RW_N

    def block_diag(se, so):
        z = jnp.zeros((RW_N, RW_N), F32)
        return jnp.concatenate([jnp.concatenate([se, z], axis=1), jnp.concatenate([z, so], axis=1)], axis=0)

    def split2(z):
        hi = z.astype(BF16)
        return hi, (z - hi.astype(F32)).astype(BF16)

    p = p_ref[...]
    rolled = pltpu.roll(p, 1, 0)
    rowi = lax.broadcasted_iota(I32, (rows, 1), 0)
    if chain:
        @pl.when(j == 0)
        def _():
            carry_ref[...] = prev_ref[0]
            for pr in range(RW_PAIRS):
                st_ref[pr] = block_diag(s0_ref[0, 2 * pr], s0_ref[0, 2 * pr + 1])

        ps = jnp.where(rowi == 0, carry_ref[...], rolled)
        carry_ref[...] = p[rows - 1:rows, :]
    else:
        first = _dot_sel_l(esel_ref[...], prev_ref[...])
        ps = jnp.where((rowi & (seq - 1)) == 0, first, rolled)
    pm = p + (ps - p) * mu_ref[...]

    r = pm[:, 0:RW_W]
    k = pm[:, RW_W:2 * RW_W]
    v = pm[:, 2 * RW_W:3 * RW_W]
    wa = pm[:, 3 * RW_W:3 * RW_W + 128]
    g_lo = pm[:, 3 * RW_W + 128:3 * RW_W + 256]
    wa = jnp.where(lo_half, jnp.tanh(wa), wa)
    xwa = _dot(wa.astype(BF16), wwa_ref[...])
    log_w = -_softplus(-(w0_ref[...] + xwa[:, :RW_W])) - 0.5
    ld = -jnp.exp(log_w)
    a = _sigmoid(a0_ref[...] + xwa[:, RW_W:])
    gate = _dot(_sigmoid(g_lo).astype(BF16), g2_ref[...])
    g128 = g128_ref[...]

    def gsum(z):
        zs = jnp.concatenate([z[:, i * 128:(i + 1) * 128] for i in range(RW_PAIRS)], axis=0)
        hi, lo = split2(zs)
        s = _dot(hi, g128) + _dot(lo, g128)
        return jnp.concatenate([s[i * rows:(i + 1) * rows] for i in range(RW_PAIRS)], axis=1)

    kkv = k * kkw_ref[...]
    kk = kkv / jnp.maximum(jnp.sqrt(gsum(kkv * kkv)), 1e-12)
    k2 = k * (1.0 + (a - 1.0) * kaw_ref[...])
    beta = kk * a

    ld_hi, ld_lo = split2(ld)
    cs = _dot(trics_ref[...], ld_hi) + _dot(trics_ref[...], ld_lo)
    gc = cs[:rows]
    gtot = cs[rows:]
    e_g = jnp.exp(gc)
    e_ng = jnp.exp(-gc)
    e_l = jnp.exp(gtot - gc)
    ah = -kk * jnp.exp(gc - ld)
    rh = r * e_g
    bh = beta * e_ng
    kh = k2 * e_ng
    bt = beta * e_l
    kt = k2 * e_l
    e_tot = jnp.exp(gtot)

    ri = lax.broadcasted_iota(I32, (2 * tt, 2 * tt), 0)
    ci = lax.broadcasted_iota(I32, (2 * tt, 2 * tt), 1)
    same = (ri // c) == (ci // c)
    m_strict = jnp.where(same & (ci < ri), 1.0, 0.0)
    m_incl = jnp.where(same & (ci <= ri), 1.0, 0.0)
    eye2 = jnp.where(ri == ci, 1.0, 0.0)
    bd = jnp.where((ri // RW_N) == (ci // RW_N), 1.0, 0.0)

    def stack(z):
        return jnp.concatenate([jnp.where(lo_half, z, 0.0), jnp.where(lo_half, 0.0, z)], axis=0)

    def fold(z):
        return z[:tt] + z[tt:]

    rowc = lax.broadcasted_iota(I32, (tt, 1), 0) // c

    def chunk_expand(z):
        return jnp.concatenate([jnp.where(rowc == ch, z, 0.0) for ch in range(nc)], axis=1).astype(BF16)

    units = [(t, pr) for t in range(ntile) for pr in range(RW_PAIRS)]
    hw = 2 * tt

    def usl(z, u):
        return z[u[0] * tt:(u[0] + 1) * tt, u[1] * 128:(u[1] + 1) * 128]

    def bf(zs):
        return [z.astype(BF16) for z in zs]

    def rows2(a, b):
        return jnp.concatenate([a, b], axis=0)

    xr = [stack(usl(rh, u)) for u in units]
    xab, xrb, xbb, xkb, xvb = (bf([stack(usl(z, u)) for u in units]) for z in (ah, rh, bh, kh, v))
    lhs_ar = [rows2(a_, r_) for a_, r_ in zip(xab, xrb)]
    ab = [_dot_nt(l_, b_) for l_, b_ in zip(lhs_ar, xbb)]
    ak = [_dot_nt(l_, k_) for l_, k_ in zip(lhs_ar, xkb)]
    a_ab = [z[:hw] * m_strict for z in ab]
    a_rb = bf([z[hw:] * m_incl for z in ab])
    a_ak = bf([z[:hw] * m_strict for z in ak])
    a_rk = bf([z[hw:] * m_incl for z in ak])
    tm = [eye2 + z for z in a_ab]
    if c > 2:
        pw = [_dot(z, z) for z in bf(a_ab)]
    n = 2
    while n < c:
        pwb = bf(pw)
        if 2 * n < c:
            both = [_dot(rows2(t_.astype(BF16), p_), p_) for t_, p_ in zip(tm, pwb)]
            tm = [t_ + z[:hw] for t_, z in zip(tm, both)]
            pw = [z[hw:] for z in both]
        else:
            tm = [t_ + _dot(t_.astype(BF16), p_) for t_, p_ in zip(tm, pwb)]
        n *= 2
    tmb = bf(tm)
    akv = [_dot(rows2(p_, q_), x_) for p_, q_, x_ in zip(a_ak, a_rk, xvb)]
    tav = [_dot(t_, jnp.concatenate([x_, z[:hw].astype(BF16)], axis=1))
           for t_, x_, z in zip(tmb, xab, akv)]
    rbt = [_dot(r_, z) for r_, z in zip(a_rb, bf(tav))]
    rpb = bf([fold(x_ + z[:, :128]) for x_, z in zip(xr, rbt)])
    y0 = [fold(z[:, 128:] + w[hw:]) for z, w in zip(rbt, akv)]
    ap = bf([fold(z[:, :128]) for z in tav])
    vp = bf([fold(z[:, 128:]) for z in tav])
    bexp = [chunk_expand(usl(bt, u)) for u in units]
    kexp = [chunk_expand(usl(kt, u)) for u in units]
    vpl = bf([usl(v, u) for u in units])
    mt_all = [_dot_tn(a_, b_) for a_, b_ in zip(ap, bexp)]
    ht_all = [_dot_tn(rows2(p_, q_), rows2(b_, k_)) for p_, q_, b_, k_ in zip(vp, vpl, bexp, kexp)]

    sts = [st_ref[pr] for pr in range(RW_PAIRS)] if chain else None
    y_parts = [[] for _ in units]
    for t in range(ntile):
        for ch in range(nc):
            rs = slice(ch * c, (ch + 1) * c)
            cl = slice(ch * 128, (ch + 1) * 128)
            r0 = t * tt + ch * c
            for pr in range(RW_PAIRS):
                u = t * RW_PAIRS + pr
                mt = mt_all[u][:, cl] * bd + eye2 * e_tot[r0:r0 + 1, pr * 128:(pr + 1) * 128]
                ht = ht_all[u][:, cl] * bd
                if chain:
                    st = sts[pr]
                else:
                    sq = t * nc + ch
                    st = block_diag(s0_ref[sq, 2 * pr], s0_ref[sq, 2 * pr + 1])
                stb = st.astype(BF16)
                y_parts[u].append(_dot_nt(rpb[u][rs], stb) + y0[u][rs])
                st = _dot(stb, mt.astype(BF16)) + ht
                if chain:
                    sts[pr] = st
                else:
                    so_ref[sq, 2 * pr] = st[:RW_N, :RW_N]
                    so_ref[sq, 2 * pr + 1] = st[RW_N:, RW_N:]
    y_tiles = [jnp.concatenate([jnp.concatenate(y_parts[t * RW_PAIRS + pr], axis=0) for pr in range(RW_PAIRS)],
                               axis=1) for t in range(ntile)]
    y = y_tiles[0] if ntile == 1 else jnp.concatenate(y_tiles, axis=0)

    if chain:
        for pr in range(RW_PAIRS):
            st_ref[pr] = sts[pr]

        @pl.when(j == nt - 1)
        def _():
            for pr in range(RW_PAIRS):
                st = st_ref[pr]
                so_ref[0, 2 * pr] = st[:RW_N, :RW_N]
                so_ref[0, 2 * pr + 1] = st[RW_N:, RW_N:]

    inv_n = 1.0 / RW_N
    mean = gsum(y) * inv_n
    yc = y - mean
    var = gsum(yc * yc) * inv_n
    yn = yc * lax.rsqrt(var + RW_GN_EPS) * lnw_ref[...] + lnb_ref[...]
    bonus = gsum(r * k2 * rk_ref[...]) * v
    o_ref[...] = (yn + bonus) * gate


def _rwkv(proj_rw, prev, s0, wts, *, batch, seq, chain):
    n = proj_rw.shape[0]
    if chain:
        rows = _pick_tile(seq, RW_ROWS_CHAIN)
        nseq, c, nt = 1, RW_CHUNK, seq // rows
        prev_spec = pl.BlockSpec((1, 1, RW_COLS), lambda b, j: (b, 0, 0))
    else:
        assert seq & (seq - 1) == 0 and RW_TILE % seq == 0
        rows = _pick_tile(batch * seq, RW_ROWS_SHORT)
        nseq, c, nt = rows // seq, seq, 1
        prev_spec = pl.BlockSpec((nseq, RW_COLS), lambda b, j: (b, 0))
    ri = lax.broadcasted_iota(I32, (rows, rows), 0)
    ci = lax.broadcasted_iota(I32, (rows, rows), 1)
    same = (ri // c) == (ci // c)
    trics = jnp.concatenate([same & (ci <= ri), same], axis=0).astype(BF16)
    gi = lax.broadcasted_iota(I32, (128, 128), 0) // RW_N
    gj = lax.broadcasted_iota(I32, (128, 128), 1) // RW_N
    g128 = (gi == gj).astype(BF16)
    esel = (lax.broadcasted_iota(I32, (rows, nseq), 0) // seq
            == lax.broadcasted_iota(I32, (rows, nseq), 1)).astype(BF16)
    (mu, w0, a0, wwa, g2, kkw, kaw, rk, lnw, lnb) = wts

    def full(arr):
        return pl.BlockSpec(arr.shape, lambda b, j: (0,) * arr.ndim)

    consts = [mu, w0, a0, wwa, g2, kkw, kaw, rk, lnw, lnb, g128, trics, esel]
    in_specs = [pl.BlockSpec((rows, RW_COLS), lambda b, j: (b * nt + j, 0)),
                prev_spec,
                pl.BlockSpec((nseq, RW_H, RW_N, RW_N), lambda b, j: (b, 0, 0, 0))]
    in_specs += [full(t) for t in consts]
    args = [proj_rw, prev, s0] + consts
    kern = functools.partial(_rwkv_kernel, c=c, chain=chain, seq=seq, rows=rows)
    return pl.pallas_call(
        kern,
        grid=(batch // nseq, nt),
        in_specs=in_specs,
        out_specs=[pl.BlockSpec((rows, RW_W), lambda b, j: (b * nt + j, 0)),
                   pl.BlockSpec((nseq, RW_H, RW_N, RW_N), lambda b, j: (b, 0, 0, 0))],
        out_shape=[jax.ShapeDtypeStruct((n, RW_W), F32),
                   jax.ShapeDtypeStruct((batch, RW_H, RW_N, RW_N), F32)],
        scratch_shapes=[pltpu.VMEM((RW_PAIRS, 128, 128), F32), pltpu.VMEM((1, RW_COLS), F32)],
        compiler_params=_cparams(("parallel", "arbitrary")),
        name="rwkv7_mix",
    )(*args)


def _store_row_tiles(ref, x):
    m = x.shape[0]
    for s in range(D_MODEL // 128):
        ref[pl.ds(s, m, stride=8), :] = x[:, s * 128:(s + 1) * 128]


def _load_row_tiles(ref, m):
    return jnp.concatenate([ref[pl.ds(s, m, stride=8), :] for s in range(D_MODEL // 128)], axis=1)


def _outproj_router_kernel(ohg_ref, orw_ref, x_ref, wo_ref, gf_ref, wr_ref, br_ref, tri_ref, cin_ref,
                           x1_ref, hf_ref, idx_ref, gate_ref, pos_ref, cnt_ref, carry_ref):
    i = pl.program_id(0)

    @pl.when(i == 0)
    def _():
        carry_ref[...] = cin_ref[...].astype(F32)

    half = HG_H * HG_DV
    mixed = _dot(ohg_ref[...].astype(BF16), wo_ref[:half, :]) + _dot(orw_ref[...].astype(BF16), wo_ref[half:, :])
    x1 = x_ref[...] + mixed
    x1_ref[...] = x1
    hf = _rms(x1, gf_ref[...])
    _store_row_tiles(hf_ref, hf)

    h1, h2, _ = _split3(hf)
    w1, w2, _ = _split3(wr_ref[...])
    logits = _dot(h1, w1) + _dot(h1, w2) + _dot(h2, w1) + br_ref[...]

    tm = logits.shape[0]
    lane = lax.broadcasted_iota(I32, (tm, N_EXPERTS), 1).astype(F32)
    lane4 = lax.broadcasted_iota(I32, (tm, TOP_K), 1)
    vals = logits
    maskf = jnp.zeros((tm, N_EXPERTS), F32)
    sels, tops = [], []
    idx_out = jnp.zeros((tm, TOP_K), F32)
    for kq in range(TOP_K):
        m = jnp.max(vals, axis=-1, keepdims=True)
        idx = jnp.min(jnp.where(vals == m, lane, float(N_EXPERTS)), axis=-1, keepdims=True)
        sel = lane == idx
        sels.append(sel)
        tops.append(m)
        idx_out = jnp.where(lane4 == kq, idx, idx_out)
        vals = jnp.where(sel, -jnp.inf, vals)
        maskf = maskf + jnp.where(sel, 1.0, 0.0)
    es = [jnp.exp(t - tops[0]) for t in tops]
    denom = es[0] + es[1] + es[2] + es[3]
    gate_out = jnp.zeros((tm, TOP_K), F32)
    for kq in range(TOP_K):
        gate_out = jnp.where(lane4 == kq, es[kq] / denom, gate_out)

    pos = _dot(tri_ref[...], maskf.astype(BF16)) + carry_ref[...]
    pos_out = jnp.zeros((tm, TOP_K), F32)
    for kq in range(TOP_K):
        pk = jnp.sum(jnp.where(sels[kq], pos, 0.0), axis=-1, keepdims=True)
        pos_out = jnp.where(lane4 == kq, pk, pos_out)
    carry_ref[...] = carry_ref[...] + jnp.sum(maskf, axis=0, keepdims=True)
    idx_ref[...] = idx_out.astype(I32)
    gate_ref[...] = gate_out
    pos_ref[...] = pos_out.astype(I32)
    cnt_ref[...] = carry_ref[...].astype(I32)


def _outproj_router(o_hg, o_rw, x, wo_bf16, gf, w_router, b_router, cnt_in, tm):
    n = x.shape[0]
    tri = (lax.broadcasted_iota(I32, (tm, tm), 0) > lax.broadcasted_iota(I32, (tm, tm), 1)).astype(BF16)
    half = HG_H * HG_DV
    row = lambda i: (i, 0)
    fix = lambda i: (0, 0)
    return pl.pallas_call(
        _outproj_router_kernel,
        grid=(n // tm,),
        in_specs=[pl.BlockSpec((tm, half), row), pl.BlockSpec((tm, RW_W), row), pl.BlockSpec((tm, D_MODEL), row),
                  pl.BlockSpec((half + RW_W, D_MODEL), fix), pl.BlockSpec((1, D_MODEL), fix),
                  pl.BlockSpec((D_MODEL, N_EXPERTS), fix), pl.BlockSpec((1, N_EXPERTS), fix),
                  pl.BlockSpec((tm, tm), fix), pl.BlockSpec((1, N_EXPERTS), fix)],
        out_specs=[pl.BlockSpec((tm, D_MODEL), row), pl.BlockSpec((8 * tm, 128), row),
                   pl.BlockSpec((tm, TOP_K), row), pl.BlockSpec((tm, TOP_K), row), pl.BlockSpec((tm, TOP_K), row),
                   pl.BlockSpec((1, N_EXPERTS), fix)],
        out_shape=[jax.ShapeDtypeStruct((n, D_MODEL), F32), jax.ShapeDtypeStruct((8 * n, 128), F32),
                   jax.ShapeDtypeStruct((n, TOP_K), I32), jax.ShapeDtypeStruct((n, TOP_K), F32),
                   jax.ShapeDtypeStruct((n, TOP_K), I32), jax.ShapeDtypeStruct((1, N_EXPERTS), I32)],
        scratch_shapes=[pltpu.VMEM((1, N_EXPERTS), F32)],
        compiler_params=_cparams(("arbitrary",)),
        name="out_proj_router",
    )(o_hg, o_rw, x, wo_bf16, gf, w_router, b_router, tri, cnt_in)


def _plan_kernel(cnt_ref, idx_ref, pos_ref, dest_ref, bexp_ref, nblk_ref, *, n_blocks):
    cnt = cnt_ref[...].astype(F32)
    padded = jnp.floor((cnt + (MOE_BLOCK - 1)) * (1.0 / MOE_BLOCK)) * MOE_BLOCK
    ei = lax.broadcasted_iota(I32, (N_EXPERTS, N_EXPERTS), 0)
    ej = lax.broadcasted_iota(I32, (N_EXPERTS, N_EXPERTS), 1)
    upper = (ei < ej).astype(BF16)
    upper_incl = (ei <= ej).astype(BF16)
    start = _dot_sel_r(padded, upper)
    ends = _dot_sel_r(padded, upper_incl)
    idx = idx_ref[...]
    dest = pos_ref[...]
    for e in range(N_EXPERTS):
        dest = dest + jnp.where(idx == e, start[:, e:e + 1].astype(I32), 0)
    dest_ref[...] = dest
    bstart = (lax.broadcasted_iota(I32, (n_blocks, 1), 0) * MOE_BLOCK).astype(F32)
    be = jnp.sum(jnp.where(ends <= bstart, 1.0, 0.0), axis=-1, keepdims=True)
    bexp_ref[...] = jnp.minimum(be, N_EXPERTS - 1.0).astype(I32)
    nblk_ref[...] = (ends[:, N_EXPERTS - 1:] * (1.0 / MOE_BLOCK)).astype(I32)


def _plan(cnt, idx, pos, n_blocks, tm):
    n = idx.shape[0]
    row = lambda i: (i, 0)
    fix = lambda i: (0, 0)
    return pl.pallas_call(
        functools.partial(_plan_kernel, n_blocks=n_blocks),
        grid=(n // tm,),
        in_specs=[pl.BlockSpec((1, N_EXPERTS), fix), pl.BlockSpec((tm, TOP_K), row), pl.BlockSpec((tm, TOP_K), row)],
        out_specs=[pl.BlockSpec((tm, TOP_K), row), pl.BlockSpec((n_blocks, 1), fix), pl.BlockSpec((1, 1), fix)],
        out_shape=[jax.ShapeDtypeStruct((n, TOP_K), I32), jax.ShapeDtypeStruct((n_blocks, 1), I32),
                   jax.ShapeDtypeStruct((1, 1), I32)],
        compiler_params=_cparams(("arbitrary",)),
        name="moe_plan",
    )(cnt, idx, pos)


def _row_tile(ref, r):
    return ref.at[pl.ds(pl.multiple_of(r * 8, 8), 8)]


def _scatter_kernel(cnt_ref, nblk_ref, dest_a_ref, dest_b_ref, hf_a_ref, hf_b_ref, xs_ref, zbuf, sem, sem_pad, sem_blk,
                    *, ts, nt_a, n_blocks):
    i = pl.program_id(0)
    blk_rows = 8 * MOE_BLOCK
    shift = MOE_BLOCK.bit_length() - 1

    @pl.when(i == 0)
    def _():
        zbuf[...] = jnp.zeros_like(zbuf)

        def pad_expert(e, start):
            c = cnt_ref[e]
            padded = ((c + (MOE_BLOCK - 1)) >> shift) << shift

            def zrow(r, carry):
                pltpu.make_async_copy(_row_tile(zbuf, 0), _row_tile(xs_ref, start + r), sem_pad).start()
                return carry

            def wrow(r, carry):
                pltpu.make_async_copy(_row_tile(zbuf, 0), _row_tile(xs_ref, 0), sem_pad).wait()
                return carry

            lax.fori_loop(c, padded, zrow, 0)
            lax.fori_loop(c, padded, wrow, 0)
            return start + padded

        lax.fori_loop(0, N_EXPERTS, pad_expert, 0)

        def zblk(b, carry):
            pltpu.make_async_copy(zbuf, xs_ref.at[pl.ds(pl.multiple_of(b * blk_rows, blk_rows), blk_rows)],
                                  sem_blk).start()
            return carry

        def wblk(b, carry):
            pltpu.make_async_copy(zbuf, xs_ref.at[pl.ds(0, blk_rows)], sem_blk).wait()
            return carry

        lax.fori_loop(nblk_ref[0], n_blocks, zblk, 0)
        lax.fori_loop(nblk_ref[0], n_blocks, wblk, 0)

    def scatter_tile(dest_ref, hf_ref):
        def issue(t, carry):
            for kq in range(TOP_K):
                d = dest_ref[t * TOP_K + kq]
                pltpu.make_async_copy(_row_tile(hf_ref, t), _row_tile(xs_ref, d), sem).start()
            return carry

        def drain(t, carry):
            for kq in range(TOP_K):
                pltpu.make_async_copy(_row_tile(hf_ref, 0), _row_tile(xs_ref, 0), sem).wait()
            return carry

        lax.fori_loop(0, ts, issue, 0)
        lax.fori_loop(0, ts, drain, 0)

    @pl.when(i < nt_a)
    def _():
        scatter_tile(dest_a_ref, hf_a_ref)

    @pl.when(i >= nt_a)
    def _():
        scatter_tile(dest_b_ref, hf_b_ref)


def _scatter(cnt, nblk, dest_a, dest_b, hf_a, hf_b, n_blocks, ts):
    assert MOE_BLOCK & (MOE_BLOCK - 1) == 0
    nt_a = hf_a.shape[0] // (8 * ts)
    nt_b = hf_b.shape[0] // (8 * ts)
    amap = lambda i: (jnp.minimum(i, nt_a - 1),)
    bmap = lambda i: (jnp.maximum(i - nt_a, 0),)
    smem = pl.BlockSpec(memory_space=pltpu.SMEM)
    return pl.pallas_call(
        functools.partial(_scatter_kernel, ts=ts, nt_a=nt_a, n_blocks=n_blocks),
        grid=(nt_a + nt_b,),
        in_specs=[smem, smem,
                  pl.BlockSpec((ts * TOP_K,), amap, memory_space=pltpu.SMEM),
                  pl.BlockSpec((ts * TOP_K,), bmap, memory_space=pltpu.SMEM),
                  pl.BlockSpec((8 * ts, 128), lambda i: (jnp.minimum(i, nt_a - 1), 0)),
                  pl.BlockSpec((8 * ts, 128), lambda i: (jnp.maximum(i - nt_a, 0), 0))],
        out_specs=pl.BlockSpec(memory_space=pl.ANY),
        out_shape=jax.ShapeDtypeStruct((8 * n_blocks * MOE_BLOCK, 128), F32),
        scratch_shapes=[pltpu.VMEM((8 * MOE_BLOCK, 128), F32), pltpu.SemaphoreType.DMA(()),
                        pltpu.SemaphoreType.DMA(()), pltpu.SemaphoreType.DMA(())],
        compiler_params=_cparams(("arbitrary",)),
        name="moe_scatter",
    )(cnt, nblk, dest_a, dest_b, hf_a, hf_b)


def _ffn_kernel(bexp_ref, nblk_ref, xs_ref, wgu_ref, bgu_ref, wd_ref, bd_ref, yb_ref, wgu_bf, wd_bf):
    b = pl.program_id(0)
    valid = b < nblk_ref[0]
    prev_e = bexp_ref[jnp.maximum(b, 1) - 1]
    fresh = (b == 0) | (bexp_ref[b] != prev_e)

    @pl.when(valid & fresh)
    def _():
        wgu_bf[...] = wgu_ref[0].astype(BF16)
        wd_bf[...] = wd_ref[0].astype(BF16)

    @pl.when(valid)
    def _():
        x = _load_row_tiles(xs_ref, MOE_BLOCK).astype(BF16)
        gu = _dot(x, wgu_bf[...]) + bgu_ref[0]
        gate = jnp.minimum(gu[:, :D_FF], SWIGLU_LIMIT)
        up = jnp.clip(gu[:, D_FF:], -SWIGLU_LIMIT, SWIGLU_LIMIT)
        glu = gate * _sigmoid(SWIGLU_ALPHA * gate)
        act = ((up + 1.0) * glu).astype(BF16)
        _store_row_tiles(yb_ref, _dot(act, wd_bf[...]) + bd_ref[0])

    @pl.when(jnp.logical_not(valid))
    def _():
        yb_ref[...] = jnp.zeros_like(yb_ref)


def _ffn(bexp, nblk, xs, w_gate_up, b_gate_up, w_down, b_down):
    n_rows = xs.shape[0] // 8
    n_blocks = n_rows // MOE_BLOCK

    def blk(b, be, nb):
        return (jnp.minimum(b, jnp.maximum(nb[0], 1) - 1), 0)

    def wmap(b, be, nb):
        return (be[jnp.minimum(b, jnp.maximum(nb[0], 1) - 1)], 0, 0)

    grid_spec = pltpu.PrefetchScalarGridSpec(
        num_scalar_prefetch=2,
        grid=(n_blocks,),
        in_specs=[pl.BlockSpec((8 * MOE_BLOCK, 128), blk),
                  pl.BlockSpec((1, D_MODEL, 2 * D_FF), wmap),
                  pl.BlockSpec((1, 1, 2 * D_FF), wmap),
                  pl.BlockSpec((1, D_FF, D_MODEL), wmap),
                  pl.BlockSpec((1, 1, D_MODEL), wmap)],
        out_specs=pl.BlockSpec((8 * MOE_BLOCK, 128), lambda b, be, nb: (b, 0)),
        scratch_shapes=[pltpu.VMEM((D_MODEL, 2 * D_FF), BF16), pltpu.VMEM((D_FF, D_MODEL), BF16)],
    )
    return pl.pallas_call(
        _ffn_kernel,
        grid_spec=grid_spec,
        out_shape=jax.ShapeDtypeStruct((8 * n_rows, 128), F32),
        compiler_params=_cparams(("arbitrary",)),
        name="moe_ffn",
    )(bexp, nblk, xs, w_gate_up, b_gate_up, w_down, b_down)


def _combine_kernel(dest_ref, gate_ref, x1_ref, gn_ref, yb_ref, y_ref, buf, sem, *, tc):
    def issue(t, carry):
        for kq in range(TOP_K):
            d = dest_ref[t * TOP_K + kq]
            pltpu.make_async_copy(_row_tile(yb_ref, d), _row_tile(buf.at[kq], t), sem).start()
        return carry

    lax.fori_loop(0, tc, issue, 0)

    def drain(t, carry):
        for kq in range(TOP_K):
            pltpu.make_async_copy(_row_tile(yb_ref, 0), _row_tile(buf.at[0], 0), sem).wait()
        return carry

    lax.fori_loop(0, tc, drain, 0)
    gate = gate_ref[...]
    acc = x1_ref[...]
    for kq in range(TOP_K):
        acc = acc + gate[:, kq:kq + 1] * _load_row_tiles(buf.at[kq], tc)
    y_ref[...] = _rms(acc, gn_ref[...])


def _combine(dest_flat, gate, x1, gn, yb, tc):
    n = x1.shape[0]
    return pl.pallas_call(
        functools.partial(_combine_kernel, tc=tc),
        grid=(n // tc,),
        in_specs=[pl.BlockSpec((tc * TOP_K,), lambda i: (i,), memory_space=pltpu.SMEM),
                  pl.BlockSpec((tc, TOP_K), lambda i: (i, 0)),
                  pl.BlockSpec((tc, D_MODEL), lambda i: (i, 0)),
                  pl.BlockSpec((1, D_MODEL), lambda i: (0, 0)),
                  pl.BlockSpec(memory_space=pl.ANY)],
        out_specs=pl.BlockSpec((tc, D_MODEL), lambda i: (i, 0)),
        out_shape=jax.ShapeDtypeStruct((n, D_MODEL), F32),
        scratch_shapes=[pltpu.VMEM((TOP_K, 8 * tc, 128), F32), pltpu.SemaphoreType.DMA(())],
        compiler_params=_cparams(("arbitrary",)),
        name="moe_combine",
    )(dest_flat, gate, x1, gn, yb)


def _pick_tile(n, pref):
    t = pref
    while n % t:
        t //= 2
    return t


def kernel(x_prompt, x_sample, state_hgrn, state_rwkv, state_shift, norm_mix, w_in, lb_logits, hg_norm_w, rw_mu,
           rw_w0, rw_w2, rw_a0, rw_a2, rw_g2, rw_k_k, rw_k_a, rw_r_k, rw_lnx_w, rw_lnx_b, w_out, norm_ffn,
           w_router, b_router, w_gate_up, b_gate_up, w_down, b_down, norm_final):
    bp, tp, d = x_prompt.shape
    bs, tsq, _ = x_sample.shape
    assert norm_mix.shape[0] == 1 and d == D_MODEL
    n_p, n_s = bp * tp, bs * tsq
    n = n_p + n_s
    xp = x_prompt.reshape(n_p, d)
    xs_ = x_sample.reshape(n_s, d)

    w_in_b = w_in[0].astype(BF16)
    splits = (HG_COLS, RW_COLS)
    hgp_p, rwp_p = _proj(xp, norm_mix, w_in_b, splits, True, _pick_tile(n_p, 256))
    hgp_s, rwp_s = _proj(xs_, norm_mix, w_in_b, splits, True, _pick_tile(n_s, 256))
    ones_d = jnp.ones((1, d), F32)
    (prev_s,) = _proj(state_shift[0], ones_d, w_in_b[:, HG_COLS:], (RW_COLS,), False, _pick_tile(bs, 128))
    prev_p = jnp.zeros((bp, 1, RW_COLS), F32)
    x_last = jnp.concatenate([x_prompt[:, -1, :], x_sample[:, -1, :]], axis=0)
    shift = _rmsnorm_rows(x_last, norm_mix)

    zero_hg = jnp.zeros((bp, HG_H, HG_DK, HG_DV), F32)
    ohg_p, hg_p = _hgrn(hgp_p, lb_logits, hg_norm_w, zero_hg, batch=bp, seq=tp,
                        nseq=1, tb=_pick_tile(tp, 256), c=HG_CHUNK, sub=HG_SUB)
    ohg_s, hg_s = _hgrn(hgp_s, lb_logits, hg_norm_w, state_hgrn[0], batch=bs, seq=tsq,
                        nseq=_pick_tile(bs, 16), tb=tsq, c=tsq, sub=tsq)

    zpad = jnp.zeros((64, RW_W), F32)
    wwa = jnp.concatenate([jnp.concatenate([rw_w2[0], zpad], axis=1),
                           jnp.concatenate([zpad, rw_a2[0]], axis=1)], axis=0).astype(BF16)
    wts = (rw_mu, rw_w0, rw_a0, wwa, rw_g2[0].astype(BF16), rw_k_k, rw_k_a, rw_r_k.reshape(1, RW_W),
           rw_lnx_w, rw_lnx_b)
    zero_rw = jnp.zeros((bp, RW_H, RW_N, RW_N), F32)
    orw_p, rw_p = _rwkv(rwp_p, prev_p, zero_rw, wts, batch=bp, seq=tp, chain=True)
    orw_s, rw_s = _rwkv(rwp_s, prev_s, state_rwkv[0], wts, batch=bs, seq=tsq, chain=False)

    wo_b = w_out[0].astype(BF16)
    tm_p, tm_s = _pick_tile(n_p, 512), _pick_tile(n_s, 512)
    cnt0 = jnp.zeros((1, N_EXPERTS), I32)
    x1_p, hf_p, idx_p, gate_p, pos_p, cnt_p = _outproj_router(ohg_p, orw_p, xp, wo_b, norm_ffn, w_router[0],
                                                              b_router, cnt0, tm_p)
    x1_s, hf_s, idx_s, gate_s, pos_s, cnt = _outproj_router(ohg_s, orw_s, xs_, wo_b, norm_ffn, w_router[0],
                                                            b_router, cnt_p, tm_s)

    n_blocks = -(-(n * TOP_K) // MOE_BLOCK) + N_EXPERTS
    dest_p, bexp, nblk = _plan(cnt, idx_p, pos_p, n_blocks, tm_p)
    dest_s, _, _ = _plan(cnt, idx_s, pos_s, n_blocks, tm_s)
    dest_p = dest_p.reshape(n_p * TOP_K)
    dest_s = dest_s.reshape(n_s * TOP_K)
    nblk = nblk.reshape(1)
    ts = _pick_tile(math.gcd(n_p, n_s), 512)
    xs = _scatter(cnt.reshape(N_EXPERTS), nblk, dest_p, dest_s, hf_p, hf_s, n_blocks, ts)
    yb = _ffn(bexp.reshape(n_blocks), nblk, xs, w_gate_up[0], b_gate_up[0].reshape(N_EXPERTS, 1, 2 * D_FF),
              w_down[0], b_down[0].reshape(N_EXPERTS, 1, d))
    gn = norm_final.reshape(1, d)
    y_p = _combine(dest_p, gate_p, x1_p, gn, yb, _pick_tile(n_p, 256))
    y_s = _combine(dest_s, gate_s, x1_s, gn, yb, _pick_tile(n_s, 256))

    return (y_p.reshape(bp, tp, d), y_s.reshape(bs, tsq, d), hg_p[None], rw_p[None], shift[:bp][None],
            hg_s[None], rw_s[None], shift[bp:][None])
```

```python
import functools
import math

import jax
import jax.numpy as jnp
from jax import lax
from jax.experimental import pallas as pl
from jax.experimental.pallas import tpu as pltpu

F32 = jnp.float32
BF16 = jnp.bfloat16
I32 = jnp.int32

D_MODEL = 1024
HG_H, HG_DK, HG_DV = 4, 128, 128
HG_CHUNK = 64
HG_SUB = 16
RW_H, RW_N = 8, 64
RW_W = RW_H * RW_N
RW_PAIRS = RW_H // 2
RW_TILE = 64
RW_CHUNK = 32
RW_ROWS_CHAIN = 256
RW_ROWS_SHORT = 128
HG_COLS = 2 * HG_H * HG_DK + 2 * HG_H * HG_DV
RW_COLS = 3 * RW_W + 64 + 64 + 128
N_EXPERTS = 32
TOP_K = 4
D_FF = D_MODEL
SWIGLU_LIMIT = 7.0
SWIGLU_ALPHA = 1.702
NORM_EPS = 1e-6
RW_GN_EPS = 64e-5
MOE_BLOCK = 256
MOE_TOKENS = 256
MOE_CHUNK = 64
VMEM_LIMIT = 56 * 1024 * 1024


def _dot(a, b):
    return jnp.dot(a, b, preferred_element_type=F32)


def _dot_nt(a, b):
    return lax.dot_general(a, b, (((1,), (1,)), ((), ())), preferred_element_type=F32)


def _dot_tn(a, b):
    return lax.dot_general(a, b, (((0,), (0,)), ((), ())), preferred_element_type=F32)


def _split3(x):
    x1 = x.astype(BF16)
    r1 = x - x1.astype(F32)
    x2 = r1.astype(BF16)
    x3 = (r1 - x2.astype(F32)).astype(BF16)
    return x1, x2, x3


def _dot_sel_l(m_bf16, x):
    x1, x2, x3 = _split3(x)
    return _dot(m_bf16, x1) + _dot(m_bf16, x2) + _dot(m_bf16, x3)


def _dot_sel_r(x, m_bf16):
    x1, x2, x3 = _split3(x)
    return _dot(x1, m_bf16) + _dot(x2, m_bf16) + _dot(x3, m_bf16)


def _sigmoid(x):
    return 1.0 / (1.0 + jnp.exp(-x))


def _rms(x, g):
    return x * lax.rsqrt(jnp.mean(x * x, axis=-1, keepdims=True) + NORM_EPS) * g


def _cparams(sem):
    return pltpu.CompilerParams(dimension_semantics=sem, vmem_limit_bytes=VMEM_LIMIT)


def _proj_kernel(x_ref, g_ref, w_ref, *o_refs, normalize, splits):
    x = x_ref[...]
    h = _rms(x, g_ref[...]) if normalize else x
    hb = h.astype(BF16)
    c0 = 0
    for o_ref, width in zip(o_refs, splits):
        o_ref[...] = _dot(hb, w_ref[:, c0:c0 + width])
        c0 += width


def _proj(x, g, w_bf16, splits, normalize, tm):
    n = x.shape[0]
    kern = functools.partial(_proj_kernel, normalize=normalize, splits=splits)
    return pl.pallas_call(
        kern,
        grid=(n // tm,),
        in_specs=[pl.BlockSpec((tm, D_MODEL), lambda i: (i, 0)),
                  pl.BlockSpec((1, D_MODEL), lambda i: (0, 0)),
                  pl.BlockSpec((D_MODEL, sum(splits)), lambda i: (0, 0))],
        out_specs=[pl.BlockSpec((tm, s), lambda i: (i, 0)) for s in splits],
        out_shape=[jax.ShapeDtypeStruct((n, s), F32) for s in splits],
        compiler_params=_cparams(("parallel",)),
        name="norm_in_proj" if normalize else "shift_proj",
    )(x, g, w_bf16)


def _rmsnorm_rows_kernel(x_ref, g_ref, o_ref):
    o_ref[...] = _rms(x_ref[...], g_ref[...])


def _rmsnorm_rows(x, g):
    return pl.pallas_call(
        _rmsnorm_rows_kernel,
        out_shape=jax.ShapeDtypeStruct(x.shape, F32),
        name="shift_norm",
    )(x, g)


def _hgrn_chunk(qs, fps, ivs, lbs, sts, tri, c, sub):
    nh = len(qs)
    hs = range(nh)
    qf = [q * _sigmoid(q) for q in qs]
    f = [lb + (1.0 - lb) * _sigmoid(fp) for lb, fp in zip(lbs, fps)]
    lf = [jnp.log(z) for z in f]
    kc = [1.0 - z for z in f]
    g = [_dot_sel_l(tri, z) for z in lf]
    o_inter = [_dot_nt((qf[h] * jnp.exp(g[h])).astype(BF16), sts[h].astype(BF16)) for h in hs]
    ivb = [z.astype(BF16) for z in ivs]
    lane = lax.broadcasted_iota(I32, (sub, sub), 1)
    row = lax.broadcasted_iota(I32, (sub, sub), 0)
    outs = [[] for _ in hs]
    for i in range(c // sub):
        lo = i * sub
        blk = slice(lo, lo + sub)
        a = [jnp.zeros((sub, sub), F32) for _ in hs]
        for s in range(sub):
            for h in hs:
                gi = g[h][blk]
                e = jnp.exp(jnp.minimum(gi - gi[s:s + 1, :], 0.0))
                col = jnp.sum(qf[h][blk] * (kc[h][lo + s:lo + s + 1, :] * e), axis=-1, keepdims=True)
                a[h] = jnp.where(lane == s, col, a[h])
        a = [jnp.where(row >= lane, z, 0.0) for z in a]
        oi = [o_inter[h][blk] + _dot(a[h].astype(BF16), ivb[h][blk]) for h in hs]
        if i > 0:
            gr = [g[h][lo - 1:lo, :] for h in hs]
            qd = [(qf[h][blk] * jnp.exp(g[h][blk] - gr[h])).astype(BF16) for h in hs]
            kd = [(kc[h][:lo] * jnp.exp(gr[h] - g[h][:lo])).astype(BF16) for h in hs]
            a_off = [_dot_nt(qd[h], kd[h]).astype(BF16) for h in hs]
            oi = [oi[h] + _dot(a_off[h], ivb[h][:lo]) for h in hs]
        for h in hs:
            outs[h].append(oi[h])
    o = [z[0] if len(z) == 1 else jnp.concatenate(z, axis=0) for z in outs]
    gl = [g[h][c - 1:c, :] for h in hs]
    kd = [(kc[h] * jnp.exp(gl[h] - g[h])).astype(BF16) for h in hs]
    st_new = [sts[h] * jnp.exp(gl[h]) + _dot_tn(ivb[h], kd[h]) for h in hs]
    return o, st_new


def _hgrn_kernel(q_ref, f_ref, i_ref, g_ref, lbl_ref, gw_ref, tri_ref, s0_ref, o_ref, so_ref, st_ref,
                 *, nseq, tb, c, sub):
    j = pl.program_id(1)
    nt = pl.num_programs(1)

    @pl.when(j == 0)
    def _():
        for s in range(nseq):
            for h in range(HG_H):
                st_ref[s * HG_H + h] = s0_ref[s, h].T

    lbl = lbl_ref[...]
    ex = jnp.exp(lbl - jnp.max(lbl, axis=0, keepdims=True))
    lb = ex[0:1, :] / jnp.sum(ex, axis=0, keepdims=True)
    gw = gw_ref[...]
    tri = tri_ref[...]
    n_chunks = tb // c

    def body(it, carry):
        s = it // n_chunks
        r0 = pl.multiple_of(it * c, c)
        rows = pl.ds(r0, c)
        hsl = [slice(h * HG_DK, (h + 1) * HG_DK) for h in range(HG_H)]
        o, st_new = _hgrn_chunk([q_ref[rows, z] for z in hsl], [f_ref[rows, z] for z in hsl],
                                [i_ref[rows, z] for z in hsl], [lb[:, z] for z in hsl],
                                [st_ref[s * HG_H + h] for h in range(HG_H)], tri, c, sub)
        for h in range(HG_H):
            st_ref[s * HG_H + h] = st_new[h]
            on = o[h] * lax.rsqrt(jnp.mean(o[h] * o[h], axis=-1, keepdims=True) + NORM_EPS) * gw
            gin = g_ref[rows, hsl[h]]
            o_ref[rows, hsl[h]] = on * (gin * _sigmoid(gin))
        return carry

    lax.fori_loop(0, nseq * n_chunks, body, 0)

    @pl.when(j == nt - 1)
    def _():
        for s in range(nseq):
            for h in range(HG_H):
                so_ref[s, h] = st_ref[s * HG_H + h].T


def _hgrn(proj_hg, lb_logits, gn_w, s0, *, batch, seq, nseq, tb, c, sub):
    n = proj_hg.shape[0]
    rows = nseq * tb
    nt = seq // tb
    hw = HG_H * HG_DK
    tri = (lax.broadcasted_iota(I32, (c, c), 0) >= lax.broadcasted_iota(I32, (c, c), 1)).astype(BF16)

    def rmap(col):
        return lambda b, j: (b * nt + j, col)

    kern = functools.partial(_hgrn_kernel, nseq=nseq, tb=tb, c=c, sub=sub)
    in_specs = [pl.BlockSpec((rows, hw), rmap(0)), pl.BlockSpec((rows, hw), rmap(1)),
                pl.BlockSpec((rows, hw), rmap(2)), pl.BlockSpec((rows, hw), rmap(3)),
                pl.BlockSpec((lb_logits.shape[0], hw), lambda b, j: (0, 0)),
                pl.BlockSpec((1, HG_DV), lambda b, j: (0, 0)),
                pl.BlockSpec((c, c), lambda b, j: (0, 0)),
                pl.BlockSpec((nseq, HG_H, HG_DK, HG_DV), lambda b, j: (b, 0, 0, 0))]
    args = [proj_hg, proj_hg, proj_hg, proj_hg, lb_logits, gn_w, tri, s0]
    return pl.pallas_call(
        kern,
        grid=(batch // nseq, nt),
        in_specs=in_specs,
        out_specs=[pl.BlockSpec((rows, hw), lambda b, j: (b * nt + j, 0)),
                   pl.BlockSpec((nseq, HG_H, HG_DK, HG_DV), lambda b, j: (b, 0, 0, 0))],
        out_shape=[jax.ShapeDtypeStruct((n, HG_H * HG_DV), F32),
                   jax.ShapeDtypeStruct((batch, HG_H, HG_DK, HG_DV), F32)],
        scratch_shapes=[pltpu.VMEM((nseq * HG_H, HG_DV, HG_DK), F32)],
        compiler_params=_cparams(("parallel", "arbitrary")),
        name="hgrn2_mix",
    )(*args)


def _softplus(z):
    return jnp.maximum(z, 0.0) + jnp.log(1.0 + jnp.exp(-jnp.abs(z)))


def _rwkv_kernel(p_ref, prev_ref, s0_ref, mu_ref, w0_ref, a0_ref, wwa_ref, g2_ref, kkw_ref, kaw_ref, rk_ref,
                 lnw_ref, lnb_ref, g128_ref, trics_ref, esel_ref, o_ref, so_ref, st_ref, carry_ref,
                 *, c, chain, seq, rows):
    tt = RW_TILE
    ntile = rows // tt
    nc = tt // c
    j = pl.program_id(1)
    nt = pl.num_programs(1)
    lane128 = lax.broadcasted_iota(I32, (1, 128), 1)
    lo_half = lane128 < RW_N

    def block_diag(se, so):
        z = jnp.zeros((RW_N, RW_N), F32)
        return jnp.concatenate([jnp.concatenate([se, z], axis=1), jnp.concatenate([z, so], axis=1)], axis=0)

    def split2(z):
        hi = z.astype(BF16)
        return hi, (z - hi.astype(F32)).astype(BF16)

    p = p_ref[...]
    rolled = pltpu.roll(p, 1, 0)
    rowi = lax.broadcasted_iota(I32, (rows, 1), 0)
    if chain:
        @pl.when(j == 0)
        def _():
            carry_ref[...] = prev_ref[0]
            for pr in range(RW_PAIRS):
                st_ref[pr] = block_diag(s0_ref[0, 2 * pr], s0_ref[0, 2 * pr + 1])

        ps = jnp.where(rowi == 0, carry_ref[...], rolled)
        carry_ref[...] = p[rows - 1:rows, :]
    else:
        first = _dot_sel_l(esel_ref[...], prev_ref[...])
        ps = jnp.where((rowi & (seq - 1)) == 0, first, rolled)
    pm = p + (ps - p) * mu_ref[...]

    r = pm[:, 0:RW_W]
    k = pm[:, RW_W:2 * RW_W]
    v = pm[:, 2 * RW_W:3 * RW_W]
    wa = pm[:, 3 * RW_W:3 * RW_W + 128]
    g_lo = pm[:, 3 * RW_W + 128:3 * RW_W + 256]
    wa = jnp.where(lo_half, jnp.tanh(wa), wa)
    xwa = _dot(wa.astype(BF16), wwa_ref[...])
    log_w = -_softplus(-(w0_ref[...] + xwa[:, :RW_W])) - 0.5
    ld = -jnp.exp(log_w)
    a = _sigmoid(a0_ref[...] + xwa[:, RW_W:])
    gate = _dot(_sigmoid(g_lo).astype(BF16), g2_ref[...])
    g128 = g128_ref[...]

    def gsum(z):
        zs = jnp.concatenate([z[:, i * 128:(i + 1) * 128] for i in range(RW_PAIRS)], axis=0)
        hi, lo = split2(zs)
        s = _dot(hi, g128) + _dot(lo, g128)
        return jnp.concatenate([s[i * rows:(i + 1) * rows] for i in range(RW_PAIRS)], axis=1)

    kkv = k * kkw_ref[...]
    kk = kkv / jnp.maximum(jnp.sqrt(gsum(kkv * kkv)), 1e-12)
    k2 = k * (1.0 + (a - 1.0) * kaw_ref[...])
    beta = kk * a

    ld_hi, ld_lo = split2(ld)
    cs = _dot(trics_ref[...], ld_hi) + _dot(trics_ref[...], ld_lo)
    gc = cs[:rows]
    gtot = cs[rows:]
    e_g = jnp.exp(gc)
    e_ng = jnp.exp(-gc)
    e_l = jnp.exp(gtot - gc)
    ah = -kk * jnp.exp(gc - ld)
    rh = r * e_g
    bh = beta * e_ng
    kh = k2 * e_ng
    bt = beta * e_l
    kt = k2 * e_l
    e_tot = jnp.exp(gtot)

    ri = lax.broadcasted_iota(I32, (2 * tt, 2 * tt), 0)
    ci = lax.broadcasted_iota(I32, (2 * tt, 2 * tt), 1)
    same = (ri // c) == (ci // c)
    m_strict = jnp.where(same & (ci < ri), 1.0, 0.0)
    m_incl = jnp.where(same & (ci <= ri), 1.0, 0.0)
    eye2 = jnp.where(ri == ci, 1.0, 0.0)
    bd = jnp.where((ri // RW_N) == (ci // RW_N), 1.0, 0.0)

    def stack(z):
        return jnp.concatenate([jnp.where(lo_half, z, 0.0), jnp.where(lo_half, 0.0, z)], axis=0)

    def fold(z):
        return z[:tt] + z[tt:]

    rowc = lax.broadcasted_iota(I32, (tt, 1), 0) // c

    def chunk_expand(z):
        return jnp.concatenate([jnp.where(rowc == ch, z, 0.0) for ch in range(nc)], axis=1).astype(BF16)

    units = [(t, pr) for t in range(ntile) for pr in range(RW_PAIRS)]
    hw = 2 * tt

    def usl(z, u):
        return z[u[0] * tt:(u[0] + 1) * tt, u[1] * 128:(u[1] + 1) * 128]

    def bf(zs):
        return [z.astype(BF16) for z in zs]

    def rows2(a, b):
        return jnp.concatenate([a, b], axis=0)

    xr = [stack(usl(rh, u)) for u in units]
    xab, xrb, xbb, xkb, xvb = (bf([stack(usl(z, u)) for u in units]) for z in (ah, rh, bh, kh, v))
    lhs_ar = [rows2(a_, r_) for a_, r_ in zip(xab, xrb)]
    ab = [_dot_nt(l_, b_) for l_, b_ in zip(lhs_ar, xbb)]
    ak = [_dot_nt(l_, k_) for l_, k_ in zip(lhs_ar, xkb)]
    a_ab = [z[:hw] * m_strict for z in ab]
    a_rb = bf([z[hw:] * m_incl for z in ab])
    a_ak = bf([z[:hw] * m_strict for z in ak])
    a_rk = bf([z[hw:] * m_incl for z in ak])
    tm = [eye2 + z for z in a_ab]
    if c > 2:
        pw = [_dot(z, z) for z in bf(a_ab)]
    n = 2
    while n < c:
        pwb = bf(pw)
        if 2 * n < c:
            both = [_dot(rows2(t_.astype(BF16), p_), p_) for t_, p_ in zip(tm, pwb)]
            tm = [t_ + z[:hw] for t_, z in zip(tm, both)]
            pw = [z[hw:] for z in both]
        else:
            tm = [t_ + _dot(t_.astype(BF16), p_) for t_, p_ in zip(tm, pwb)]
        n *= 2
    tmb = bf(tm)
    akv = [_dot(rows2(p_, q_), x_) for p_, q_, x_ in zip(a_ak, a_rk, xvb)]
    tav = [_dot(t_, jnp.concatenate([x_, z[:hw].astype(BF16)], axis=1))
           for t_, x_, z in zip(tmb, xab, akv)]
    rbt = [_dot(r_, z) for r_, z in zip(a_rb, bf(tav))]
    rpb = bf([fold(x_ + z[:, :128]) for x_, z in zip(xr, rbt)])
    y0 = [fold(z[:, 128:] + w[hw:]) for z, w in zip(rbt, akv)]
    ap = bf([fold(z[:, :128]) for z in tav])
    vp = bf([fold(z[:, 128:]) for z in tav])
    bexp = [chunk_expand(usl(bt, u)) for u in units]
    kexp = [chunk_expand(usl(kt, u)) for u in units]
    vpl = bf([usl(v, u) for u in units])
    mt_all = [_dot_tn(a_, b_) for a_, b_ in zip(ap, bexp)]
    ht_all = [_dot_tn(rows2(p_, q_), rows2(b_, k_)) for p_, q_, b_, k_ in zip(vp, vpl, bexp, kexp)]

    sts = [st_ref[pr] for pr in range(RW_PAIRS)] if chain else None
    y_parts = [[] for _ in units]
    for t in range(ntile):
        for ch in range(nc):
            rs = slice(ch * c, (ch + 1) * c)
            cl = slice(ch * 128, (ch + 1) * 128)
            r0 = t * tt + ch * c
            for pr in range(RW_PAIRS):
                u = t * RW_PAIRS + pr
                mt = mt_all[u][:, cl] * bd + eye2 * e_tot[r0:r0 + 1, pr * 128:(pr + 1) * 128]
                ht = ht_all[u][:, cl] * bd
                if chain:
                    st = sts[pr]
                else:
                    sq = t * nc + ch
                    st = block_diag(s0_ref[sq, 2 * pr], s0_ref[sq, 2 * pr + 1])
                stb = st.astype(BF16)
                y_parts[u].append(_dot_nt(rpb[u][rs], stb) + y0[u][rs])
                st = _dot(stb, mt.astype(BF16)) + ht
                if chain:
                    sts[pr] = st
                else:
                    so_ref[sq, 2 * pr] = st[:RW_N, :RW_N]
                    so_ref[sq, 2 * pr + 1] = st[RW_N:, RW_N:]
    y_tiles = [jnp.concatenate([jnp.concatenate(y_parts[t * RW_PAIRS + pr], axis=0) for pr in range(RW_PAIRS)],
                               axis=1) for t in range(ntile)]
    y = y_tiles[0] if ntile == 1 else jnp.concatenate(y_tiles, axis=0)

    if chain:
        for pr in range(RW_PAIRS):
            st_ref[pr] = sts[pr]

        @pl.when(j == nt - 1)
        def _():
            for pr in range(RW_PAIRS):
                st = st_ref[pr]
                so_ref[0, 2 * pr] = st[:RW_N, :RW_N]
                so_ref[0, 2 * pr + 1] = st[RW_N:, RW_N:]

    inv_n = 1.0 / RW_N
    mean = gsum(y) * inv_n
    yc = y - mean
    var = gsum(yc * yc) * inv_n
    yn = yc * lax.rsqrt(var + RW_GN_EPS) * lnw_ref[...] + lnb_ref[...]
    bonus = gsum(r * k2 * rk_ref[...]) * v
    o_ref[...] = (yn + bonus) * gate


def _rwkv(proj_rw, prev, s0, wts, *, batch, seq, chain):
    n = proj_rw.shape[0]
    if chain:
        rows = _pick_tile(seq, RW_ROWS_CHAIN)
        nseq, c, nt = 1, RW_CHUNK, seq // rows
        prev_spec = pl.BlockSpec((1, 1, RW_COLS), lambda b, j: (b, 0, 0))
    else:
        assert seq & (seq - 1) == 0 and RW_TILE % seq == 0
        rows = _pick_tile(batch * seq, RW_ROWS_SHORT)
        nseq, c, nt = rows // seq, seq, 1
        prev_spec = pl.BlockSpec((nseq, RW_COLS), lambda b, j: (b, 0))
    ri = lax.broadcasted_iota(I32, (rows, rows), 0)
    ci = lax.broadcasted_iota(I32, (rows, rows), 1)
    same = (ri // c) == (ci // c)
    trics = jnp.concatenate([same & (ci <= ri), same], axis=0).astype(BF16)
    gi = lax.broadcasted_iota(I32, (128, 128), 0) // RW_N
    gj = lax.broadcasted_iota(I32, (128, 128), 1) // RW_N
    g128 = (gi == gj).astype(BF16)
    esel = (lax.broadcasted_iota(I32, (rows, nseq), 0) // seq
            == lax.broadcasted_iota(I32, (rows, nseq), 1)).astype(BF16)
    (mu, w0, a0, wwa, g2, kkw, kaw, rk, lnw, lnb) = wts

    def full(arr):
        return pl.BlockSpec(arr.shape, lambda b, j: (0,) * arr.ndim)

    consts = [mu, w0, a0, wwa, g2, kkw, kaw, rk, lnw, lnb, g128, trics, esel]
    in_specs = [pl.BlockSpec((rows, RW_COLS), lambda b, j: (b * nt + j, 0)),
                prev_spec,
                pl.BlockSpec((nseq, RW_H, RW_N, RW_N), lambda b, j: (b, 0, 0, 0))]
    in_specs += [full(t) for t in consts]
    args = [proj_rw, prev, s0] + consts
    kern = functools.partial(_rwkv_kernel, c=c, chain=chain, seq=seq, rows=rows)
    return pl.pallas_call(
        kern,
        grid=(batch // nseq, nt),
        in_specs=in_specs,
        out_specs=[pl.BlockSpec((rows, RW_W), lambda b, j: (b * nt + j, 0)),
                   pl.BlockSpec((nseq, RW_H, RW_N, RW_N), lambda b, j: (b, 0, 0, 0))],
        out_shape=[jax.ShapeDtypeStruct((n, RW_W), F32),
                   jax.ShapeDtypeStruct((batch, RW_H, RW_N, RW_N), F32)],
        scratch_shapes=[pltpu.VMEM((RW_PAIRS, 128, 128), F32), pltpu.VMEM((1, RW_COLS), F32)],
        compiler_params=_cparams(("parallel", "arbitrary")),
        name="rwkv7_mix",
    )(*args)


def _store_row_tiles(ref, x):
    m = x.shape[0]
    for s in range(D_MODEL // 128):
        ref[pl.ds(s, m, stride=8), :] = x[:, s * 128:(s + 1) * 128]


def _load_row_tiles(ref, m):
    return jnp.concatenate([ref[pl.ds(s, m, stride=8), :] for s in range(D_MODEL // 128)], axis=1)


def _outproj_router_kernel(ohg_ref, orw_ref, x_ref, wo_ref, gf_ref, wr_ref, br_ref, tri_ref, trit_ref, eye_ref,
                           cin_ref, x1_ref, hf_ref, rk_ref, rkt_ref, gm_ref, tb_ref, cnt_ref, carry_ref):
    i = pl.program_id(0)

    @pl.when(i == 0)
    def _():
        carry_ref[...] = cin_ref[...].astype(F32)

    half = HG_H * HG_DV
    mixed = _dot(ohg_ref[...].astype(BF16), wo_ref[:half, :]) + _dot(orw_ref[...].astype(BF16), wo_ref[half:, :])
    x1 = x_ref[...] + mixed
    x1_ref[...] = x1
    hf = _rms(x1, gf_ref[...])
    hf_ref[...] = hf.astype(BF16)

    h1, h2, _ = _split3(hf)
    w1, w2, _ = _split3(wr_ref[...])
    logits = _dot(h1, w1) + _dot(h1, w2) + _dot(h2, w1) + br_ref[...]

    tm = logits.shape[0]
    lane = lax.broadcasted_iota(I32, (tm, N_EXPERTS), 1).astype(F32)
    vals = logits
    sels, tops = [], []
    for kq in range(TOP_K):
        m = jnp.max(vals, axis=-1, keepdims=True)
        idx = jnp.min(jnp.where(vals == m, lane, float(N_EXPERTS)), axis=-1, keepdims=True)
        sel = lane == idx
        sels.append(sel)
        tops.append(m)
        vals = jnp.where(sel, -jnp.inf, vals)
    es = [jnp.exp(t - tops[0]) for t in tops]
    denom = es[0] + es[1] + es[2] + es[3]
    maskf = jnp.zeros((tm, N_EXPERTS), F32)
    gm = jnp.zeros((tm, N_EXPERTS), F32)
    for kq in range(TOP_K):
        maskf = maskf + jnp.where(sels[kq], 1.0, 0.0)
        gm = gm + jnp.where(sels[kq], es[kq] / denom, 0.0)
    gm_ref[...] = gm

    maskb = maskf.astype(BF16)
    pos = _dot(tri_ref[...], maskb)
    pos_t = _dot_tn(maskb, trit_ref[...])
    mask_t = _dot_tn(maskb, eye_ref[...])
    rk_ref[...] = jnp.where(maskf > 0.5, pos, -1.0).astype(I32)
    rkt_ref[...] = jnp.where(mask_t > 0.5, pos_t, -1.0).astype(I32)
    tb_ref[0] = carry_ref[...].astype(I32)
    carry_ref[...] = carry_ref[...] + jnp.sum(maskf, axis=0, keepdims=True)
    cnt_ref[...] = carry_ref[...].astype(I32)


def _outproj_router(o_hg, o_rw, x, wo_bf16, gf, w_router, b_router, cnt_in, tm):
    n = x.shape[0]
    ri = lax.broadcasted_iota(I32, (tm, tm), 0)
    ci = lax.broadcasted_iota(I32, (tm, tm), 1)
    tri, trit, eye = ((ri > ci).astype(BF16), (ri < ci).astype(BF16), (ri == ci).astype(BF16))
    half = HG_H * HG_DV
    row = lambda i: (i, 0)
    fix = lambda i: (0, 0)
    return pl.pallas_call(
        _outproj_router_kernel,
        grid=(n // tm,),
        in_specs=[pl.BlockSpec((tm, half), row), pl.BlockSpec((tm, RW_W), row), pl.BlockSpec((tm, D_MODEL), row),
                  pl.BlockSpec((half + RW_W, D_MODEL), fix), pl.BlockSpec((1, D_MODEL), fix),
                  pl.BlockSpec((D_MODEL, N_EXPERTS), fix), pl.BlockSpec((1, N_EXPERTS), fix),
                  pl.BlockSpec((tm, tm), fix), pl.BlockSpec((tm, tm), fix), pl.BlockSpec((tm, tm), fix),
                  pl.BlockSpec((1, N_EXPERTS), fix)],
        out_specs=[pl.BlockSpec((tm, D_MODEL), row), pl.BlockSpec((tm, D_MODEL), row),
                   pl.BlockSpec((tm, N_EXPERTS), row), pl.BlockSpec((N_EXPERTS, tm), lambda i: (0, i)),
                   pl.BlockSpec((tm, N_EXPERTS), row), pl.BlockSpec((1, 1, N_EXPERTS), lambda i: (i, 0, 0)),
                   pl.BlockSpec((1, N_EXPERTS), fix)],
        out_shape=[jax.ShapeDtypeStruct((n, D_MODEL), F32), jax.ShapeDtypeStruct((n, D_MODEL), BF16),
                   jax.ShapeDtypeStruct((n, N_EXPERTS), I32), jax.ShapeDtypeStruct((N_EXPERTS, n), I32),
                   jax.ShapeDtypeStruct((n, N_EXPERTS), F32), jax.ShapeDtypeStruct((n // tm, 1, N_EXPERTS), I32),
                   jax.ShapeDtypeStruct((1, N_EXPERTS), I32)],
        scratch_shapes=[pltpu.VMEM((1, N_EXPERTS), F32)],
        compiler_params=_cparams(("arbitrary",)),
        name="out_proj_router",
    )(o_hg, o_rw, x, wo_bf16, gf, w_router, b_router, tri, trit, eye, cnt_in)


def _padded_rows(cnt):
    return (cnt + (MOE_CHUNK + MOE_BLOCK - 1)) // MOE_BLOCK * MOE_BLOCK


def _plan_kernel(cnt_ref, start_ref, bexp_ref, nblk_ref, *, n_blocks):
    cnt = cnt_ref[...].astype(F32)
    padded = jnp.floor((cnt + (MOE_CHUNK + MOE_BLOCK - 1)) * (1.0 / MOE_BLOCK)) * MOE_BLOCK
    ei = lax.broadcasted_iota(I32, (N_EXPERTS, N_EXPERTS), 0)
    ej = lax.broadcasted_iota(I32, (N_EXPERTS, N_EXPERTS), 1)
    upper = (ei < ej).astype(BF16)
    upper_incl = (ei <= ej).astype(BF16)
    start = _dot_sel_r(padded, upper)
    ends = _dot_sel_r(padded, upper_incl)
    start_ref[...] = start.astype(I32)
    bstart = (lax.broadcasted_iota(I32, (n_blocks, 1), 0) * MOE_BLOCK).astype(F32)
    be = jnp.sum(jnp.where(ends <= bstart, 1.0, 0.0), axis=-1, keepdims=True)
    bexp_ref[...] = jnp.minimum(be, N_EXPERTS - 1.0).astype(I32)
    nblk_ref[...] = (ends[:, N_EXPERTS - 1:] * (1.0 / MOE_BLOCK)).astype(I32)


def _plan(cnt, n_blocks):
    return pl.pallas_call(
        functools.partial(_plan_kernel, n_blocks=n_blocks),
        out_shape=[jax.ShapeDtypeStruct((1, N_EXPERTS), I32), jax.ShapeDtypeStruct((n_blocks, 1), I32),
                   jax.ShapeDtypeStruct((1, 1), I32)],
        name="moe_plan",
    )(cnt)


def _row_tile(ref, r):
    return ref.at[pl.ds(pl.multiple_of(r * 8, 8), 8)]


def _row_chunk(ref, r):
    return ref.at[pl.ds(pl.multiple_of(r * 8, 8), 8 * MOE_CHUNK)]


def _dispatch_kernel(start_ref, cnt_ref, nblk_ref, tb_ref, hf_a_ref, hf_b_ref, rk_a_ref, rk_b_ref, xs_ref,
                     buf, zbuf, obuf, sem, sem_row, sem_z, sem_o, *, td, nt_a, n_blocks):
    i = pl.program_id(0)
    ch = MOE_CHUNK

    @pl.when(i == 0)
    def _():
        zbuf[...] = jnp.zeros_like(zbuf)

        def zrow(r, carry):
            pltpu.make_async_copy(_row_tile(zbuf, 0), _row_tile(xs_ref, r), sem_row).start()
            return carry

        def wrow(r, carry):
            pltpu.make_async_copy(_row_tile(zbuf, 0), _row_tile(xs_ref, 0), sem_row).wait()
            return carry

        def zchunk(q, carry):
            pltpu.make_async_copy(zbuf, _row_chunk(xs_ref, q * ch), sem_z).start()
            return carry

        def wchunk(q, carry):
            pltpu.make_async_copy(zbuf, _row_chunk(xs_ref, 0), sem_z).wait()
            return carry

        for e in range(N_EXPERTS):
            c = cnt_ref[e]
            lo = start_ref[e] + c
            hi = start_ref[e] + _padded_rows(c)
            mid = jnp.minimum((lo + (ch - 1)) // ch * ch, hi)
            lax.fori_loop(lo, mid, zrow, 0)
            lax.fori_loop(mid // ch, hi // ch, zchunk, 0)
            lax.fori_loop(lo, mid, wrow, 0)
            lax.fori_loop(mid // ch, hi // ch, wchunk, 0)
        q0 = nblk_ref[0] * (MOE_BLOCK // ch)
        q1 = n_blocks * (MOE_BLOCK // ch)
        lax.fori_loop(q0, q1, zchunk, 0)
        lax.fori_loop(q0, q1, wchunk, 0)

    def wait_chunks():
        for _ in range(N_EXPERTS):
            pltpu.make_async_copy(buf.at[pl.ds(0, 8 * ch)], _row_chunk(xs_ref, 0), sem).wait()

    def tile(hf_ref, rk_ref):
        hf = hf_ref[...]
        rk = rk_ref[...]
        riota = lax.broadcasted_iota(I32, (ch, td), 0)
        sel = jnp.concatenate([jnp.where(riota == rk[e:e + 1, :], 1.0, 0.0) for e in range(N_EXPERTS)],
                              axis=0).astype(BF16)
        out = _dot(sel, hf)

        @pl.when(i > 0)
        def _():
            wait_chunks()

        _store_row_tiles(buf, out)
        for e in range(N_EXPERTS):
            row0 = start_ref[e] + tb_ref[i * N_EXPERTS + e]
            pltpu.make_async_copy(buf.at[pl.ds(e * 8 * ch, 8 * ch)], _row_chunk(xs_ref, row0), sem).start()
        for e in range(N_EXPERTS):
            base = tb_ref[i * N_EXPERTS + e]
            n_e = tb_ref[(i + 1) * N_EXPERTS + e] - base
            row0 = start_ref[e] + base

            def extra(j, carry, e=e, row0=row0):
                sel_j = jnp.where(riota + j * ch == rk[e:e + 1, :], 1.0, 0.0).astype(BF16)
                _store_row_tiles(obuf, _dot(sel_j, hf))
                cp = pltpu.make_async_copy(obuf, _row_chunk(xs_ref, row0 + j * ch), sem_o)
                cp.start()
                cp.wait()
                return carry

            lax.fori_loop(1, (n_e + (ch - 1)) // ch, extra, 0)

        @pl.when(i == pl.num_programs(0) - 1)
        def _():
            wait_chunks()

    @pl.when(i < nt_a)
    def _():
        tile(hf_a_ref, rk_a_ref)

    @pl.when(i >= nt_a)
    def _():
        tile(hf_b_ref, rk_b_ref)


def _dispatch(start, cnt, nblk, tb, hf_a, hf_b, rk_a, rk_b, n_blocks, td):
    nt_a = hf_a.shape[0] // td
    nt_b = hf_b.shape[0] // td
    grid_spec = pltpu.PrefetchScalarGridSpec(
        num_scalar_prefetch=4,
        grid=(nt_a + nt_b,),
        in_specs=[pl.BlockSpec((td, D_MODEL), lambda i, *_: (jnp.minimum(i, nt_a - 1), 0)),
                  pl.BlockSpec((td, D_MODEL), lambda i, *_: (jnp.maximum(i - nt_a, 0), 0)),
                  pl.BlockSpec((N_EXPERTS, td), lambda i, *_: (0, jnp.minimum(i, nt_a - 1))),
                  pl.BlockSpec((N_EXPERTS, td), lambda i, *_: (0, jnp.maximum(i - nt_a, 0)))],
        out_specs=pl.BlockSpec(memory_space=pl.ANY),
        scratch_shapes=[pltpu.VMEM((8 * N_EXPERTS * MOE_CHUNK, 128), F32), pltpu.VMEM((8 * MOE_CHUNK, 128), F32),
                        pltpu.VMEM((8 * MOE_CHUNK, 128), F32), pltpu.SemaphoreType.DMA(()),
                        pltpu.SemaphoreType.DMA(()), pltpu.SemaphoreType.DMA(()), pltpu.SemaphoreType.DMA(())],
    )
    return pl.pallas_call(
        functools.partial(_dispatch_kernel, td=td, nt_a=nt_a, n_blocks=n_blocks),
        grid_spec=grid_spec,
        out_shape=jax.ShapeDtypeStruct((8 * n_blocks * MOE_BLOCK, 128), F32),
        compiler_params=_cparams(("arbitrary",)),
        name="moe_dispatch",
    )(start, cnt, nblk, tb, hf_a, hf_b, rk_a, rk_b)


def _ffn_kernel(bexp_ref, nblk_ref, xs_ref, wgu_ref, bgu_ref, wd_ref, bd_ref, yb_ref, wgu_bf, wd_bf):
    b = pl.program_id(0)
    valid = b < nblk_ref[0]
    prev_e = bexp_ref[jnp.maximum(b, 1) - 1]
    fresh = (b == 0) | (bexp_ref[b] != prev_e)

    @pl.when(valid & fresh)
    def _():
        wgu_bf[...] = wgu_ref[0].astype(BF16)
        wd_bf[...] = wd_ref[0].astype(BF16)

    @pl.when(valid)
    def _():
        x = _load_row_tiles(xs_ref, MOE_BLOCK).astype(BF16)
        gu = _dot(x, wgu_bf[...]) + bgu_ref[0]
        gate = jnp.minimum(gu[:, :D_FF], SWIGLU_LIMIT)
        up = jnp.clip(gu[:, D_FF:], -SWIGLU_LIMIT, SWIGLU_LIMIT)
        glu = gate * _sigmoid(SWIGLU_ALPHA * gate)
        act = ((up + 1.0) * glu).astype(BF16)
        _store_row_tiles(yb_ref, _dot(act, wd_bf[...]) + bd_ref[0])

    @pl.when(jnp.logical_not(valid))
    def _():
        yb_ref[...] = jnp.zeros_like(yb_ref)


def _ffn(bexp, nblk, xs, w_gate_up, b_gate_up, w_down, b_down):
    n_rows = xs.shape[0] // 8
    n_blocks = n_rows // MOE_BLOCK

    def blk(b, be, nb):
        return (jnp.minimum(b, jnp.maximum(nb[0], 1) - 1), 0)

    def wmap(b, be, nb):
        return (be[jnp.minimum(b, jnp.maximum(nb[0], 1) - 1)], 0, 0)

    grid_spec = pltpu.PrefetchScalarGridSpec(
        num_scalar_prefetch=2,
        grid=(n_blocks,),
        in_specs=[pl.BlockSpec((8 * MOE_BLOCK, 128), blk),
                  pl.BlockSpec((1, D_MODEL, 2 * D_FF), wmap),
                  pl.BlockSpec((1, 1, 2 * D_FF), wmap),
                  pl.BlockSpec((1, D_FF, D_MODEL), wmap),
                  pl.BlockSpec((1, 1, D_MODEL), wmap)],
        out_specs=pl.BlockSpec((8 * MOE_BLOCK, 128), lambda b, be, nb: (b, 0)),
        scratch_shapes=[pltpu.VMEM((D_MODEL, 2 * D_FF), BF16), pltpu.VMEM((D_FF, D_MODEL), BF16)],
    )
    return pl.pallas_call(
        _ffn_kernel,
        grid_spec=grid_spec,
        out_shape=jax.ShapeDtypeStruct((8 * n_rows, 128), F32),
        compiler_params=_cparams(("arbitrary",)),
        name="moe_ffn",
    )(bexp, nblk, xs, w_gate_up, b_gate_up, w_down, b_down)


def _combine_kernel(start_ref, tb_ref, rk_ref, gm_ref, x1_ref, gn_ref, yb_ref, y_ref, buf, obuf, acc_ref, sem, sem_o,
                    *, td, tile0):
    i = pl.program_id(0) + tile0
    ch = MOE_CHUNK
    for e in range(N_EXPERTS):
        row0 = start_ref[e] + tb_ref[i * N_EXPERTS + e]
        pltpu.make_async_copy(_row_chunk(yb_ref, row0), buf.at[pl.ds(e * 8 * ch, 8 * ch)], sem).start()
    rk = rk_ref[...]
    gm = gm_ref[...]
    lane = lax.broadcasted_iota(I32, (td, 128), 1)
    lo_half = lane < ch
    g_parts = []
    for p in range(N_EXPERTS * ch // 128):
        ra, rb = rk[:, 2 * p:2 * p + 1], rk[:, 2 * p + 1:2 * p + 2]
        ta = jnp.where(ra < ch, ra, -1)
        tb_ = jnp.where((rb >= 0) & (rb < ch), rb + ch, -1)
        target = jnp.where(lo_half, ta, tb_)
        gate = jnp.where(lo_half, gm[:, 2 * p:2 * p + 1], gm[:, 2 * p + 1:2 * p + 2])
        g_parts.append(jnp.where(lane == target, gate, 0.0))
    g = jnp.concatenate(g_parts, axis=1)
    g_hi = g.astype(BF16)
    g_lo = (g - g_hi.astype(F32)).astype(BF16)
    for e in range(N_EXPERTS):
        pltpu.make_async_copy(_row_chunk(yb_ref, 0), buf.at[pl.ds(0, 8 * ch)], sem).wait()
    rows = _load_row_tiles(buf, N_EXPERTS * ch).astype(BF16)
    acc_ref[...] = x1_ref[...] + _dot(g_hi, rows) + _dot(g_lo, rows)

    lane_c = lax.broadcasted_iota(I32, (td, ch), 1)
    for e in range(N_EXPERTS):
        base = tb_ref[i * N_EXPERTS + e]
        n_e = tb_ref[(i + 1) * N_EXPERTS + e] - base
        row0 = start_ref[e] + base

        def extra(j, carry, e=e, row0=row0):
            cp = pltpu.make_async_copy(_row_chunk(yb_ref, row0 + j * ch), obuf, sem_o)
            cp.start()
            cp.wait()
            ge = jnp.where(lane_c + j * ch == rk[:, e:e + 1], gm[:, e:e + 1], 0.0)
            ge_hi = ge.astype(BF16)
            ge_lo = (ge - ge_hi.astype(F32)).astype(BF16)
            rows_j = _load_row_tiles(obuf, ch).astype(BF16)
            acc_ref[...] += _dot(ge_hi, rows_j) + _dot(ge_lo, rows_j)
            return carry

        lax.fori_loop(1, (n_e + (ch - 1)) // ch, extra, 0)
    y_ref[...] = _rms(acc_ref[...], gn_ref[...])


def _combine(start, tb, rk, gm, x1, gn, yb, td, tile0):
    n = x1.shape[0]
    row = lambda i, *_: (i, 0)
    grid_spec = pltpu.PrefetchScalarGridSpec(
        num_scalar_prefetch=2,
        grid=(n // td,),
        in_specs=[pl.BlockSpec((td, N_EXPERTS), row), pl.BlockSpec((td, N_EXPERTS), row),
                  pl.BlockSpec((td, D_MODEL), row), pl.BlockSpec((1, D_MODEL), lambda i, *_: (0, 0)),
                  pl.BlockSpec(memory_space=pl.ANY)],
        out_specs=pl.BlockSpec((td, D_MODEL), row),
        scratch_shapes=[pltpu.VMEM((8 * N_EXPERTS * MOE_CHUNK, 128), F32), pltpu.VMEM((8 * MOE_CHUNK, 128), F32),
                        pltpu.VMEM((td, D_MODEL), F32), pltpu.SemaphoreType.DMA(()), pltpu.SemaphoreType.DMA(())],
    )
    return pl.pallas_call(
        functools.partial(_combine_kernel, td=td, tile0=tile0),
        grid_spec=grid_spec,
        out_shape=jax.ShapeDtypeStruct((n, D_MODEL), F32),
        compiler_params=_cparams(("arbitrary",)),
        name="moe_combine",
    )(start, tb, rk, gm, x1, gn, yb)


def _pick_tile(n, pref):
    t = pref
    while n % t:
        t //= 2
    return t


def kernel(x_prompt, x_sample, state_hgrn, state_rwkv, state_shift, norm_mix, w_in, lb_logits, hg_norm_w, rw_mu,
           rw_w0, rw_w2, rw_a0, rw_a2, rw_g2, rw_k_k, rw_k_a, rw_r_k, rw_lnx_w, rw_lnx_b, w_out, norm_ffn,
           w_router, b_router, w_gate_up, b_gate_up, w_down, b_down, norm_final):
    bp, tp, d = x_prompt.shape
    bs, tsq, _ = x_sample.shape
    assert norm_mix.shape[0] == 1 and d == D_MODEL
    n_p, n_s = bp * tp, bs * tsq
    n = n_p + n_s
    xp = x_prompt.reshape(n_p, d)
    xs_ = x_sample.reshape(n_s, d)

    w_in_b = w_in[0].astype(BF16)
    splits = (HG_COLS, RW_COLS)
    hgp_p, rwp_p = _proj(xp, norm_mix, w_in_b, splits, True, _pick_tile(n_p, 256))
    hgp_s, rwp_s = _proj(xs_, norm_mix, w_in_b, splits, True, _pick_tile(n_s, 256))
    ones_d = jnp.ones((1, d), F32)
    (prev_s,) = _proj(state_shift[0], ones_d, w_in_b[:, HG_COLS:], (RW_COLS,), False, _pick_tile(bs, 128))
    prev_p = jnp.zeros((bp, 1, RW_COLS), F32)
    x_last = jnp.concatenate([x_prompt[:, -1, :], x_sample[:, -1, :]], axis=0)
    shift = _rmsnorm_rows(x_last, norm_mix)

    zero_hg = jnp.zeros((bp, HG_H, HG_DK, HG_DV), F32)
    ohg_p, hg_p = _hgrn(hgp_p, lb_logits, hg_norm_w, zero_hg, batch=bp, seq=tp,
                        nseq=1, tb=_pick_tile(tp, 256), c=HG_CHUNK, sub=HG_SUB)
    ohg_s, hg_s = _hgrn(hgp_s, lb_logits, hg_norm_w, state_hgrn[0], batch=bs, seq=tsq,
                        nseq=_pick_tile(bs, 16), tb=tsq, c=tsq, sub=tsq)

    zpad = jnp.zeros((64, RW_W), F32)
    wwa = jnp.concatenate([jnp.concatenate([rw_w2[0], zpad], axis=1),
                           jnp.concatenate([zpad, rw_a2[0]], axis=1)], axis=0).astype(BF16)
    wts = (rw_mu, rw_w0, rw_a0, wwa, rw_g2[0].astype(BF16), rw_k_k, rw_k_a, rw_r_k.reshape(1, RW_W),
           rw_lnx_w, rw_lnx_b)
    zero_rw = jnp.zeros((bp, RW_H, RW_N, RW_N), F32)
    orw_p, rw_p = _rwkv(rwp_p, prev_p, zero_rw, wts, batch=bp, seq=tp, chain=True)
    orw_s, rw_s = _rwkv(rwp_s, prev_s, state_rwkv[0], wts, batch=bs, seq=tsq, chain=False)

    wo_b = w_out[0].astype(BF16)
    td = _pick_tile(math.gcd(n_p, n_s), MOE_TOKENS)
    cnt0 = jnp.zeros((1, N_EXPERTS), I32)
    x1_p, hf_p, rk_p, rkt_p, gm_p, tb_p, cnt_p = _outproj_router(ohg_p, orw_p, xp, wo_b, norm_ffn, w_router[0],
                                                                 b_router, cnt0, td)
    x1_s, hf_s, rk_s, rkt_s, gm_s, tb_s, cnt = _outproj_router(ohg_s, orw_s, xs_, wo_b, norm_ffn, w_router[0],
                                                               b_router, cnt_p, td)

    assert MOE_BLOCK % MOE_CHUNK == 0 and 2 * MOE_CHUNK == 128
    n_blocks = (n * TOP_K + N_EXPERTS * (MOE_BLOCK + MOE_CHUNK - 1)) // MOE_BLOCK + 1
    start, bexp, nblk = _plan(cnt, n_blocks)
    start = start.reshape(N_EXPERTS)
    nblk = nblk.reshape(1)
    tb = jnp.concatenate([tb_p.reshape(-1), tb_s.reshape(-1), cnt.reshape(-1)])
    xs = _dispatch(start, cnt.reshape(N_EXPERTS), nblk, tb, hf_p, hf_s, rkt_p, rkt_s, n_blocks, td)
    yb = _ffn(bexp.reshape(n_blocks), nblk, xs, w_gate_up[0], b_gate_up[0].reshape(N_EXPERTS, 1, 2 * D_FF),
              w_down[0], b_down[0].reshape(N_EXPERTS, 1, d))
    gn = norm_final.reshape(1, d)
    y_p = _combine(start, tb, rk_p, gm_p, x1_p, gn, yb, td, 0)
    y_s = _combine(start, tb, rk_s, gm_s, x1_s, gn, yb, td, n_p // td)

    return (y_p.reshape(bp, tp, d), y_s.reshape(bs, tsq, d), hg_p[None], rw_p[None], shift[:bp][None],
            hg_s[None], rw_s[None], shift[bp:][None])
```

```python
import functools
import math

import jax
import jax.numpy as jnp
from jax import lax
from jax.experimental import pallas as pl
from jax.experimental.pallas import tpu as pltpu

F32 = jnp.float32
BF16 = jnp.bfloat16
I32 = jnp.int32

D_MODEL = 1024
HG_H, HG_DK, HG_DV = 4, 128, 128
HG_CHUNK = 64
HG_SUB = 8
RW_H, RW_N = 8, 64
RW_W = RW_H * RW_N
RW_PAIRS = RW_H // 2
RW_TILE = 64
RW_CHUNK = 32
RW_ROWS_CHAIN = 256
RW_ROWS_SHORT = 128
HG_COLS = 2 * HG_H * HG_DK + 2 * HG_H * HG_DV
RW_COLS = 3 * RW_W + 64 + 64 + 128
N_EXPERTS = 32
TOP_K = 4
D_FF = D_MODEL
SWIGLU_LIMIT = 7.0
SWIGLU_ALPHA = 1.702
NORM_EPS = 1e-6
RW_GN_EPS = 64e-5
MOE_BLOCK = 256
VMEM_LIMIT = 56 * 1024 * 1024


def _dot(a, b):
    return jnp.dot(a, b, preferred_element_type=F32)


def _dot_nt(a, b):
    return lax.dot_general(a, b, (((1,), (1,)), ((), ())), preferred_element_type=F32)


def _dot_tn(a, b):
    return lax.dot_general(a, b, (((0,), (0,)), ((), ())), preferred_element_type=F32)


def _split3(x):
    x1 = x.astype(BF16)
    r1 = x - x1.astype(F32)
    x2 = r1.astype(BF16)
    x3 = (r1 - x2.astype(F32)).astype(BF16)
    return x1, x2, x3


def _dot_sel_l(m_bf16, x):
    x1, x2, x3 = _split3(x)
    return _dot(m_bf16, x1) + _dot(m_bf16, x2) + _dot(m_bf16, x3)


def _dot_sel_r(x, m_bf16):
    x1, x2, x3 = _split3(x)
    return _dot(x1, m_bf16) + _dot(x2, m_bf16) + _dot(x3, m_bf16)


def _sigmoid(x):
    return 1.0 / (1.0 + jnp.exp(-x))


def _rms(x, g):
    return x * lax.rsqrt(jnp.mean(x * x, axis=-1, keepdims=True) + NORM_EPS) * g


def _cparams(sem):
    return pltpu.CompilerParams(dimension_semantics=sem, vmem_limit_bytes=VMEM_LIMIT)


def _proj_kernel(x_ref, g_ref, w_ref, *o_refs, normalize, splits):
    x = x_ref[...]
    h = _rms(x, g_ref[...]) if normalize else x
    hb = h.astype(BF16)
    c0 = 0
    for o_ref, width in zip(o_refs, splits):
        o_ref[...] = _dot(hb, w_ref[:, c0:c0 + width])
        c0 += width


def _proj(x, g, w_bf16, splits, normalize, tm):
    n = x.shape[0]
    kern = functools.partial(_proj_kernel, normalize=normalize, splits=splits)
    return pl.pallas_call(
        kern,
        grid=(n // tm,),
        in_specs=[pl.BlockSpec((tm, D_MODEL), lambda i: (i, 0)),
                  pl.BlockSpec((1, D_MODEL), lambda i: (0, 0)),
                  pl.BlockSpec((D_MODEL, sum(splits)), lambda i: (0, 0))],
        out_specs=[pl.BlockSpec((tm, s), lambda i: (i, 0)) for s in splits],
        out_shape=[jax.ShapeDtypeStruct((n, s), F32) for s in splits],
        compiler_params=_cparams(("parallel",)),
        name="norm_in_proj" if normalize else "shift_proj",
    )(x, g, w_bf16)


def _rmsnorm_rows_kernel(x_ref, g_ref, o_ref):
    o_ref[...] = _rms(x_ref[...], g_ref[...])


def _rmsnorm_rows(x, g):
    return pl.pallas_call(
        _rmsnorm_rows_kernel,
        out_shape=jax.ShapeDtypeStruct(x.shape, F32),
        name="shift_norm",
    )(x, g)


def _hgrn_masks(c, sub):
    ri = lax.broadcasted_iota(I32, (c, c), 0)
    ci = lax.broadcasted_iota(I32, (c, c), 1)
    masks = []
    h = sub
    while h < c:
        keep = ((ri // (2 * h)) == (ci // (2 * h))) & ((ri % (2 * h)) >= h) & ((ci % (2 * h)) < h)
        masks.append(keep.astype(F32))
        h *= 2
    return masks


def _hgrn_chunk(qs, fps, ivs, lbs, sts, tri, masks, c, sub):
    nh = len(qs)
    hs = range(nh)
    qf = [q * _sigmoid(q) for q in qs]
    f = [lb + (1.0 - lb) * _sigmoid(fp) for lb, fp in zip(lbs, fps)]
    lf = [jnp.log(z) for z in f]
    kc = [1.0 - z for z in f]
    g = [_dot_sel_l(tri, z) for z in lf]
    o_inter = [_dot_nt((qf[h] * jnp.exp(g[h])).astype(BF16), sts[h].astype(BF16)) for h in hs]
    ivb = [z.astype(BF16) for z in ivs]
    lane = lax.broadcasted_iota(I32, (sub, c), 1)
    row = lax.broadcasted_iota(I32, (sub, c), 0)
    a_rows = [[] for _ in hs]
    for i in range(c // sub):
        lo = i * sub
        blk = slice(lo, lo + sub)
        a = [jnp.zeros((sub, c), F32) for _ in hs]
        for s in range(sub):
            for h in hs:
                gi = g[h][blk]
                e = jnp.exp(jnp.minimum(gi - gi[s:s + 1, :], 0.0))
                col = jnp.sum(qf[h][blk] * (kc[h][lo + s:lo + s + 1, :] * e), axis=-1, keepdims=True)
                a[h] = jnp.where(lane == lo + s, col, a[h])
        for h in hs:
            a_rows[h].append(jnp.where(row + lo >= lane, a[h], 0.0))
    a_tot = [z[0] if len(z) == 1 else jnp.concatenate(z, axis=0) for z in a_rows]
    rowc = lax.broadcasted_iota(I32, (c, 1), 0)
    half = sub
    for mask in masks:
        gref = []
        for h in hs:
            ref = g[h][half - 1:half, :]
            for b in range(1, c // (2 * half)):
                r0 = b * 2 * half + half - 1
                ref = jnp.where(rowc >= b * 2 * half, g[h][r0:r0 + 1, :], ref)
            gref.append(ref)
        ql = [(qf[h] * jnp.exp(jnp.minimum(g[h] - gref[h], 0.0))).astype(BF16) for h in hs]
        kl = [(kc[h] * jnp.exp(jnp.minimum(gref[h] - g[h], 0.0))).astype(BF16) for h in hs]
        a_tot = [a_tot[h] + _dot_nt(ql[h], kl[h]) * mask for h in hs]
        half *= 2
    o = [o_inter[h] + _dot(a_tot[h].astype(BF16), ivb[h]) for h in hs]
    gl = [g[h][c - 1:c, :] for h in hs]
    kd = [(kc[h] * jnp.exp(gl[h] - g[h])).astype(BF16) for h in hs]
    st_new = [sts[h] * jnp.exp(gl[h]) + _dot_tn(ivb[h], kd[h]) for h in hs]
    return o, st_new


def _hgrn_kernel(q_ref, f_ref, i_ref, g_ref, lbl_ref, gw_ref, tri_ref, s0_ref, o_ref, so_ref, st_ref,
                 *, nseq, tb, c, sub):
    j = pl.program_id(1)
    nt = pl.num_programs(1)

    @pl.when(j == 0)
    def _():
        for s in range(nseq):
            for h in range(HG_H):
                st_ref[s * HG_H + h] = s0_ref[s, h].T

    lbl = lbl_ref[...]
    ex = jnp.exp(lbl - jnp.max(lbl, axis=0, keepdims=True))
    lb = ex[0:1, :] / jnp.sum(ex, axis=0, keepdims=True)
    gw = gw_ref[...]
    tri = tri_ref[...]
    masks = _hgrn_masks(c, sub)
    n_chunks = tb // c

    def body(it, carry):
        s = it // n_chunks
        r0 = pl.multiple_of(it * c, c)
        rows = pl.ds(r0, c)
        hsl = [slice(h * HG_DK, (h + 1) * HG_DK) for h in range(HG_H)]
        o, st_new = _hgrn_chunk([q_ref[rows, z] for z in hsl], [f_ref[rows, z] for z in hsl],
                                [i_ref[rows, z] for z in hsl], [lb[:, z] for z in hsl],
                                [st_ref[s * HG_H + h] for h in range(HG_H)], tri, masks, c, sub)
        for h in range(HG_H):
            st_ref[s * HG_H + h] = st_new[h]
            on = o[h] * lax.rsqrt(jnp.mean(o[h] * o[h], axis=-1, keepdims=True) + NORM_EPS) * gw
            gin = g_ref[rows, hsl[h]]
            o_ref[rows, hsl[h]] = on * (gin * _sigmoid(gin))
        return carry

    lax.fori_loop(0, nseq * n_chunks, body, 0)

    @pl.when(j == nt - 1)
    def _():
        for s in range(nseq):
            for h in range(HG_H):
                so_ref[s, h] = st_ref[s * HG_H + h].T


def _hgrn(proj_hg, lb_logits, gn_w, s0, *, batch, seq, nseq, tb, c, sub):
    n = proj_hg.shape[0]
    rows = nseq * tb
    nt = seq // tb
    hw = HG_H * HG_DK
    tri = (lax.broadcasted_iota(I32, (c, c), 0) >= lax.broadcasted_iota(I32, (c, c), 1)).astype(BF16)

    def rmap(col):
        return lambda b, j: (b * nt + j, col)

    kern = functools.partial(_hgrn_kernel, nseq=nseq, tb=tb, c=c, sub=sub)
    in_specs = [pl.BlockSpec((rows, hw), rmap(0)), pl.BlockSpec((rows, hw), rmap(1)),
                pl.BlockSpec((rows, hw), rmap(2)), pl.BlockSpec((rows, hw), rmap(3)),
                pl.BlockSpec((lb_logits.shape[0], hw), lambda b, j: (0, 0)),
                pl.BlockSpec((1, HG_DV), lambda b, j: (0, 0)),
                pl.BlockSpec((c, c), lambda b, j: (0, 0)),
                pl.BlockSpec((nseq, HG_H, HG_DK, HG_DV), lambda b, j: (b, 0, 0, 0))]
    args = [proj_hg, proj_hg, proj_hg, proj_hg, lb_logits, gn_w, tri, s0]
    return pl.pallas_call(
        kern,
        grid=(batch // nseq, nt),
        in_specs=in_specs,
        out_specs=[pl.BlockSpec((rows, hw), lambda b, j: (b * nt + j, 0)),
                   pl.BlockSpec((nseq, HG_H, HG_DK, HG_DV), lambda b, j: (b, 0, 0, 0))],
        out_shape=[jax.ShapeDtypeStruct((n, HG_H * HG_DV), F32),
                   jax.ShapeDtypeStruct((batch, HG_H, HG_DK, HG_DV), F32)],
        scratch_shapes=[pltpu.VMEM((nseq * HG_H, HG_DV, HG_DK), F32)],
        compiler_params=_cparams(("parallel", "arbitrary")),
        name="hgrn2_mix",
    )(*args)


def _softplus(z):
    return jnp.maximum(z, 0.0) + jnp.log(1.0 + jnp.exp(-jnp.abs(z)))


def _rwkv_kernel(p_ref, prev_ref, s0_ref, mu_ref, w0_ref, a0_ref, wwa_ref, g2_ref, kkw_ref, kaw_ref, rk_ref,
                 lnw_ref, lnb_ref, g128_ref, trics_ref, esel_ref, o_ref, so_ref, st_ref, carry_ref,
                 *, c, chain, seq, rows):
    tt = RW_TILE
    ntile = rows // tt
    nc = tt // c
    j = pl.program_id(1)
    nt = pl.num_programs(1)
    lane128 = lax.broadcasted_iota(I32, (1, 128), 1)
    lo_half = lane128 < RW_N

    def block_diag(se, so):
        z = jnp.zeros((RW_N, RW_N), F32)
        return jnp.concatenate([jnp.concatenate([se, z], axis=1), jnp.concatenate([z, so], axis=1)], axis=0)

    def split2(z):
        hi = z.astype(BF16)
        return hi, (z - hi.astype(F32)).astype(BF16)

    p = p_ref[...]
    rolled = pltpu.roll(p, 1, 0)
    rowi = lax.broadcasted_iota(I32, (rows, 1), 0)
    if chain:
        @pl.when(j == 0)
        def _():
            carry_ref[...] = prev_ref[0]
            for pr in range(RW_PAIRS):
                st_ref[pr] = block_diag(s0_ref[0, 2 * pr], s0_ref[0, 2 * pr + 1])

        ps = jnp.where(rowi == 0, carry_ref[...], rolled)
        carry_ref[...] = p[rows - 1:rows, :]
    else:
        first = _dot_sel_l(esel_ref[...], prev_ref[...])
        ps = jnp.where((rowi & (seq - 1)) == 0, first, rolled)
    pm = p + (ps - p) * mu_ref[...]

    r = pm[:, 0:RW_W]
    k = pm[:, RW_W:2 * RW_W]
    v = pm[:, 2 * RW_W:3 * RW_W]
    wa = pm[:, 3 * RW_W:3 * RW_W + 128]
    g_lo = pm[:, 3 * RW_W + 128:3 * RW_W + 256]
    wa = jnp.where(lo_half, jnp.tanh(wa), wa)
    xwa = _dot(wa.astype(BF16), wwa_ref[...])
    log_w = -_softplus(-(w0_ref[...] + xwa[:, :RW_W])) - 0.5
    ld = -jnp.exp(log_w)
    a = _sigmoid(a0_ref[...] + xwa[:, RW_W:])
    gate = _dot(_sigmoid(g_lo).astype(BF16), g2_ref[...])
    g128 = g128_ref[...]

    def gsum(z):
        zs = jnp.concatenate([z[:, i * 128:(i + 1) * 128] for i in range(RW_PAIRS)], axis=0)
        hi, lo = split2(zs)
        s = _dot(hi, g128) + _dot(lo, g128)
        return jnp.concatenate([s[i * rows:(i + 1) * rows] for i in range(RW_PAIRS)], axis=1)

    kkv = k * kkw_ref[...]
    kk = kkv / jnp.maximum(jnp.sqrt(gsum(kkv * kkv)), 1e-12)
    k2 = k * (1.0 + (a - 1.0) * kaw_ref[...])
    beta = kk * a

    ld_hi, ld_lo = split2(ld)
    cs = _dot(trics_ref[...], ld_hi) + _dot(trics_ref[...], ld_lo)
    gc = cs[:rows]
    gtot = cs[rows:]
    e_g = jnp.exp(gc)
    e_ng = jnp.exp(-gc)
    e_l = jnp.exp(gtot - gc)
    ah = -kk * jnp.exp(gc - ld)
    rh = r * e_g
    bh = beta * e_ng
    kh = k2 * e_ng
    bt = beta * e_l
    kt = k2 * e_l
    e_tot = jnp.exp(gtot)

    ri = lax.broadcasted_iota(I32, (2 * tt, 2 * tt), 0)
    ci = lax.broadcasted_iota(I32, (2 * tt, 2 * tt), 1)
    same = (ri // c) == (ci // c)
    m_strict = jnp.where(same & (ci < ri), 1.0, 0.0)
    m_incl = jnp.where(same & (ci <= ri), 1.0, 0.0)
    eye2 = jnp.where(ri == ci, 1.0, 0.0)
    bd = jnp.where((ri // RW_N) == (ci // RW_N), 1.0, 0.0)

    def stack(z):
        return jnp.concatenate([jnp.where(lo_half, z, 0.0), jnp.where(lo_half, 0.0, z)], axis=0)

    def fold(z):
        return z[:tt] + z[tt:]

    rowc = lax.broadcasted_iota(I32, (tt, 1), 0) // c

    def chunk_expand(z):
        return jnp.concatenate([jnp.where(rowc == ch, z, 0.0) for ch in range(nc)], axis=1).astype(BF16)

    units = [(t, pr) for t in range(ntile) for pr in range(RW_PAIRS)]
    hw = 2 * tt

    def usl(z, u):
        return z[u[0] * tt:(u[0] + 1) * tt, u[1] * 128:(u[1] + 1) * 128]

    def bf(zs):
        return [z.astype(BF16) for z in zs]

    def rows2(a, b):
        return jnp.concatenate([a, b], axis=0)

    xr = [stack(usl(rh, u)) for u in units]
    xab, xrb, xbb, xkb, xvb = (bf([stack(usl(z, u)) for u in units]) for z in (ah, rh, bh, kh, v))
    lhs_ar = [rows2(a_, r_) for a_, r_ in zip(xab, xrb)]
    ab = [_dot_nt(l_, b_) for l_, b_ in zip(lhs_ar, xbb)]
    ak = [_dot_nt(l_, k_) for l_, k_ in zip(lhs_ar, xkb)]
    a_ab = [z[:hw] * m_strict for z in ab]
    a_rb = bf([z[hw:] * m_incl for z in ab])
    a_ak = bf([z[:hw] * m_strict for z in ak])
    a_rk = bf([z[hw:] * m_incl for z in ak])
    tm = [eye2 + z for z in a_ab]
    if c > 2:
        pw = [_dot(z, z) for z in bf(a_ab)]
    n = 2
    while n < c:
        pwb = bf(pw)
        if 2 * n < c:
            both = [_dot(rows2(t_.astype(BF16), p_), p_) for t_, p_ in zip(tm, pwb)]
            tm = [t_ + z[:hw] for t_, z in zip(tm, both)]
            pw = [z[hw:] for z in both]
        else:
            tm = [t_ + _dot(t_.astype(BF16), p_) for t_, p_ in zip(tm, pwb)]
        n *= 2
    tmb = bf(tm)
    akv = [_dot(rows2(p_, q_), x_) for p_, q_, x_ in zip(a_ak, a_rk, xvb)]
    tav = [_dot(t_, jnp.concatenate([x_, z[:hw].astype(BF16)], axis=1))
           for t_, x_, z in zip(tmb, xab, akv)]
    rbt = [_dot(r_, z) for r_, z in zip(a_rb, bf(tav))]
    rpb = bf([fold(x_ + z[:, :128]) for x_, z in zip(xr, rbt)])
    y0 = [fold(z[:, 128:] + w[hw:]) for z, w in zip(rbt, akv)]
    ap = bf([fold(z[:, :128]) for z in tav])
    vp = bf([fold(z[:, 128:]) for z in tav])
    bexp = [chunk_expand(usl(bt, u)) for u in units]
    kexp = [chunk_expand(usl(kt, u)) for u in units]
    vpl = bf([usl(v, u) for u in units])
    mt_all = [_dot_tn(a_, b_) for a_, b_ in zip(ap, bexp)]
    ht_all = [_dot_tn(rows2(p_, q_), rows2(b_, k_)) for p_, q_, b_, k_ in zip(vp, vpl, bexp, kexp)]

    sts = [st_ref[pr] for pr in range(RW_PAIRS)] if chain else None
    y_parts = [[] for _ in units]
    for t in range(ntile):
        for ch in range(nc):
            rs = slice(ch * c, (ch + 1) * c)
            cl = slice(ch * 128, (ch + 1) * 128)
            r0 = t * tt + ch * c
            for pr in range(RW_PAIRS):
                u = t * RW_PAIRS + pr
                mt = mt_all[u][:, cl] * bd + eye2 * e_tot[r0:r0 + 1, pr * 128:(pr + 1) * 128]
                ht = ht_all[u][:, cl] * bd
                if chain:
                    st = sts[pr]
                else:
                    sq = t * nc + ch
                    st = block_diag(s0_ref[sq, 2 * pr], s0_ref[sq, 2 * pr + 1])
                stb = st.astype(BF16)
                y_parts[u].append(_dot_nt(rpb[u][rs], stb) + y0[u][rs])
                st = _dot(stb, mt.astype(BF16)) + ht
                if chain:
                    sts[pr] = st
                else:
                    so_ref[sq, 2 * pr] = st[:RW_N, :RW_N]
                    so_ref[sq, 2 * pr + 1] = st[RW_N:, RW_N:]
    y_tiles = [jnp.concatenate([jnp.concatenate(y_parts[t * RW_PAIRS + pr], axis=0) for pr in range(RW_PAIRS)],
                               axis=1) for t in range(ntile)]
    y = y_tiles[0] if ntile == 1 else jnp.concatenate(y_tiles, axis=0)

    if chain:
        for pr in range(RW_PAIRS):
            st_ref[pr] = sts[pr]

        @pl.when(j == nt - 1)
        def _():
            for pr in range(RW_PAIRS):
                st = st_ref[pr]
                so_ref[0, 2 * pr] = st[:RW_N, :RW_N]
                so_ref[0, 2 * pr + 1] = st[RW_N:, RW_N:]

    inv_n = 1.0 / RW_N
    mean = gsum(y) * inv_n
    yc = y - mean
    var = gsum(yc * yc) * inv_n
    yn = yc * lax.rsqrt(var + RW_GN_EPS) * lnw_ref[...] + lnb_ref[...]
    bonus = gsum(r * k2 * rk_ref[...]) * v
    o_ref[...] = (yn + bonus) * gate


def _rwkv(proj_rw, prev, s0, wts, *, batch, seq, chain):
    n = proj_rw.shape[0]
    if chain:
        rows = _pick_tile(seq, RW_ROWS_CHAIN)
        nseq, c, nt = 1, RW_CHUNK, seq // rows
        prev_spec = pl.BlockSpec((1, 1, RW_COLS), lambda b, j: (b, 0, 0))
    else:
        assert seq & (seq - 1) == 0 and RW_TILE % seq == 0
        rows = _pick_tile(batch * seq, RW_ROWS_SHORT)
        nseq, c, nt = rows // seq, seq, 1
        prev_spec = pl.BlockSpec((nseq, RW_COLS), lambda b, j: (b, 0))
    ri = lax.broadcasted_iota(I32, (rows, rows), 0)
    ci = lax.broadcasted_iota(I32, (rows, rows), 1)
    same = (ri // c) == (ci // c)
    trics = jnp.concatenate([same & (ci <= ri), same], axis=0).astype(BF16)
    gi = lax.broadcasted_iota(I32, (128, 128), 0) // RW_N
    gj = lax.broadcasted_iota(I32, (128, 128), 1) // RW_N
    g128 = (gi == gj).astype(BF16)
    esel = (lax.broadcasted_iota(I32, (rows, nseq), 0) // seq
            == lax.broadcasted_iota(I32, (rows, nseq), 1)).astype(BF16)
    (mu, w0, a0, wwa, g2, kkw, kaw, rk, lnw, lnb) = wts

    def full(arr):
        return pl.BlockSpec(arr.shape, lambda b, j: (0,) * arr.ndim)

    consts = [mu, w0, a0, wwa, g2, kkw, kaw, rk, lnw, lnb, g128, trics, esel]
    in_specs = [pl.BlockSpec((rows, RW_COLS), lambda b, j: (b * nt + j, 0)),
                prev_spec,
                pl.BlockSpec((nseq, RW_H, RW_N, RW_N), lambda b, j: (b, 0, 0, 0))]
    in_specs += [full(t) for t in consts]
    args = [proj_rw, prev, s0] + consts
    kern = functools.partial(_rwkv_kernel, c=c, chain=chain, seq=seq, rows=rows)
    return pl.pallas_call(
        kern,
        grid=(batch // nseq, nt),
        in_specs=in_specs,
        out_specs=[pl.BlockSpec((rows, RW_W), lambda b, j: (b * nt + j, 0)),
                   pl.BlockSpec((nseq, RW_H, RW_N, RW_N), lambda b, j: (b, 0, 0, 0))],
        out_shape=[jax.ShapeDtypeStruct((n, RW_W), F32),
                   jax.ShapeDtypeStruct((batch, RW_H, RW_N, RW_N), F32)],
        scratch_shapes=[pltpu.VMEM((RW_PAIRS, 128, 128), F32), pltpu.VMEM((1, RW_COLS), F32)],
        compiler_params=_cparams(("parallel", "arbitrary")),
        name="rwkv7_mix",
    )(*args)


def _store_row_tiles(ref, x):
    m = x.shape[0]
    for s in range(D_MODEL // 128):
        ref[pl.ds(s, m, stride=8), :] = x[:, s * 128:(s + 1) * 128]


def _load_row_tiles(ref, m):
    return jnp.concatenate([ref[pl.ds(s, m, stride=8), :] for s in range(D_MODEL // 128)], axis=1)


def _outproj_router_kernel(ohg_ref, orw_ref, x_ref, wo_ref, gf_ref, wr_ref, br_ref, tri_ref, cin_ref,
                           x1_ref, hf_ref, idx_ref, gate_ref, pos_ref, cnt_ref, carry_ref):
    i = pl.program_id(0)

    @pl.when(i == 0)
    def _():
        carry_ref[...] = cin_ref[...].astype(F32)

    half = HG_H * HG_DV
    mixed = _dot(ohg_ref[...].astype(BF16), wo_ref[:half, :]) + _dot(orw_ref[...].astype(BF16), wo_ref[half:, :])
    x1 = x_ref[...] + mixed
    x1_ref[...] = x1
    hf = _rms(x1, gf_ref[...])
    _store_row_tiles(hf_ref, hf)

    h1, h2, _ = _split3(hf)
    w1, w2, _ = _split3(wr_ref[...])
    logits = _dot(h1, w1) + _dot(h1, w2) + _dot(h2, w1) + br_ref[...]

    tm = logits.shape[0]
    lane = lax.broadcasted_iota(I32, (tm, N_EXPERTS), 1).astype(F32)
    lane4 = lax.broadcasted_iota(I32, (tm, TOP_K), 1)
    vals = logits
    maskf = jnp.zeros((tm, N_EXPERTS), F32)
    sels, tops = [], []
    idx_out = jnp.zeros((tm, TOP_K), F32)
    for kq in range(TOP_K):
        m = jnp.max(vals, axis=-1, keepdims=True)
        idx = jnp.min(jnp.where(vals == m, lane, float(N_EXPERTS)), axis=-1, keepdims=True)
        sel = lane == idx
        sels.append(sel)
        tops.append(m)
        idx_out = jnp.where(lane4 == kq, idx, idx_out)
        vals = jnp.where(sel, -jnp.inf, vals)
        maskf = maskf + jnp.where(sel, 1.0, 0.0)
    es = [jnp.exp(t - tops[0]) for t in tops]
    denom = es[0] + es[1] + es[2] + es[3]
    gate_out = jnp.zeros((tm, TOP_K), F32)
    for kq in range(TOP_K):
        gate_out = jnp.where(lane4 == kq, es[kq] / denom, gate_out)

    pos = _dot(tri_ref[...], maskf.astype(BF16)) + carry_ref[...]
    pos_out = jnp.zeros((tm, TOP_K), F32)
    for kq in range(TOP_K):
        pk = jnp.sum(jnp.where(sels[kq], pos, 0.0), axis=-1, keepdims=True)
        pos_out = jnp.where(lane4 == kq, pk, pos_out)
    carry_ref[...] = carry_ref[...] + jnp.sum(maskf, axis=0, keepdims=True)
    idx_ref[...] = idx_out.astype(I32)
    gate_ref[...] = gate_out
    pos_ref[...] = pos_out.astype(I32)
    cnt_ref[...] = carry_ref[...].astype(I32)


def _outproj_router(o_hg, o_rw, x, wo_bf16, gf, w_router, b_router, cnt_in, tm):
    n = x.shape[0]
    tri = (lax.broadcasted_iota(I32, (tm, tm), 0) > lax.broadcasted_iota(I32, (tm, tm), 1)).astype(BF16)
    half = HG_H * HG_DV
    row = lambda i: (i, 0)
    fix = lambda i: (0, 0)
    return pl.pallas_call(
        _outproj_router_kernel,
        grid=(n // tm,),
        in_specs=[pl.BlockSpec((tm, half), row), pl.BlockSpec((tm, RW_W), row), pl.BlockSpec((tm, D_MODEL), row),
                  pl.BlockSpec((half + RW_W, D_MODEL), fix), pl.BlockSpec((1, D_MODEL), fix),
                  pl.BlockSpec((D_MODEL, N_EXPERTS), fix), pl.BlockSpec((1, N_EXPERTS), fix),
                  pl.BlockSpec((tm, tm), fix), pl.BlockSpec((1, N_EXPERTS), fix)],
        out_specs=[pl.BlockSpec((tm, D_MODEL), row), pl.BlockSpec((8 * tm, 128), row),
                   pl.BlockSpec((tm, TOP_K), row), pl.BlockSpec((tm, TOP_K), row), pl.BlockSpec((tm, TOP_K), row),
                   pl.BlockSpec((1, N_EXPERTS), fix)],
        out_shape=[jax.ShapeDtypeStruct((n, D_MODEL), F32), jax.ShapeDtypeStruct((8 * n, 128), F32),
                   jax.ShapeDtypeStruct((n, TOP_K), I32), jax.ShapeDtypeStruct((n, TOP_K), F32),
                   jax.ShapeDtypeStruct((n, TOP_K), I32), jax.ShapeDtypeStruct((1, N_EXPERTS), I32)],
        scratch_shapes=[pltpu.VMEM((1, N_EXPERTS), F32)],
        compiler_params=_cparams(("arbitrary",)),
        name="out_proj_router",
    )(o_hg, o_rw, x, wo_bf16, gf, w_router, b_router, tri, cnt_in)


def _plan_kernel(cnt_ref, idx_ref, pos_ref, dest_ref, bexp_ref, nblk_ref, *, n_blocks):
    cnt = cnt_ref[...].astype(F32)
    padded = jnp.floor((cnt + (MOE_BLOCK - 1)) * (1.0 / MOE_BLOCK)) * MOE_BLOCK
    ei = lax.broadcasted_iota(I32, (N_EXPERTS, N_EXPERTS), 0)
    ej = lax.broadcasted_iota(I32, (N_EXPERTS, N_EXPERTS), 1)
    upper = (ei < ej).astype(BF16)
    upper_incl = (ei <= ej).astype(BF16)
    start = _dot_sel_r(padded, upper)
    ends = _dot_sel_r(padded, upper_incl)
    idx = idx_ref[...]
    dest = pos_ref[...]
    for e in range(N_EXPERTS):
        dest = dest + jnp.where(idx == e, start[:, e:e + 1].astype(I32), 0)
    dest_ref[...] = dest
    bstart = (lax.broadcasted_iota(I32, (n_blocks, 1), 0) * MOE_BLOCK).astype(F32)
    be = jnp.sum(jnp.where(ends <= bstart, 1.0, 0.0), axis=-1, keepdims=True)
    bexp_ref[...] = jnp.minimum(be, N_EXPERTS - 1.0).astype(I32)
    nblk_ref[...] = (ends[:, N_EXPERTS - 1:] * (1.0 / MOE_BLOCK)).astype(I32)


def _plan(cnt, idx, pos, n_blocks, tm):
    n = idx.shape[0]
    row = lambda i: (i, 0)
    fix = lambda i: (0, 0)
    return pl.pallas_call(
        functools.partial(_plan_kernel, n_blocks=n_blocks),
        grid=(n // tm,),
        in_specs=[pl.BlockSpec((1, N_EXPERTS), fix), pl.BlockSpec((tm, TOP_K), row), pl.BlockSpec((tm, TOP_K), row)],
        out_specs=[pl.BlockSpec((tm, TOP_K), row), pl.BlockSpec((n_blocks, 1), fix), pl.BlockSpec((1, 1), fix)],
        out_shape=[jax.ShapeDtypeStruct((n, TOP_K), I32), jax.ShapeDtypeStruct((n_blocks, 1), I32),
                   jax.ShapeDtypeStruct((1, 1), I32)],
        compiler_params=_cparams(("arbitrary",)),
        name="moe_plan",
    )(cnt, idx, pos)


def _row_tile(ref, r):
    return ref.at[pl.ds(pl.multiple_of(r * 8, 8), 8)]


def _scatter_kernel(cnt_ref, nblk_ref, dest_a_ref, dest_b_ref, hf_a_ref, hf_b_ref, xs_ref, zbuf, sem, sem_pad, sem_blk,
                    *, ts, nt_a, n_blocks):
    i = pl.program_id(0)
    blk_rows = 8 * MOE_BLOCK
    shift = MOE_BLOCK.bit_length() - 1

    @pl.when(i == 0)
    def _():
        zbuf[...] = jnp.zeros_like(zbuf)

        def pad_expert(e, start):
            c = cnt_ref[e]
            padded = ((c + (MOE_BLOCK - 1)) >> shift) << shift

            def zrow(r, carry):
                pltpu.make_async_copy(_row_tile(zbuf, 0), _row_tile(xs_ref, start + r), sem_pad).start()
                return carry

            def wrow(r, carry):
                pltpu.make_async_copy(_row_tile(zbuf, 0), _row_tile(xs_ref, 0), sem_pad).wait()
                return carry

            lax.fori_loop(c, padded, zrow, 0)
            lax.fori_loop(c, padded, wrow, 0)
            return start + padded

        lax.fori_loop(0, N_EXPERTS, pad_expert, 0)

        def zblk(b, carry):
            pltpu.make_async_copy(zbuf, xs_ref.at[pl.ds(pl.multiple_of(b * blk_rows, blk_rows), blk_rows)],
                                  sem_blk).start()
            return carry

        def wblk(b, carry):
            pltpu.make_async_copy(zbuf, xs_ref.at[pl.ds(0, blk_rows)], sem_blk).wait()
            return carry

        lax.fori_loop(nblk_ref[0], n_blocks, zblk, 0)
        lax.fori_loop(nblk_ref[0], n_blocks, wblk, 0)

    def scatter_tile(dest_ref, hf_ref):
        def issue(t, carry):
            for kq in range(TOP_K):
                d = dest_ref[t * TOP_K + kq]
                pltpu.make_async_copy(_row_tile(hf_ref, t), _row_tile(xs_ref, d), sem).start()
            return carry

        def drain(t, carry):
            for kq in range(TOP_K):
                pltpu.make_async_copy(_row_tile(hf_ref, 0), _row_tile(xs_ref, 0), sem).wait()
            return carry

        lax.fori_loop(0, ts, issue, 0)
        lax.fori_loop(0, ts, drain, 0)

    @pl.when(i < nt_a)
    def _():
        scatter_tile(dest_a_ref, hf_a_ref)

    @pl.when(i >= nt_a)
    def _():
        scatter_tile(dest_b_ref, hf_b_ref)


def _scatter(cnt, nblk, dest_a, dest_b, hf_a, hf_b, n_blocks, ts):
    assert MOE_BLOCK & (MOE_BLOCK - 1) == 0
    nt_a = hf_a.shape[0] // (8 * ts)
    nt_b = hf_b.shape[0] // (8 * ts)
    amap = lambda i: (jnp.minimum(i, nt_a - 1),)
    bmap = lambda i: (jnp.maximum(i - nt_a, 0),)
    smem = pl.BlockSpec(memory_space=pltpu.SMEM)
    return pl.pallas_call(
        functools.partial(_scatter_kernel, ts=ts, nt_a=nt_a, n_blocks=n_blocks),
        grid=(nt_a + nt_b,),
        in_specs=[smem, smem,
                  pl.BlockSpec((ts * TOP_K,), amap, memory_space=pltpu.SMEM),
                  pl.BlockSpec((ts * TOP_K,), bmap, memory_space=pltpu.SMEM),
                  pl.BlockSpec((8 * ts, 128), lambda i: (jnp.minimum(i, nt_a - 1), 0)),
                  pl.BlockSpec((8 * ts, 128), lambda i: (jnp.maximum(i - nt_a, 0), 0))],
        out_specs=pl.BlockSpec(memory_space=pl.ANY),
        out_shape=jax.ShapeDtypeStruct((8 * n_blocks * MOE_BLOCK, 128), F32),
        scratch_shapes=[pltpu.VMEM((8 * MOE_BLOCK, 128), F32), pltpu.SemaphoreType.DMA(()),
                        pltpu.SemaphoreType.DMA(()), pltpu.SemaphoreType.DMA(())],
        compiler_params=_cparams(("arbitrary",)),
        name="moe_scatter",
    )(cnt, nblk, dest_a, dest_b, hf_a, hf_b)


def _ffn_kernel(bexp_ref, nblk_ref, xs_ref, wgu_ref, bgu_ref, wd_ref, bd_ref, yb_ref, wgu_bf, wd_bf):
    b = pl.program_id(0)
    valid = b < nblk_ref[0]
    prev_e = bexp_ref[jnp.maximum(b, 1) - 1]
    fresh = (b == 0) | (bexp_ref[b] != prev_e)

    @pl.when(valid & fresh)
    def _():
        wgu_bf[...] = wgu_ref[0].astype(BF16)
        wd_bf[...] = wd_ref[0].astype(BF16)

    @pl.when(valid)
    def _():
        x = _load_row_tiles(xs_ref, MOE_BLOCK).astype(BF16)
        gu = _dot(x, wgu_bf[...]) + bgu_ref[0]
        gate = jnp.minimum(gu[:, :D_FF], SWIGLU_LIMIT)
        up = jnp.clip(gu[:, D_FF:], -SWIGLU_LIMIT, SWIGLU_LIMIT)
        glu = gate * _sigmoid(SWIGLU_ALPHA * gate)
        act = ((up + 1.0) * glu).astype(BF16)
        _store_row_tiles(yb_ref, _dot(act, wd_bf[...]) + bd_ref[0])

    @pl.when(jnp.logical_not(valid))
    def _():
        yb_ref[...] = jnp.zeros_like(yb_ref)


def _ffn(bexp, nblk, xs, w_gate_up, b_gate_up, w_down, b_down):
    n_rows = xs.shape[0] // 8
    n_blocks = n_rows // MOE_BLOCK

    def blk(b, be, nb):
        return (jnp.minimum(b, jnp.maximum(nb[0], 1) - 1), 0)

    def wmap(b, be, nb):
        return (be[jnp.minimum(b, jnp.maximum(nb[0], 1) - 1)], 0, 0)

    grid_spec = pltpu.PrefetchScalarGridSpec(
        num_scalar_prefetch=2,
        grid=(n_blocks,),
        in_specs=[pl.BlockSpec((8 * MOE_BLOCK, 128), blk),
                  pl.BlockSpec((1, D_MODEL, 2 * D_FF), wmap),
                  pl.BlockSpec((1, 1, 2 * D_FF), wmap),
                  pl.BlockSpec((1, D_FF, D_MODEL), wmap),
                  pl.BlockSpec((1, 1, D_MODEL), wmap)],
        out_specs=pl.BlockSpec((8 * MOE_BLOCK, 128), lambda b, be, nb: (b, 0)),
        scratch_shapes=[pltpu.VMEM((D_MODEL, 2 * D_FF), BF16), pltpu.VMEM((D_FF, D_MODEL), BF16)],
    )
    return pl.pallas_call(
        _ffn_kernel,
        grid_spec=grid_spec,
        out_shape=jax.ShapeDtypeStruct((8 * n_rows, 128), F32),
        compiler_params=_cparams(("arbitrary",)),
        name="moe_ffn",
    )(bexp, nblk, xs, w_gate_up, b_gate_up, w_down, b_down)


def _combine_kernel(dest_ref, gate_ref, x1_ref, gn_ref, yb_ref, y_ref, buf, sem, *, tc):
    def issue(t, carry):
        for kq in range(TOP_K):
            d = dest_ref[t * TOP_K + kq]
            pltpu.make_async_copy(_row_tile(yb_ref, d), _row_tile(buf.at[kq], t), sem).start()
        return carry

    lax.fori_loop(0, tc, issue, 0)

    def drain(t, carry):
        for kq in range(TOP_K):
            pltpu.make_async_copy(_row_tile(yb_ref, 0), _row_tile(buf.at[0], 0), sem).wait()
        return carry

    lax.fori_loop(0, tc, drain, 0)
    gate = gate_ref[...]
    acc = x1_ref[...]
    for kq in range(TOP_K):
        acc = acc + gate[:, kq:kq + 1] * _load_row_tiles(buf.at[kq], tc)
    y_ref[...] = _rms(acc, gn_ref[...])


def _combine(dest_flat, gate, x1, gn, yb, tc):
    n = x1.shape[0]
    return pl.pallas_call(
        functools.partial(_combine_kernel, tc=tc),
        grid=(n // tc,),
        in_specs=[pl.BlockSpec((tc * TOP_K,), lambda i: (i,), memory_space=pltpu.SMEM),
                  pl.BlockSpec((tc, TOP_K), lambda i: (i, 0)),
                  pl.BlockSpec((tc, D_MODEL), lambda i: (i, 0)),
                  pl.BlockSpec((1, D_MODEL), lambda i: (0, 0)),
                  pl.BlockSpec(memory_space=pl.ANY)],
        out_specs=pl.BlockSpec((tc, D_MODEL), lambda i: (i, 0)),
        out_shape=jax.ShapeDtypeStruct((n, D_MODEL), F32),
        scratch_shapes=[pltpu.VMEM((TOP_K, 8 * tc, 128), F32), pltpu.SemaphoreType.DMA(())],
        compiler_params=_cparams(("arbitrary",)),
        name="moe_combine",
    )(dest_flat, gate, x1, gn, yb)


def _pick_tile(n, pref):
    t = pref
    while n % t:
        t //= 2
    return t


def kernel(x_prompt, x_sample, state_hgrn, state_rwkv, state_shift, norm_mix, w_in, lb_logits, hg_norm_w, rw_mu,
           rw_w0, rw_w2, rw_a0, rw_a2, rw_g2, rw_k_k, rw_k_a, rw_r_k, rw_lnx_w, rw_lnx_b, w_out, norm_ffn,
           w_router, b_router, w_gate_up, b_gate_up, w_down, b_down, norm_final):
    bp, tp, d = x_prompt.shape
    bs, tsq, _ = x_sample.shape
    assert norm_mix.shape[0] == 1 and d == D_MODEL
    n_p, n_s = bp * tp, bs * tsq
    n = n_p + n_s
    xp = x_prompt.reshape(n_p, d)
    xs_ = x_sample.reshape(n_s, d)

    w_in_b = w_in[0].astype(BF16)
    splits = (HG_COLS, RW_COLS)
    hgp_p, rwp_p = _proj(xp, norm_mix, w_in_b, splits, True, _pick_tile(n_p, 256))
    hgp_s, rwp_s = _proj(xs_, norm_mix, w_in_b, splits, True, _pick_tile(n_s, 256))
    ones_d = jnp.ones((1, d), F32)
    (prev_s,) = _proj(state_shift[0], ones_d, w_in_b[:, HG_COLS:], (RW_COLS,), False, _pick_tile(bs, 128))
    prev_p = jnp.zeros((bp, 1, RW_COLS), F32)
    x_last = jnp.concatenate([x_prompt[:, -1, :], x_sample[:, -1, :]], axis=0)
    shift = _rmsnorm_rows(x_last, norm_mix)

    zero_hg = jnp.zeros((bp, HG_H, HG_DK, HG_DV), F32)
    ohg_p, hg_p = _hgrn(hgp_p, lb_logits, hg_norm_w, zero_hg, batch=bp, seq=tp,
                        nseq=1, tb=_pick_tile(tp, 256), c=HG_CHUNK, sub=HG_SUB)
    ohg_s, hg_s = _hgrn(hgp_s, lb_logits, hg_norm_w, state_hgrn[0], batch=bs, seq=tsq,
                        nseq=_pick_tile(bs, 16), tb=tsq, c=tsq, sub=tsq)

    zpad = jnp.zeros((64, RW_W), F32)
    wwa = jnp.concatenate([jnp.concatenate([rw_w2[0], zpad], axis=1),
                           jnp.concatenate([zpad, rw_a2[0]], axis=1)], axis=0).astype(BF16)
    wts = (rw_mu, rw_w0, rw_a0, wwa, rw_g2[0].astype(BF16), rw_k_k, rw_k_a, rw_r_k.reshape(1, RW_W),
           rw_lnx_w, rw_lnx_b)
    zero_rw = jnp.zeros((bp, RW_H, RW_N, RW_N), F32)
    orw_p, rw_p = _rwkv(rwp_p, prev_p, zero_rw, wts, batch=bp, seq=tp, chain=True)
    orw_s, rw_s = _rwkv(rwp_s, prev_s, state_rwkv[0], wts, batch=bs, seq=tsq, chain=False)

    wo_b = w_out[0].astype(BF16)
    tm_p, tm_s = _pick_tile(n_p, 512), _pick_tile(n_s, 512)
    cnt0 = jnp.zeros((1, N_EXPERTS), I32)
    x1_p, hf_p, idx_p, gate_p, pos_p, cnt_p = _outproj_router(ohg_p, orw_p, xp, wo_b, norm_ffn, w_router[0],
                                                              b_router, cnt0, tm_p)
    x1_s, hf_s, idx_s, gate_s, pos_s, cnt = _outproj_router(ohg_s, orw_s, xs_, wo_b, norm_ffn, w_router[0],
                                                            b_router, cnt_p, tm_s)

    n_blocks = -(-(n * TOP_K) // MOE_BLOCK) + N_EXPERTS
    dest_p, bexp, nblk = _plan(cnt, idx_p, pos_p, n_blocks, tm_p)
    dest_s, _, _ = _plan(cnt, idx_s, pos_s, n_blocks, tm_s)
    dest_p = dest_p.reshape(n_p * TOP_K)
    dest_s = dest_s.reshape(n_s * TOP_K)
    nblk = nblk.reshape(1)
    ts = _pick_tile(math.gcd(n_p, n_s), 512)
    xs = _scatter(cnt.reshape(N_EXPERTS), nblk, dest_p, dest_s, hf_p, hf_s, n_blocks, ts)
    yb = _ffn(bexp.reshape(n_blocks), nblk, xs, w_gate_up[0], b_gate_up[0].reshape(N_EXPERTS, 1, 2 * D_FF),
              w_down[0], b_down[0].reshape(N_EXPERTS, 1, d))
    gn = norm_final.reshape(1, d)
    y_p = _combine(dest_p, gate_p, x1_p, gn, yb, _pick_tile(n_p, 256))
    y_s = _combine(dest_s, gate_s, x1_s, gn, yb, _pick_tile(n_s, 256))

    return (y_p.reshape(bp, tp, d), y_s.reshape(bs, tsq, d), hg_p[None], rw_p[None], shift[:bp][None],
            hg_s[None], rw_s[None], shift[bp:][None])
```

```python
import functools
import math

import jax
import jax.numpy as jnp
from jax import lax
from jax.experimental import pallas as pl
from jax.experimental.pallas import tpu as pltpu

F32 = jnp.float32
BF16 = jnp.bfloat16
I32 = jnp.int32

D_MODEL = 1024
HG_H, HG_DK, HG_DV = 4, 128, 128
HG_CHUNK = 64
HG_SUB = 8
RW_H, RW_N = 8, 64
RW_W = RW_H * RW_N
RW_PAIRS = RW_H // 2
RW_TILE = 64
RW_CHUNK = 32
RW_ROWS_CHAIN = 256
RW_ROWS_SHORT = 128
HG_COLS = 2 * HG_H * HG_DK + 2 * HG_H * HG_DV
RW_COLS = 3 * RW_W + 64 + 64 + 128
N_EXPERTS = 32
TOP_K = 4
D_FF = D_MODEL
SWIGLU_LIMIT = 7.0
SWIGLU_ALPHA = 1.702
NORM_EPS = 1e-6
RW_GN_EPS = 64e-5
MOE_BLOCK = 256
VMEM_LIMIT = 56 * 1024 * 1024


def _dot(a, b):
    return jnp.dot(a, b, preferred_element_type=F32)


def _dot_nt(a, b):
    return lax.dot_general(a, b, (((1,), (1,)), ((), ())), preferred_element_type=F32)


def _dot_tn(a, b):
    return lax.dot_general(a, b, (((0,), (0,)), ((), ())), preferred_element_type=F32)


def _split3(x):
    x1 = x.astype(BF16)
    r1 = x - x1.astype(F32)
    x2 = r1.astype(BF16)
    x3 = (r1 - x2.astype(F32)).astype(BF16)
    return x1, x2, x3


def _dot_sel_l(m_bf16, x):
    x1, x2, x3 = _split3(x)
    return _dot(m_bf16, x1) + _dot(m_bf16, x2) + _dot(m_bf16, x3)


def _dot_sel_r(x, m_bf16):
    x1, x2, x3 = _split3(x)
    return _dot(x1, m_bf16) + _dot(x2, m_bf16) + _dot(x3, m_bf16)


def _sigmoid(x):
    return 1.0 / (1.0 + jnp.exp(-x))


def _rms(x, g):
    return x * lax.rsqrt(jnp.mean(x * x, axis=-1, keepdims=True) + NORM_EPS) * g


def _cparams(sem):
    return pltpu.CompilerParams(dimension_semantics=sem, vmem_limit_bytes=VMEM_LIMIT)


def _proj_kernel(x_ref, g_ref, w_ref, *o_refs, normalize, splits):
    x = x_ref[...]
    h = _rms(x, g_ref[...]) if normalize else x
    hb = h.astype(BF16)
    c0 = 0
    for o_ref, width in zip(o_refs, splits):
        o_ref[...] = _dot(hb, w_ref[:, c0:c0 + width])
        c0 += width


def _proj(x, g, w_bf16, splits, normalize, tm):
    n = x.shape[0]
    kern = functools.partial(_proj_kernel, normalize=normalize, splits=splits)
    return pl.pallas_call(
        kern,
        grid=(n // tm,),
        in_specs=[pl.BlockSpec((tm, D_MODEL), lambda i: (i, 0)),
                  pl.BlockSpec((1, D_MODEL), lambda i: (0, 0)),
                  pl.BlockSpec((D_MODEL, sum(splits)), lambda i: (0, 0))],
        out_specs=[pl.BlockSpec((tm, s), lambda i: (i, 0)) for s in splits],
        out_shape=[jax.ShapeDtypeStruct((n, s), F32) for s in splits],
        compiler_params=_cparams(("parallel",)),
        name="norm_in_proj" if normalize else "shift_proj",
    )(x, g, w_bf16)


def _rmsnorm_rows_kernel(x_ref, g_ref, o_ref):
    o_ref[...] = _rms(x_ref[...], g_ref[...])


def _rmsnorm_rows(x, g):
    return pl.pallas_call(
        _rmsnorm_rows_kernel,
        out_shape=jax.ShapeDtypeStruct(x.shape, F32),
        name="shift_norm",
    )(x, g)


def _hgrn_masks(c, sub):
    ri = lax.broadcasted_iota(I32, (c, c), 0)
    ci = lax.broadcasted_iota(I32, (c, c), 1)
    masks = []
    h = sub
    while h < c:
        keep = ((ri // (2 * h)) == (ci // (2 * h))) & ((ri % (2 * h)) >= h) & ((ci % (2 * h)) < h)
        masks.append(keep.astype(F32))
        h *= 2
    return masks


def _hgrn_chunk(qs, fps, ivs, lbs, sts, tri, masks, c, sub):
    nh = len(qs)
    hs = range(nh)
    qf = [q * _sigmoid(q) for q in qs]
    f = [lb + (1.0 - lb) * _sigmoid(fp) for lb, fp in zip(lbs, fps)]
    lf = [jnp.log(z) for z in f]
    kc = [1.0 - z for z in f]
    g = [_dot_sel_l(tri, z) for z in lf]
    o_inter = [_dot_nt((qf[h] * jnp.exp(g[h])).astype(BF16), sts[h].astype(BF16)) for h in hs]
    ivb = [z.astype(BF16) for z in ivs]
    lane = lax.broadcasted_iota(I32, (sub, c), 1)
    row = lax.broadcasted_iota(I32, (sub, c), 0)
    a_rows = [[] for _ in hs]
    for i in range(c // sub):
        lo = i * sub
        blk = slice(lo, lo + sub)
        a = [jnp.zeros((sub, c), F32) for _ in hs]
        for s in range(sub):
            for h in hs:
                gi = g[h][blk]
                e = jnp.exp(jnp.minimum(gi - gi[s:s + 1, :], 0.0))
                col = jnp.sum(qf[h][blk] * (kc[h][lo + s:lo + s + 1, :] * e), axis=-1, keepdims=True)
                a[h] = jnp.where(lane == lo + s, col, a[h])
        for h in hs:
            a_rows[h].append(jnp.where(row + lo >= lane, a[h], 0.0))
    a_tot = [z[0] if len(z) == 1 else jnp.concatenate(z, axis=0) for z in a_rows]
    rowc = lax.broadcasted_iota(I32, (c, 1), 0)
    half = sub
    for mask in masks:
        gref = []
        for h in hs:
            ref = g[h][half - 1:half, :]
            for b in range(1, c // (2 * half)):
                r0 = b * 2 * half + half - 1
                ref = jnp.where(rowc >= b * 2 * half, g[h][r0:r0 + 1, :], ref)
            gref.append(ref)
        ql = [(qf[h] * jnp.exp(jnp.minimum(g[h] - gref[h], 0.0))).astype(BF16) for h in hs]
        kl = [(kc[h] * jnp.exp(jnp.minimum(gref[h] - g[h], 0.0))).astype(BF16) for h in hs]
        a_tot = [a_tot[h] + _dot_nt(ql[h], kl[h]) * mask for h in hs]
        half *= 2
    o = [o_inter[h] + _dot(a_tot[h].astype(BF16), ivb[h]) for h in hs]
    gl = [g[h][c - 1:c, :] for h in hs]
    kd = [(kc[h] * jnp.exp(gl[h] - g[h])).astype(BF16) for h in hs]
    st_new = [sts[h] * jnp.exp(gl[h]) + _dot_tn(ivb[h], kd[h]) for h in hs]
    return o, st_new


def _hgrn_kernel(q_ref, f_ref, i_ref, g_ref, lbl_ref, gw_ref, tri_ref, s0_ref, o_ref, so_ref, st_ref,
                 *, nseq, tb, c, sub):
    j = pl.program_id(1)
    nt = pl.num_programs(1)

    @pl.when(j == 0)
    def _():
        for s in range(nseq):
            for h in range(HG_H):
                st_ref[s * HG_H + h] = s0_ref[s, h].T

    lbl = lbl_ref[...]
    ex = jnp.exp(lbl - jnp.max(lbl, axis=0, keepdims=True))
    lb = ex[0:1, :] / jnp.sum(ex, axis=0, keepdims=True)
    gw = gw_ref[...]
    tri = tri_ref[...]
    masks = _hgrn_masks(c, sub)
    n_chunks = tb // c

    def body(it, carry):
        s = it // n_chunks
        r0 = pl.multiple_of(it * c, c)
        rows = pl.ds(r0, c)
        hsl = [slice(h * HG_DK, (h + 1) * HG_DK) for h in range(HG_H)]
        o, st_new = _hgrn_chunk([q_ref[rows, z] for z in hsl], [f_ref[rows, z] for z in hsl],
                                [i_ref[rows, z] for z in hsl], [lb[:, z] for z in hsl],
                                [st_ref[s * HG_H + h] for h in range(HG_H)], tri, masks, c, sub)
        for h in range(HG_H):
            st_ref[s * HG_H + h] = st_new[h]
            on = o[h] * lax.rsqrt(jnp.mean(o[h] * o[h], axis=-1, keepdims=True) + NORM_EPS) * gw
            gin = g_ref[rows, hsl[h]]
            o_ref[rows, hsl[h]] = on * (gin * _sigmoid(gin))
        return carry

    lax.fori_loop(0, nseq * n_chunks, body, 0)

    @pl.when(j == nt - 1)
    def _():
        for s in range(nseq):
            for h in range(HG_H):
                so_ref[s, h] = st_ref[s * HG_H + h].T


def _hgrn(proj_hg, lb_logits, gn_w, s0, *, batch, seq, nseq, tb, c, sub):
    n = proj_hg.shape[0]
    rows = nseq * tb
    nt = seq // tb
    hw = HG_H * HG_DK
    tri = (lax.broadcasted_iota(I32, (c, c), 0) >= lax.broadcasted_iota(I32, (c, c), 1)).astype(BF16)

    def rmap(col):
        return lambda b, j: (b * nt + j, col)

    kern = functools.partial(_hgrn_kernel, nseq=nseq, tb=tb, c=c, sub=sub)
    in_specs = [pl.BlockSpec((rows, hw), rmap(0)), pl.BlockSpec((rows, hw), rmap(1)),
                pl.BlockSpec((rows, hw), rmap(2)), pl.BlockSpec((rows, hw), rmap(3)),
                pl.BlockSpec((lb_logits.shape[0], hw), lambda b, j: (0, 0)),
                pl.BlockSpec((1, HG_DV), lambda b, j: (0, 0)),
                pl.BlockSpec((c, c), lambda b, j: (0, 0)),
                pl.BlockSpec((nseq, HG_H, HG_DK, HG_DV), lambda b, j: (b, 0, 0, 0))]
    args = [proj_hg, proj_hg, proj_hg, proj_hg, lb_logits, gn_w, tri, s0]
    return pl.pallas_call(
        kern,
        grid=(batch // nseq, nt),
        in_specs=in_specs,
        out_specs=[pl.BlockSpec((rows, hw), lambda b, j: (b * nt + j, 0)),
                   pl.BlockSpec((nseq, HG_H, HG_DK, HG_DV), lambda b, j: (b, 0, 0, 0))],
        out_shape=[jax.ShapeDtypeStruct((n, HG_H * HG_DV), F32),
                   jax.ShapeDtypeStruct((batch, HG_H, HG_DK, HG_DV), F32)],
        scratch_shapes=[pltpu.VMEM((nseq * HG_H, HG_DV, HG_DK), F32)],
        compiler_params=_cparams(("parallel", "arbitrary")),
        name="hgrn2_mix",
    )(*args)


def _softplus(z):
    return jnp.maximum(z, 0.0) + jnp.log(1.0 + jnp.exp(-jnp.abs(z)))


def _rwkv_kernel(p_ref, prev_ref, s0_ref, mu_ref, w0_ref, a0_ref, wwa_ref, g2_ref, kkw_ref, kaw_ref, rk_ref,
                 lnw_ref, lnb_ref, g128_ref, trics_ref, esel_ref, o_ref, so_ref, st_ref, carry_ref,
                 *, c, chain, seq, rows):
    tt = RW_TILE
    ntile = rows // tt
    nc = tt // c
    j = pl.program_id(1)
    nt = pl.num_programs(1)
    lane128 = lax.broadcasted_iota(I32, (1, 128), 1)
    lo_half = lane128 < RW_N

    def block_diag(se, so):
        z = jnp.zeros((RW_N, RW_N), F32)
        return jnp.concatenate([jnp.concatenate([se, z], axis=1), jnp.concatenate([z, so], axis=1)], axis=0)

    def split2(z):
        hi = z.astype(BF16)
        return hi, (z - hi.astype(F32)).astype(BF16)

    p = p_ref[...]
    rolled = pltpu.roll(p, 1, 0)
    rowi = lax.broadcasted_iota(I32, (rows, 1), 0)
    if chain:
        @pl.when(j == 0)
        def _():
            carry_ref[...] = prev_ref[0]
            for pr in range(RW_PAIRS):
                st_ref[pr] = block_diag(s0_ref[0, 2 * pr], s0_ref[0, 2 * pr + 1])

        ps = jnp.where(rowi == 0, carry_ref[...], rolled)
        carry_ref[...] = p[rows - 1:rows, :]
    else:
        first = _dot_sel_l(esel_ref[...], prev_ref[...])
        ps = jnp.where((rowi & (seq - 1)) == 0, first, rolled)
    pm = p + (ps - p) * mu_ref[...]

    r = pm[:, 0:RW_W]
    k = pm[:, RW_W:2 * RW_W]
    v = pm[:, 2 * RW_W:3 * RW_W]
    wa = pm[:, 3 * RW_W:3 * RW_W + 128]
    g_lo = pm[:, 3 * RW_W + 128:3 * RW_W + 256]
    wa = jnp.where(lo_half, jnp.tanh(wa), wa)
    xwa = _dot(wa.astype(BF16), wwa_ref[...])
    log_w = -_softplus(-(w0_ref[...] + xwa[:, :RW_W])) - 0.5
    ld = -jnp.exp(log_w)
    a = _sigmoid(a0_ref[...] + xwa[:, RW_W:])
    gate = _dot(_sigmoid(g_lo).astype(BF16), g2_ref[...])
    g128 = g128_ref[...]

    def gsum(z):
        zs = jnp.concatenate([z[:, i * 128:(i + 1) * 128] for i in range(RW_PAIRS)], axis=0)
        hi, lo = split2(zs)
        s = _dot(hi, g128) + _dot(lo, g128)
        return jnp.concatenate([s[i * rows:(i + 1) * rows] for i in range(RW_PAIRS)], axis=1)

    kkv = k * kkw_ref[...]
    kk = kkv / jnp.maximum(jnp.sqrt(gsum(kkv * kkv)), 1e-12)
    k2 = k * (1.0 + (a - 1.0) * kaw_ref[...])
    beta = kk * a

    ld_hi, ld_lo = split2(ld)
    cs = _dot(trics_ref[...], ld_hi) + _dot(trics_ref[...], ld_lo)
    gc = cs[:rows]
    gtot = cs[rows:]
    e_g = jnp.exp(gc)
    e_ng = jnp.exp(-gc)
    e_l = jnp.exp(gtot - gc)
    ah = -kk * jnp.exp(gc - ld)
    rh = r * e_g
    bh = beta * e_ng
    kh = k2 * e_ng
    bt = beta * e_l
    kt = k2 * e_l
    e_tot = jnp.exp(gtot)

    ri = lax.broadcasted_iota(I32, (2 * tt, 2 * tt), 0)
    ci = lax.broadcasted_iota(I32, (2 * tt, 2 * tt), 1)
    same = (ri // c) == (ci // c)
    m_strict = jnp.where(same & (ci < ri), 1.0, 0.0)
    m_incl = jnp.where(same & (ci <= ri), 1.0, 0.0)
    eye2 = jnp.where(ri == ci, 1.0, 0.0)
    bd = jnp.where((ri // RW_N) == (ci // RW_N), 1.0, 0.0)

    def stack(z):
        return jnp.concatenate([jnp.where(lo_half, z, 0.0), jnp.where(lo_half, 0.0, z)], axis=0)

    def fold(z):
        return z[:tt] + z[tt:]

    rowc = lax.broadcasted_iota(I32, (tt, 1), 0) // c

    def chunk_expand(z):
        return jnp.concatenate([jnp.where(rowc == ch, z, 0.0) for ch in range(nc)], axis=1).astype(BF16)

    units = [(t, pr) for t in range(ntile) for pr in range(RW_PAIRS)]
    hw = 2 * tt

    def usl(z, u):
        return z[u[0] * tt:(u[0] + 1) * tt, u[1] * 128:(u[1] + 1) * 128]

    def bf(zs):
        return [z.astype(BF16) for z in zs]

    def rows2(a, b):
        return jnp.concatenate([a, b], axis=0)

    xr = [stack(usl(rh, u)) for u in units]
    xab, xrb, xbb, xkb, xvb = (bf([stack(usl(z, u)) for u in units]) for z in (ah, rh, bh, kh, v))
    lhs_ar = [rows2(a_, r_) for a_, r_ in zip(xab, xrb)]
    ab = [_dot_nt(l_, b_) for l_, b_ in zip(lhs_ar, xbb)]
    ak = [_dot_nt(l_, k_) for l_, k_ in zip(lhs_ar, xkb)]
    a_ab = [z[:hw] * m_strict for z in ab]
    a_rb = bf([z[hw:] * m_incl for z in ab])
    a_ak = bf([z[:hw] * m_strict for z in ak])
    a_rk = bf([z[hw:] * m_incl for z in ak])
    tm = [eye2 + z for z in a_ab]
    if c > 2:
        pw = [_dot(z, z) for z in bf(a_ab)]
    n = 2
    while n < c:
        pwb = bf(pw)
        if 2 * n < c:
            both = [_dot(rows2(t_.astype(BF16), p_), p_) for t_, p_ in zip(tm, pwb)]
            tm = [t_ + z[:hw] for t_, z in zip(tm, both)]
            pw = [z[hw:] for z in both]
        else:
            tm = [t_ + _dot(t_.astype(BF16), p_) for t_, p_ in zip(tm, pwb)]
        n *= 2
    tmb = bf(tm)
    akv = [_dot(rows2(p_, q_), x_) for p_, q_, x_ in zip(a_ak, a_rk, xvb)]
    tav = [_dot(t_, jnp.concatenate([x_, z[:hw].astype(BF16)], axis=1))
           for t_, x_, z in zip(tmb, xab, akv)]
    rbt = [_dot(r_, z) for r_, z in zip(a_rb, bf(tav))]
    rpb = bf([fold(x_ + z[:, :128]) for x_, z in zip(xr, rbt)])
    y0 = [fold(z[:, 128:] + w[hw:]) for z, w in zip(rbt, akv)]
    ap = bf([fold(z[:, :128]) for z in tav])
    vp = bf([fold(z[:, 128:]) for z in tav])
    bexp = [chunk_expand(usl(bt, u)) for u in units]
    kexp = [chunk_expand(usl(kt, u)) for u in units]
    vpl = bf([usl(v, u) for u in units])
    mt_all = [_dot_tn(a_, b_) for a_, b_ in zip(ap, bexp)]
    ht_all = [_dot_tn(rows2(p_, q_), rows2(b_, k_)) for p_, q_, b_, k_ in zip(vp, vpl, bexp, kexp)]

    sts = [st_ref[pr] for pr in range(RW_PAIRS)] if chain else None
    y_parts = [[] for _ in units]
    for t in range(ntile):
        for ch in range(nc):
            rs = slice(ch * c, (ch + 1) * c)
            cl = slice(ch * 128, (ch + 1) * 128)
            r0 = t * tt + ch * c
            for pr in range(RW_PAIRS):
                u = t * RW_PAIRS + pr
                mt = mt_all[u][:, cl] * bd + eye2 * e_tot[r0:r0 + 1, pr * 128:(pr + 1) * 128]
                ht = ht_all[u][:, cl] * bd
                if chain:
                    st = sts[pr]
                else:
                    sq = t * nc + ch
                    st = block_diag(s0_ref[sq, 2 * pr], s0_ref[sq, 2 * pr + 1])
                stb = st.astype(BF16)
                y_parts[u].append(_dot_nt(rpb[u][rs], stb) + y0[u][rs])
                st = _dot(stb, mt.astype(BF16)) + ht
                if chain:
                    sts[pr] = st
                else:
                    so_ref[sq, 2 * pr] = st[:RW_N, :RW_N]
                    so_ref[sq, 2 * pr + 1] = st[RW_N:, RW_N:]
    y_tiles = [jnp.concatenate([jnp.concatenate(y_parts[t * RW_PAIRS + pr], axis=0) for pr in range(RW_PAIRS)],
                               axis=1) for t in range(ntile)]
    y = y_tiles[0] if ntile == 1 else jnp.concatenate(y_tiles, axis=0)

    if chain:
        for pr in range(RW_PAIRS):
            st_ref[pr] = sts[pr]

        @pl.when(j == nt - 1)
        def _():
            for pr in range(RW_PAIRS):
                st = st_ref[pr]
                so_ref[0, 2 * pr] = st[:RW_N, :RW_N]
                so_ref[0, 2 * pr + 1] = st[RW_N:, RW_N:]

    inv_n = 1.0 / RW_N
    mean = gsum(y) * inv_n
    yc = y - mean
    var = gsum(yc * yc) * inv_n
    yn = yc * lax.rsqrt(var + RW_GN_EPS) * lnw_ref[...] + lnb_ref[...]
    bonus = gsum(r * k2 * rk_ref[...]) * v
    o_ref[...] = (yn + bonus) * gate


def _rwkv(proj_rw, prev, s0, wts, *, batch, seq, chain):
    n = proj_rw.shape[0]
    if chain:
        rows = _pick_tile(seq, RW_ROWS_CHAIN)
        nseq, c, nt = 1, RW_CHUNK, seq // rows
        prev_spec = pl.BlockSpec((1, 1, RW_COLS), lambda b, j: (b, 0, 0))
    else:
        assert seq & (seq - 1) == 0 and RW_TILE % seq == 0
        rows = _pick_tile(batch * seq, RW_ROWS_SHORT)
        nseq, c, nt = rows // seq, seq, 1
        prev_spec = pl.BlockSpec((nseq, RW_COLS), lambda b, j: (b, 0))
    ri = lax.broadcasted_iota(I32, (rows, rows), 0)
    ci = lax.broadcasted_iota(I32, (rows, rows), 1)
    same = (ri // c) == (ci // c)
    trics = jnp.concatenate([same & (ci <= ri), same], axis=0).astype(BF16)
    gi = lax.broadcasted_iota(I32, (128, 128), 0) // RW_N
    gj = lax.broadcasted_iota(I32, (128, 128), 1) // RW_N
    g128 = (gi == gj).astype(BF16)
    esel = (lax.broadcasted_iota(I32, (rows, nseq), 0) // seq
            == lax.broadcasted_iota(I32, (rows, nseq), 1)).astype(BF16)
    (mu, w0, a0, wwa, g2, kkw, kaw, rk, lnw, lnb) = wts

    def full(arr):
        return pl.BlockSpec(arr.shape, lambda b, j: (0,) * arr.ndim)

    consts = [mu, w0, a0, wwa, g2, kkw, kaw, rk, lnw, lnb, g128, trics, esel]
    in_specs = [pl.BlockSpec((rows, RW_COLS), lambda b, j: (b * nt + j, 0)),
                prev_spec,
                pl.BlockSpec((nseq, RW_H, RW_N, RW_N), lambda b, j: (b, 0, 0, 0))]
    in_specs += [full(t) for t in consts]
    args = [proj_rw, prev, s0] + consts
    kern = functools.partial(_rwkv_kernel, c=c, chain=chain, seq=seq, rows=rows)
    return pl.pallas_call(
        kern,
        grid=(batch // nseq, nt),
        in_specs=in_specs,
        out_specs=[pl.BlockSpec((rows, RW_W), lambda b, j: (b * nt + j, 0)),
                   pl.BlockSpec((nseq, RW_H, RW_N, RW_N), lambda b, j: (b, 0, 0, 0))],
        out_shape=[jax.ShapeDtypeStruct((n, RW_W), F32),
                   jax.ShapeDtypeStruct((batch, RW_H, RW_N, RW_N), F32)],
        scratch_shapes=[pltpu.VMEM((RW_PAIRS, 128, 128), F32), pltpu.VMEM((1, RW_COLS), F32)],
        compiler_params=_cparams(("parallel", "arbitrary")),
        name="rwkv7_mix",
    )(*args)


def _store_row_tiles(ref, x):
    m = x.shape[0]
    for s in range(D_MODEL // 128):
        ref[pl.ds(s, m, stride=8), :] = x[:, s * 128:(s + 1) * 128]


def _load_row_tiles(ref, m):
    return jnp.concatenate([ref[pl.ds(s, m, stride=8), :] for s in range(D_MODEL // 128)], axis=1)


def _outproj_router_kernel(ohg_ref, orw_ref, x_ref, wo_ref, gf_ref, wr_ref, br_ref, tri_ref, cin_ref,
                           x1_ref, hf_ref, idx_ref, gate_ref, pos_ref, cnt_ref, carry_ref):
    i = pl.program_id(0)

    @pl.when(i == 0)
    def _():
        carry_ref[...] = cin_ref[...].astype(F32)

    half = HG_H * HG_DV
    mixed = _dot(ohg_ref[...].astype(BF16), wo_ref[:half, :]) + _dot(orw_ref[...].astype(BF16), wo_ref[half:, :])
    x1 = x_ref[...] + mixed
    x1_ref[...] = x1
    hf = _rms(x1, gf_ref[...])
    _store_row_tiles(hf_ref, hf)

    h1, h2, _ = _split3(hf)
    w1, w2, _ = _split3(wr_ref[...])
    logits = _dot(h1, w1) + _dot(h1, w2) + _dot(h2, w1) + br_ref[...]

    tm = logits.shape[0]
    lane = lax.broadcasted_iota(I32, (tm, N_EXPERTS), 1).astype(F32)
    lane4 = lax.broadcasted_iota(I32, (tm, TOP_K), 1)
    vals = logits
    maskf = jnp.zeros((tm, N_EXPERTS), F32)
    sels, tops = [], []
    idx_out = jnp.zeros((tm, TOP_K), F32)
    for kq in range(TOP_K):
        m = jnp.max(vals, axis=-1, keepdims=True)
        idx = jnp.min(jnp.where(vals == m, lane, float(N_EXPERTS)), axis=-1, keepdims=True)
        sel = lane == idx
        sels.append(sel)
        tops.append(m)
        idx_out = jnp.where(lane4 == kq, idx, idx_out)
        vals = jnp.where(sel, -jnp.inf, vals)
        maskf = maskf + jnp.where(sel, 1.0, 0.0)
    es = [jnp.exp(t - tops[0]) for t in tops]
    denom = es[0] + es[1] + es[2] + es[3]
    gate_out = jnp.zeros((tm, TOP_K), F32)
    for kq in range(TOP_K):
        gate_out = jnp.where(lane4 == kq, es[kq] / denom, gate_out)

    pos = _dot(tri_ref[...], maskf.astype(BF16)) + carry_ref[...]
    pos_out = jnp.zeros((tm, TOP_K), F32)
    for kq in range(TOP_K):
        pk = jnp.sum(jnp.where(sels[kq], pos, 0.0), axis=-1, keepdims=True)
        pos_out = jnp.where(lane4 == kq, pk, pos_out)
    carry_ref[...] = carry_ref[...] + jnp.sum(maskf, axis=0, keepdims=True)
    idx_ref[...] = idx_out.astype(I32)
    gate_ref[...] = gate_out
    pos_ref[...] = pos_out.astype(I32)
    cnt_ref[...] = carry_ref[...].astype(I32)


def _outproj_router(o_hg, o_rw, x, wo_bf16, gf, w_router, b_router, cnt_in, tm):
    n = x.shape[0]
    tri = (lax.broadcasted_iota(I32, (tm, tm), 0) > lax.broadcasted_iota(I32, (tm, tm), 1)).astype(BF16)
    half = HG_H * HG_DV
    row = lambda i: (i, 0)
    fix = lambda i: (0, 0)
    return pl.pallas_call(
        _outproj_router_kernel,
        grid=(n // tm,),
        in_specs=[pl.BlockSpec((tm, half), row), pl.BlockSpec((tm, RW_W), row), pl.BlockSpec((tm, D_MODEL), row),
                  pl.BlockSpec((half + RW_W, D_MODEL), fix), pl.BlockSpec((1, D_MODEL), fix),
                  pl.BlockSpec((D_MODEL, N_EXPERTS), fix), pl.BlockSpec((1, N_EXPERTS), fix),
                  pl.BlockSpec((tm, tm), fix), pl.BlockSpec((1, N_EXPERTS), fix)],
        out_specs=[pl.BlockSpec((tm, D_MODEL), row), pl.BlockSpec((8 * tm, 128), row),
                   pl.BlockSpec((tm, TOP_K), row), pl.BlockSpec((tm, TOP_K), row), pl.BlockSpec((tm, TOP_K), row),
                   pl.BlockSpec((1, N_EXPERTS), fix)],
        out_shape=[jax.ShapeDtypeStruct((n, D_MODEL), F32), jax.ShapeDtypeStruct((8 * n, 128), F32),
                   jax.ShapeDtypeStruct((n, TOP_K), I32), jax.ShapeDtypeStruct((n, TOP_K), F32),
                   jax.ShapeDtypeStruct((n, TOP_K), I32), jax.ShapeDtypeStruct((1, N_EXPERTS), I32)],
        scratch_shapes=[pltpu.VMEM((1, N_EXPERTS), F32)],
        compiler_params=_cparams(("arbitrary",)),
        name="out_proj_router",
    )(o_hg, o_rw, x, wo_bf16, gf, w_router, b_router, tri, cnt_in)


def _plan_kernel(cnt_ref, start_ref, bexp_ref, nblk_ref, *, n_blocks):
    cnt = cnt_ref[...].astype(F32)
    padded = jnp.floor((cnt + (MOE_BLOCK - 1)) * (1.0 / MOE_BLOCK)) * MOE_BLOCK
    ei = lax.broadcasted_iota(I32, (N_EXPERTS, N_EXPERTS), 0)
    ej = lax.broadcasted_iota(I32, (N_EXPERTS, N_EXPERTS), 1)
    upper = (ei < ej).astype(BF16)
    upper_incl = (ei <= ej).astype(BF16)
    start = _dot_sel_r(padded, upper)
    ends = _dot_sel_r(padded, upper_incl)
    start_ref[...] = start.astype(I32)
    bstart = (lax.broadcasted_iota(I32, (n_blocks, 1), 0) * MOE_BLOCK).astype(F32)
    be = jnp.sum(jnp.where(ends <= bstart, 1.0, 0.0), axis=-1, keepdims=True)
    bexp_ref[...] = jnp.minimum(be, N_EXPERTS - 1.0).astype(I32)
    nblk_ref[...] = (ends[:, N_EXPERTS - 1:] * (1.0 / MOE_BLOCK)).astype(I32)


def _plan(cnt, n_blocks):
    return pl.pallas_call(
        functools.partial(_plan_kernel, n_blocks=n_blocks),
        out_shape=[jax.ShapeDtypeStruct((1, N_EXPERTS), I32), jax.ShapeDtypeStruct((n_blocks, 1), I32),
                   jax.ShapeDtypeStruct((1, 1), I32)],
        name="moe_plan",
    )(cnt)


def _row_tile(ref, r):
    return ref.at[pl.ds(pl.multiple_of(r * 8, 8), 8)]


def _scatter_kernel(cnt_ref, nblk_ref, start_ref, idx_a_ref, pos_a_ref, idx_b_ref, pos_b_ref, hf_a_ref, hf_b_ref,
                    xs_ref, zbuf, sem, sem_pad, sem_blk, *, ts, nt_a, n_blocks):
    i = pl.program_id(0)
    blk_rows = 8 * MOE_BLOCK
    shift = MOE_BLOCK.bit_length() - 1

    @pl.when(i == 0)
    def _():
        zbuf[...] = jnp.zeros_like(zbuf)

        def pad_expert(e, start):
            c = cnt_ref[e]
            padded = ((c + (MOE_BLOCK - 1)) >> shift) << shift

            def zrow(r, carry):
                pltpu.make_async_copy(_row_tile(zbuf, 0), _row_tile(xs_ref, start + r), sem_pad).start()
                return carry

            def wrow(r, carry):
                pltpu.make_async_copy(_row_tile(zbuf, 0), _row_tile(xs_ref, 0), sem_pad).wait()
                return carry

            lax.fori_loop(c, padded, zrow, 0)
            lax.fori_loop(c, padded, wrow, 0)
            return start + padded

        lax.fori_loop(0, N_EXPERTS, pad_expert, 0)

        def zblk(b, carry):
            pltpu.make_async_copy(zbuf, xs_ref.at[pl.ds(pl.multiple_of(b * blk_rows, blk_rows), blk_rows)],
                                  sem_blk).start()
            return carry

        def wblk(b, carry):
            pltpu.make_async_copy(zbuf, xs_ref.at[pl.ds(0, blk_rows)], sem_blk).wait()
            return carry

        lax.fori_loop(nblk_ref[0], n_blocks, zblk, 0)
        lax.fori_loop(nblk_ref[0], n_blocks, wblk, 0)

    def scatter_tile(idx_ref, pos_ref, hf_ref):
        def issue(t, carry):
            for kq in range(TOP_K):
                d = start_ref[idx_ref[t * TOP_K + kq]] + pos_ref[t * TOP_K + kq]
                pltpu.make_async_copy(_row_tile(hf_ref, t), _row_tile(xs_ref, d), sem).start()
            return carry

        def drain(t, carry):
            for kq in range(TOP_K):
                pltpu.make_async_copy(_row_tile(hf_ref, 0), _row_tile(xs_ref, 0), sem).wait()
            return carry

        lax.fori_loop(0, ts, issue, 0)
        lax.fori_loop(0, ts, drain, 0)

    @pl.when(i < nt_a)
    def _():
        scatter_tile(idx_a_ref, pos_a_ref, hf_a_ref)

    @pl.when(i >= nt_a)
    def _():
        scatter_tile(idx_b_ref, pos_b_ref, hf_b_ref)


def _scatter(cnt, nblk, start, idx_a, pos_a, idx_b, pos_b, hf_a, hf_b, n_blocks, ts):
    assert MOE_BLOCK & (MOE_BLOCK - 1) == 0
    nt_a = hf_a.shape[0] // (8 * ts)
    nt_b = hf_b.shape[0] // (8 * ts)
    amap = lambda i: (jnp.minimum(i, nt_a - 1),)
    bmap = lambda i: (jnp.maximum(i - nt_a, 0),)
    smem = pl.BlockSpec(memory_space=pltpu.SMEM)
    return pl.pallas_call(
        functools.partial(_scatter_kernel, ts=ts, nt_a=nt_a, n_blocks=n_blocks),
        grid=(nt_a + nt_b,),
        in_specs=[smem, smem, smem,
                  pl.BlockSpec((ts * TOP_K,), amap, memory_space=pltpu.SMEM),
                  pl.BlockSpec((ts * TOP_K,), amap, memory_space=pltpu.SMEM),
                  pl.BlockSpec((ts * TOP_K,), bmap, memory_space=pltpu.SMEM),
                  pl.BlockSpec((ts * TOP_K,), bmap, memory_space=pltpu.SMEM),
                  pl.BlockSpec((8 * ts, 128), lambda i: (jnp.minimum(i, nt_a - 1), 0)),
                  pl.BlockSpec((8 * ts, 128), lambda i: (jnp.maximum(i - nt_a, 0), 0))],
        out_specs=pl.BlockSpec(memory_space=pl.ANY),
        out_shape=jax.ShapeDtypeStruct((8 * n_blocks * MOE_BLOCK, 128), F32),
        scratch_shapes=[pltpu.VMEM((8 * MOE_BLOCK, 128), F32), pltpu.SemaphoreType.DMA(()),
                        pltpu.SemaphoreType.DMA(()), pltpu.SemaphoreType.DMA(())],
        compiler_params=_cparams(("arbitrary",)),
        name="moe_scatter",
    )(cnt, nblk, start, idx_a, pos_a, idx_b, pos_b, hf_a, hf_b)


def _ffn_kernel(bexp_ref, nblk_ref, xs_ref, wgu_ref, bgu_ref, wd_ref, bd_ref, yb_ref, wgu_bf, wd_bf):
    b = pl.program_id(0)
    valid = b < nblk_ref[0]
    prev_e = bexp_ref[jnp.maximum(b, 1) - 1]
    fresh = (b == 0) | (bexp_ref[b] != prev_e)

    @pl.when(valid & fresh)
    def _():
        wgu_bf[...] = wgu_ref[0].astype(BF16)
        wd_bf[...] = wd_ref[0].astype(BF16)

    @pl.when(valid)
    def _():
        x = _load_row_tiles(xs_ref, MOE_BLOCK).astype(BF16)
        gu = _dot(x, wgu_bf[...]) + bgu_ref[0]
        gate = jnp.minimum(gu[:, :D_FF], SWIGLU_LIMIT)
        up = jnp.clip(gu[:, D_FF:], -SWIGLU_LIMIT, SWIGLU_LIMIT)
        glu = gate * _sigmoid(SWIGLU_ALPHA * gate)
        act = ((up + 1.0) * glu).astype(BF16)
        _store_row_tiles(yb_ref, _dot(act, wd_bf[...]) + bd_ref[0])

    @pl.when(jnp.logical_not(valid))
    def _():
        yb_ref[...] = jnp.zeros_like(yb_ref)


def _ffn(bexp, nblk, xs, w_gate_up, b_gate_up, w_down, b_down):
    n_rows = xs.shape[0] // 8
    n_blocks = n_rows // MOE_BLOCK

    def blk(b, be, nb):
        return (jnp.minimum(b, jnp.maximum(nb[0], 1) - 1), 0)

    def wmap(b, be, nb):
        return (be[jnp.minimum(b, jnp.maximum(nb[0], 1) - 1)], 0, 0)

    grid_spec = pltpu.PrefetchScalarGridSpec(
        num_scalar_prefetch=2,
        grid=(n_blocks,),
        in_specs=[pl.BlockSpec((8 * MOE_BLOCK, 128), blk),
                  pl.BlockSpec((1, D_MODEL, 2 * D_FF), wmap),
                  pl.BlockSpec((1, 1, 2 * D_FF), wmap),
                  pl.BlockSpec((1, D_FF, D_MODEL), wmap),
                  pl.BlockSpec((1, 1, D_MODEL), wmap)],
        out_specs=pl.BlockSpec((8 * MOE_BLOCK, 128), lambda b, be, nb: (b, 0)),
        scratch_shapes=[pltpu.VMEM((D_MODEL, 2 * D_FF), BF16), pltpu.VMEM((D_FF, D_MODEL), BF16)],
    )
    return pl.pallas_call(
        _ffn_kernel,
        grid_spec=grid_spec,
        out_shape=jax.ShapeDtypeStruct((8 * n_rows, 128), F32),
        compiler_params=_cparams(("arbitrary",)),
        name="moe_ffn",
    )(bexp, nblk, xs, w_gate_up, b_gate_up, w_down, b_down)


def _combine_kernel(start_ref, idx_ref, pos_ref, idx_nx_ref, pos_nx_ref, gate_ref, x1_ref, gn_ref, yb_ref, y_ref,
                    buf, sems, *, tc):
    i = pl.program_id(0)
    slot = i % 2

    def fetch(i_ref, p_ref, s):
        def issue(t, carry):
            for kq in range(TOP_K):
                d = start_ref[i_ref[t * TOP_K + kq]] + p_ref[t * TOP_K + kq]
                pltpu.make_async_copy(_row_tile(yb_ref, d), _row_tile(buf.at[s, kq], t), sems.at[s]).start()
            return carry

        lax.fori_loop(0, tc, issue, 0)

    @pl.when(i == 0)
    def _():
        fetch(idx_ref, pos_ref, 0)

    @pl.when(i + 1 < pl.num_programs(0))
    def _():
        fetch(idx_nx_ref, pos_nx_ref, 1 - slot)

    def drain(t, carry):
        for kq in range(TOP_K):
            pltpu.make_async_copy(_row_tile(yb_ref, 0), _row_tile(buf.at[slot, 0], 0), sems.at[slot]).wait()
        return carry

    lax.fori_loop(0, tc, drain, 0)
    gate = gate_ref[...]
    acc = x1_ref[...]
    for kq in range(TOP_K):
        acc = acc + gate[:, kq:kq + 1] * _load_row_tiles(buf.at[slot, kq], tc)
    y_ref[...] = _rms(acc, gn_ref[...])


def _combine(start, idx_flat, pos_flat, gate, x1, gn, yb, tc):
    n = x1.shape[0]
    nt = n // tc
    cur = lambda i: (i,)
    nxt = lambda i: (jnp.minimum(i + 1, nt - 1),)
    return pl.pallas_call(
        functools.partial(_combine_kernel, tc=tc),
        grid=(nt,),
        in_specs=[pl.BlockSpec(memory_space=pltpu.SMEM),
                  pl.BlockSpec((tc * TOP_K,), cur, memory_space=pltpu.SMEM),
                  pl.BlockSpec((tc * TOP_K,), cur, memory_space=pltpu.SMEM),
                  pl.BlockSpec((tc * TOP_K,), nxt, memory_space=pltpu.SMEM),
                  pl.BlockSpec((tc * TOP_K,), nxt, memory_space=pltpu.SMEM),
                  pl.BlockSpec((tc, TOP_K), lambda i: (i, 0)),
                  pl.BlockSpec((tc, D_MODEL), lambda i: (i, 0)),
                  pl.BlockSpec((1, D_MODEL), lambda i: (0, 0)),
                  pl.BlockSpec(memory_space=pl.ANY)],
        out_specs=pl.BlockSpec((tc, D_MODEL), lambda i: (i, 0)),
        out_shape=jax.ShapeDtypeStruct((n, D_MODEL), F32),
        scratch_shapes=[pltpu.VMEM((2, TOP_K, 8 * tc, 128), F32), pltpu.SemaphoreType.DMA((2,))],
        compiler_params=_cparams(("arbitrary",)),
        name="moe_combine",
    )(start, idx_flat, pos_flat, idx_flat, pos_flat, gate, x1, gn, yb)


def _pick_tile(n, pref):
    t = pref
    while n % t:
        t //= 2
    return t


def kernel(x_prompt, x_sample, state_hgrn, state_rwkv, state_shift, norm_mix, w_in, lb_logits, hg_norm_w, rw_mu,
           rw_w0, rw_w2, rw_a0, rw_a2, rw_g2, rw_k_k, rw_k_a, rw_r_k, rw_lnx_w, rw_lnx_b, w_out, norm_ffn,
           w_router, b_router, w_gate_up, b_gate_up, w_down, b_down, norm_final):
    bp, tp, d = x_prompt.shape
    bs, tsq, _ = x_sample.shape
    assert norm_mix.shape[0] == 1 and d == D_MODEL
    n_p, n_s = bp * tp, bs * tsq
    n = n_p + n_s
    xp = x_prompt.reshape(n_p, d)
    xs_ = x_sample.reshape(n_s, d)

    w_in_b = w_in[0].astype(BF16)
    splits = (HG_COLS, RW_COLS)
    hgp_p, rwp_p = _proj(xp, norm_mix, w_in_b, splits, True, _pick_tile(n_p, 512))
    hgp_s, rwp_s = _proj(xs_, norm_mix, w_in_b, splits, True, _pick_tile(n_s, 256))
    ones_d = jnp.ones((1, d), F32)
    (prev_s,) = _proj(state_shift[0], ones_d, w_in_b[:, HG_COLS:], (RW_COLS,), False, _pick_tile(bs, 128))
    prev_p = jnp.zeros((bp, 1, RW_COLS), F32)
    x_last = jnp.concatenate([x_prompt[:, -1, :], x_sample[:, -1, :]], axis=0)
    shift = _rmsnorm_rows(x_last, norm_mix)

    zero_hg = jnp.zeros((bp, HG_H, HG_DK, HG_DV), F32)
    ohg_p, hg_p = _hgrn(hgp_p, lb_logits, hg_norm_w, zero_hg, batch=bp, seq=tp,
                        nseq=1, tb=_pick_tile(tp, 256), c=HG_CHUNK, sub=HG_SUB)
    ohg_s, hg_s = _hgrn(hgp_s, lb_logits, hg_norm_w, state_hgrn[0], batch=bs, seq=tsq,
                        nseq=_pick_tile(bs, 16), tb=tsq, c=tsq, sub=tsq)

    zpad = jnp.zeros((64, RW_W), F32)
    wwa = jnp.concatenate([jnp.concatenate([rw_w2[0], zpad], axis=1),
                           jnp.concatenate([zpad, rw_a2[0]], axis=1)], axis=0).astype(BF16)
    wts = (rw_mu, rw_w0, rw_a0, wwa, rw_g2[0].astype(BF16), rw_k_k, rw_k_a, rw_r_k.reshape(1, RW_W),
           rw_lnx_w, rw_lnx_b)
    zero_rw = jnp.zeros((bp, RW_H, RW_N, RW_N), F32)
    orw_p, rw_p = _rwkv(rwp_p, prev_p, zero_rw, wts, batch=bp, seq=tp, chain=True)
    orw_s, rw_s = _rwkv(rwp_s, prev_s, state_rwkv[0], wts, batch=bs, seq=tsq, chain=False)

    wo_b = w_out[0].astype(BF16)
    tm_p, tm_s = _pick_tile(n_p, 512), _pick_tile(n_s, 512)
    cnt0 = jnp.zeros((1, N_EXPERTS), I32)
    x1_p, hf_p, idx_p, gate_p, pos_p, cnt_p = _outproj_router(ohg_p, orw_p, xp, wo_b, norm_ffn, w_router[0],
                                                              b_router, cnt0, tm_p)
    x1_s, hf_s, idx_s, gate_s, pos_s, cnt = _outproj_router(ohg_s, orw_s, xs_, wo_b, norm_ffn, w_router[0],
                                                            b_router, cnt_p, tm_s)

    n_blocks = -(-(n * TOP_K) // MOE_BLOCK) + N_EXPERTS
    start, bexp, nblk = _plan(cnt, n_blocks)
    start = start.reshape(N_EXPERTS)
    nblk = nblk.reshape(1)
    idx_p, pos_p = idx_p.reshape(n_p * TOP_K), pos_p.reshape(n_p * TOP_K)
    idx_s, pos_s = idx_s.reshape(n_s * TOP_K), pos_s.reshape(n_s * TOP_K)
    ts = _pick_tile(math.gcd(n_p, n_s), 512)
    xs = _scatter(cnt.reshape(N_EXPERTS), nblk, start, idx_p, pos_p, idx_s, pos_s, hf_p, hf_s, n_blocks, ts)
    yb = _ffn(bexp.reshape(n_blocks), nblk, xs, w_gate_up[0], b_gate_up[0].reshape(N_EXPERTS, 1, 2 * D_FF),
              w_down[0], b_down[0].reshape(N_EXPERTS, 1, d))
    gn = norm_final.reshape(1, d)
    y_p = _combine(start, idx_p, pos_p, gate_p, x1_p, gn, yb, _pick_tile(n_p, 256))
    y_s = _combine(start, idx_s, pos_s, gate_s, x1_s, gn, yb, _pick_tile(n_s, 256))

    return (y_p.reshape(bp, tp, d), y_s.reshape(bs, tsq, d), hg_p[None], rw_p[None], shift[:bp][None],
            hg_s[None], rw_s[None], shift[bp:][None])
```

```python
import functools
import math

import jax
import jax.numpy as jnp
from jax import lax
from jax.experimental import pallas as pl
from jax.experimental.pallas import tpu as pltpu

F32 = jnp.float32
BF16 = jnp.bfloat16
I32 = jnp.int32

D_MODEL = 1024
HG_H, HG_DK, HG_DV = 4, 128, 128
HG_CHUNK = 64
HG_SUB = 8
RW_H, RW_N = 8, 64
RW_W = RW_H * RW_N
RW_PAIRS = RW_H // 2
RW_TILE = 64
RW_CHUNK = 32
RW_ROWS_CHAIN = 256
RW_ROWS_SHORT = 128
HG_COLS = 2 * HG_H * HG_DK + 2 * HG_H * HG_DV
RW_COLS = 3 * RW_W + 64 + 64 + 128
N_EXPERTS = 32
TOP_K = 4
D_FF = D_MODEL
SWIGLU_LIMIT = 7.0
SWIGLU_ALPHA = 1.702
NORM_EPS = 1e-6
RW_GN_EPS = 64e-5
MOE_BLOCK = 256
VMEM_LIMIT = 56 * 1024 * 1024


def _dot(a, b):
    return jnp.dot(a, b, preferred_element_type=F32)


def _dot_nt(a, b):
    return lax.dot_general(a, b, (((1,), (1,)), ((), ())), preferred_element_type=F32)


def _dot_tn(a, b):
    return lax.dot_general(a, b, (((0,), (0,)), ((), ())), preferred_element_type=F32)


def _split3(x):
    x1 = x.astype(BF16)
    r1 = x - x1.astype(F32)
    x2 = r1.astype(BF16)
    x3 = (r1 - x2.astype(F32)).astype(BF16)
    return x1, x2, x3


def _dot_sel_l(m_bf16, x):
    x1, x2, x3 = _split3(x)
    return _dot(m_bf16, x1) + _dot(m_bf16, x2) + _dot(m_bf16, x3)


def _dot_sel_r(x, m_bf16):
    x1, x2, x3 = _split3(x)
    return _dot(x1, m_bf16) + _dot(x2, m_bf16) + _dot(x3, m_bf16)


def _sigmoid(x):
    return 1.0 / (1.0 + jnp.exp(-x))


def _rms(x, g):
    return x * lax.rsqrt(jnp.mean(x * x, axis=-1, keepdims=True) + NORM_EPS) * g


def _cparams(sem):
    return pltpu.CompilerParams(dimension_semantics=sem, vmem_limit_bytes=VMEM_LIMIT)


def _proj_kernel(x_ref, g_ref, w_ref, *o_refs, normalize, splits):
    x = x_ref[...]
    h = _rms(x, g_ref[...]) if normalize else x
    hb = h.astype(BF16)
    c0 = 0
    for o_ref, width in zip(o_refs, splits):
        o_ref[...] = _dot(hb, w_ref[:, c0:c0 + width])
        c0 += width


def _proj(x, g, w_bf16, splits, normalize, tm):
    n = x.shape[0]
    kern = functools.partial(_proj_kernel, normalize=normalize, splits=splits)
    return pl.pallas_call(
        kern,
        grid=(n // tm,),
        in_specs=[pl.BlockSpec((tm, D_MODEL), lambda i: (i, 0)),
                  pl.BlockSpec((1, D_MODEL), lambda i: (0, 0)),
                  pl.BlockSpec((D_MODEL, sum(splits)), lambda i: (0, 0))],
        out_specs=[pl.BlockSpec((tm, s), lambda i: (i, 0)) for s in splits],
        out_shape=[jax.ShapeDtypeStruct((n, s), F32) for s in splits],
        compiler_params=_cparams(("parallel",)),
        name="norm_in_proj" if normalize else "shift_proj",
    )(x, g, w_bf16)


def _rmsnorm_rows_kernel(x_ref, g_ref, o_ref):
    o_ref[...] = _rms(x_ref[...], g_ref[...])


def _rmsnorm_rows(x, g):
    return pl.pallas_call(
        _rmsnorm_rows_kernel,
        out_shape=jax.ShapeDtypeStruct(x.shape, F32),
        name="shift_norm",
    )(x, g)


def _hgrn_masks(c, sub):
    ri = lax.broadcasted_iota(I32, (c, c), 0)
    ci = lax.broadcasted_iota(I32, (c, c), 1)
    masks = []
    h = sub
    while h < c:
        keep = ((ri // (2 * h)) == (ci // (2 * h))) & ((ri % (2 * h)) >= h) & ((ci % (2 * h)) < h)
        masks.append(keep.astype(F32))
        h *= 2
    return masks


def _hgrn_chunk(qs, fps, ivs, lbs, sts, tri, masks, c, sub):
    nh = len(qs)
    hs = range(nh)
    qf = [q * _sigmoid(q) for q in qs]
    f = [lb + (1.0 - lb) * _sigmoid(fp) for lb, fp in zip(lbs, fps)]
    lf = [jnp.log(z) for z in f]
    kc = [1.0 - z for z in f]
    g = [_dot_sel_l(tri, z) for z in lf]
    o_inter = [_dot_nt((qf[h] * jnp.exp(g[h])).astype(BF16), sts[h].astype(BF16)) for h in hs]
    ivb = [z.astype(BF16) for z in ivs]
    lane = lax.broadcasted_iota(I32, (sub, c), 1)
    row = lax.broadcasted_iota(I32, (sub, c), 0)
    a_rows = [[] for _ in hs]
    for i in range(c // sub):
        lo = i * sub
        blk = slice(lo, lo + sub)
        a = [jnp.zeros((sub, c), F32) for _ in hs]
        for s in range(sub):
            for h in hs:
                gi = g[h][blk]
                e = jnp.exp(jnp.minimum(gi - gi[s:s + 1, :], 0.0))
                col = jnp.sum(qf[h][blk] * (kc[h][lo + s:lo + s + 1, :] * e), axis=-1, keepdims=True)
                a[h] = jnp.where(lane == lo + s, col, a[h])
        for h in hs:
            a_rows[h].append(jnp.where(row + lo >= lane, a[h], 0.0))
    a_tot = [z[0] if len(z) == 1 else jnp.concatenate(z, axis=0) for z in a_rows]
    rowc = lax.broadcasted_iota(I32, (c, 1), 0)
    half = sub
    for mask in masks:
        gref = []
        for h in hs:
            ref = g[h][half - 1:half, :]
            for b in range(1, c // (2 * half)):
                r0 = b * 2 * half + half - 1
                ref = jnp.where(rowc >= b * 2 * half, g[h][r0:r0 + 1, :], ref)
            gref.append(ref)
        ql = [(qf[h] * jnp.exp(jnp.minimum(g[h] - gref[h], 0.0))).astype(BF16) for h in hs]
        kl = [(kc[h] * jnp.exp(jnp.minimum(gref[h] - g[h], 0.0))).astype(BF16) for h in hs]
        a_tot = [a_tot[h] + _dot_nt(ql[h], kl[h]) * mask for h in hs]
        half *= 2
    o = [o_inter[h] + _dot(a_tot[h].astype(BF16), ivb[h]) for h in hs]
    gl = [g[h][c - 1:c, :] for h in hs]
    kd = [(kc[h] * jnp.exp(gl[h] - g[h])).astype(BF16) for h in hs]
    st_new = [sts[h] * jnp.exp(gl[h]) + _dot_tn(ivb[h], kd[h]) for h in hs]
    return o, st_new


def _hgrn_kernel(q_ref, f_ref, i_ref, g_ref, lbl_ref, gw_ref, tri_ref, s0_ref, o_ref, so_ref, st_ref,
                 *, nseq, tb, c, sub):
    j = pl.program_id(1)
    nt = pl.num_programs(1)

    @pl.when(j == 0)
    def _():
        for s in range(nseq):
            for h in range(HG_H):
                st_ref[s * HG_H + h] = s0_ref[s, h].T

    lbl = lbl_ref[...]
    ex = jnp.exp(lbl - jnp.max(lbl, axis=0, keepdims=True))
    lb = ex[0:1, :] / jnp.sum(ex, axis=0, keepdims=True)
    gw = gw_ref[...]
    tri = tri_ref[...]
    masks = _hgrn_masks(c, sub)
    n_chunks = tb // c

    def body(it, carry):
        s = it // n_chunks
        r0 = pl.multiple_of(it * c, c)
        rows = pl.ds(r0, c)
        hsl = [slice(h * HG_DK, (h + 1) * HG_DK) for h in range(HG_H)]
        o, st_new = _hgrn_chunk([q_ref[rows, z] for z in hsl], [f_ref[rows, z] for z in hsl],
                                [i_ref[rows, z] for z in hsl], [lb[:, z] for z in hsl],
                                [st_ref[s * HG_H + h] for h in range(HG_H)], tri, masks, c, sub)
        for h in range(HG_H):
            st_ref[s * HG_H + h] = st_new[h]
            on = o[h] * lax.rsqrt(jnp.mean(o[h] * o[h], axis=-1, keepdims=True) + NORM_EPS) * gw
            gin = g_ref[rows, hsl[h]]
            o_ref[rows, hsl[h]] = on * (gin * _sigmoid(gin))
        return carry

    lax.fori_loop(0, nseq * n_chunks, body, 0)

    @pl.when(j == nt - 1)
    def _():
        for s in range(nseq):
            for h in range(HG_H):
                so_ref[s, h] = st_ref[s * HG_H + h].T


def _hgrn(proj_hg, lb_logits, gn_w, s0, *, batch, seq, nseq, tb, c, sub):
    n = proj_hg.shape[0]
    rows = nseq * tb
    nt = seq // tb
    hw = HG_H * HG_DK
    tri = (lax.broadcasted_iota(I32, (c, c), 0) >= lax.broadcasted_iota(I32, (c, c), 1)).astype(BF16)

    def rmap(col):
        return lambda b, j: (b * nt + j, col)

    kern = functools.partial(_hgrn_kernel, nseq=nseq, tb=tb, c=c, sub=sub)
    in_specs = [pl.BlockSpec((rows, hw), rmap(0)), pl.BlockSpec((rows, hw), rmap(1)),
                pl.BlockSpec((rows, hw), rmap(2)), pl.BlockSpec((rows, hw), rmap(3)),
                pl.BlockSpec((lb_logits.shape[0], hw), lambda b, j: (0, 0)),
                pl.BlockSpec((1, HG_DV), lambda b, j: (0, 0)),
                pl.BlockSpec((c, c), lambda b, j: (0, 0)),
                pl.BlockSpec((nseq, HG_H, HG_DK, HG_DV), lambda b, j: (b, 0, 0, 0))]
    args = [proj_hg, proj_hg, proj_hg, proj_hg, lb_logits, gn_w, tri, s0]
    return pl.pallas_call(
        kern,
        grid=(batch // nseq, nt),
        in_specs=in_specs,
        out_specs=[pl.BlockSpec((rows, hw), lambda b, j: (b * nt + j, 0)),
                   pl.BlockSpec((nseq, HG_H, HG_DK, HG_DV), lambda b, j: (b, 0, 0, 0))],
        out_shape=[jax.ShapeDtypeStruct((n, HG_H * HG_DV), F32),
                   jax.ShapeDtypeStruct((batch, HG_H, HG_DK, HG_DV), F32)],
        scratch_shapes=[pltpu.VMEM((nseq * HG_H, HG_DV, HG_DK), F32)],
        compiler_params=_cparams(("parallel", "arbitrary")),
        name="hgrn2_mix",
    )(*args)


def _softplus(z):
    return jnp.maximum(z, 0.0) + jnp.log(1.0 + jnp.exp(-jnp.abs(z)))


def _rwkv_kernel(p_ref, prev_ref, s0_ref, mu_ref, w0_ref, a0_ref, wwa_ref, g2_ref, kkw_ref, kaw_ref, rk_ref,
                 lnw_ref, lnb_ref, g128_ref, trics_ref, esel_ref, o_ref, so_ref, st_ref, carry_ref,
                 *, c, chain, seq, rows):
    tt = RW_TILE
    ntile = rows // tt
    nc = tt // c
    j = pl.program_id(1)
    nt = pl.num_programs(1)
    lane128 = lax.broadcasted_iota(I32, (1, 128), 1)
    lo_half = lane128 < RW_N

    def block_diag(se, so):
        z = jnp.zeros((RW_N, RW_N), F32)
        return jnp.concatenate([jnp.concatenate([se, z], axis=1), jnp.concatenate([z, so], axis=1)], axis=0)

    def split2(z):
        hi = z.astype(BF16)
        return hi, (z - hi.astype(F32)).astype(BF16)

    p = p_ref[...]
    rolled = pltpu.roll(p, 1, 0)
    rowi = lax.broadcasted_iota(I32, (rows, 1), 0)
    if chain:
        @pl.when(j == 0)
        def _():
            carry_ref[...] = prev_ref[0]
            for pr in range(RW_PAIRS):
                st_ref[pr] = block_diag(s0_ref[0, 2 * pr], s0_ref[0, 2 * pr + 1])

        ps = jnp.where(rowi == 0, carry_ref[...], rolled)
        carry_ref[...] = p[rows - 1:rows, :]
    else:
        first = _dot_sel_l(esel_ref[...], prev_ref[...])
        ps = jnp.where((rowi & (seq - 1)) == 0, first, rolled)
    pm = p + (ps - p) * mu_ref[...]

    r = pm[:, 0:RW_W]
    k = pm[:, RW_W:2 * RW_W]
    v = pm[:, 2 * RW_W:3 * RW_W]
    wa = pm[:, 3 * RW_W:3 * RW_W + 128]
    g_lo = pm[:, 3 * RW_W + 128:3 * RW_W + 256]
    wa = jnp.where(lo_half, jnp.tanh(wa), wa)
    xwa = _dot(wa.astype(BF16), wwa_ref[...])
    log_w = -_softplus(-(w0_ref[...] + xwa[:, :RW_W])) - 0.5
    ld = -jnp.exp(log_w)
    a = _sigmoid(a0_ref[...] + xwa[:, RW_W:])
    gate = _dot(_sigmoid(g_lo).astype(BF16), g2_ref[...])
    g128 = g128_ref[...]

    def gsum(z):
        zs = jnp.concatenate([z[:, i * 128:(i + 1) * 128] for i in range(RW_PAIRS)], axis=0)
        hi, lo = split2(zs)
        s = _dot(hi, g128) + _dot(lo, g128)
        return jnp.concatenate([s[i * rows:(i + 1) * rows] for i in range(RW_PAIRS)], axis=1)

    kkv = k * kkw_ref[...]
    kk = kkv / jnp.maximum(jnp.sqrt(gsum(kkv * kkv)), 1e-12)
    k2 = k * (1.0 + (a - 1.0) * kaw_ref[...])
    beta = kk * a

    ld_hi, ld_lo = split2(ld)
    cs = _dot(trics_ref[...], ld_hi) + _dot(trics_ref[...], ld_lo)
    gc = cs[:rows]
    gtot = cs[rows:]
    e_g = jnp.exp(gc)
    e_ng = jnp.exp(-gc)
    e_l = jnp.exp(gtot - gc)
    ah = -kk * jnp.exp(gc - ld)
    rh = r * e_g
    bh = beta * e_ng
    kh = k2 * e_ng
    bt = beta * e_l
    kt = k2 * e_l
    e_tot = jnp.exp(gtot)

    ri = lax.broadcasted_iota(I32, (2 * tt, 2 * tt), 0)
    ci = lax.broadcasted_iota(I32, (2 * tt, 2 * tt), 1)
    same = (ri // c) == (ci // c)
    m_strict = jnp.where(same & (ci < ri), 1.0, 0.0)
    m_incl = jnp.where(same & (ci <= ri), 1.0, 0.0)
    eye2 = jnp.where(ri == ci, 1.0, 0.0)
    bd = jnp.where((ri // RW_N) == (ci // RW_N), 1.0, 0.0)

    def stack(z):
        return jnp.concatenate([jnp.where(lo_half, z, 0.0), jnp.where(lo_half, 0.0, z)], axis=0)

    def fold(z):
        return z[:tt] + z[tt:]

    rowc = lax.broadcasted_iota(I32, (tt, 1), 0) // c

    def chunk_expand(z):
        return jnp.concatenate([jnp.where(rowc == ch, z, 0.0) for ch in range(nc)], axis=1).astype(BF16)

    units = [(t, pr) for t in range(ntile) for pr in range(RW_PAIRS)]
    hw = 2 * tt

    def usl(z, u):
        return z[u[0] * tt:(u[0] + 1) * tt, u[1] * 128:(u[1] + 1) * 128]

    def bf(zs):
        return [z.astype(BF16) for z in zs]

    def rows2(a, b):
        return jnp.concatenate([a, b], axis=0)

    xr = [stack(usl(rh, u)) for u in units]
    xab, xrb, xbb, xkb, xvb = (bf([stack(usl(z, u)) for u in units]) for z in (ah, rh, bh, kh, v))
    lhs_ar = [rows2(a_, r_) for a_, r_ in zip(xab, xrb)]
    ab = [_dot_nt(l_, b_) for l_, b_ in zip(lhs_ar, xbb)]
    ak = [_dot_nt(l_, k_) for l_, k_ in zip(lhs_ar, xkb)]
    a_ab = [z[:hw] * m_strict for z in ab]
    a_rb = bf([z[hw:] * m_incl for z in ab])
    a_ak = bf([z[:hw] * m_strict for z in ak])
    a_rk = bf([z[hw:] * m_incl for z in ak])
    tm = [eye2 + z for z in a_ab]
    if c > 2:
        pw = [_dot(z, z) for z in bf(a_ab)]
    n = 2
    while n < c:
        pwb = bf(pw)
        if 2 * n < c:
            both = [_dot(rows2(t_.astype(BF16), p_), p_) for t_, p_ in zip(tm, pwb)]
            tm = [t_ + z[:hw] for t_, z in zip(tm, both)]
            pw = [z[hw:] for z in both]
        else:
            tm = [t_ + _dot(t_.astype(BF16), p_) for t_, p_ in zip(tm, pwb)]
        n *= 2
    tmb = bf(tm)
    akv = [_dot(rows2(p_, q_), x_) for p_, q_, x_ in zip(a_ak, a_rk, xvb)]
    tav = [_dot(t_, jnp.concatenate([x_, z[:hw].astype(BF16)], axis=1))
           for t_, x_, z in zip(tmb, xab, akv)]
    rbt = [_dot(r_, z) for r_, z in zip(a_rb, bf(tav))]
    rpb = bf([fold(x_ + z[:, :128]) for x_, z in zip(xr, rbt)])
    y0 = [fold(z[:, 128:] + w[hw:]) for z, w in zip(rbt, akv)]
    ap = bf([fold(z[:, :128]) for z in tav])
    vp = bf([fold(z[:, 128:]) for z in tav])
    bexp = [chunk_expand(usl(bt, u)) for u in units]
    kexp = [chunk_expand(usl(kt, u)) for u in units]
    vpl = bf([usl(v, u) for u in units])
    mt_all = [_dot_tn(a_, b_) for a_, b_ in zip(ap, bexp)]
    ht_all = [_dot_tn(rows2(p_, q_), rows2(b_, k_)) for p_, q_, b_, k_ in zip(vp, vpl, bexp, kexp)]

    sts = [st_ref[pr] for pr in range(RW_PAIRS)] if chain else None
    y_parts = [[] for _ in units]
    for t in range(ntile):
        for ch in range(nc):
            rs = slice(ch * c, (ch + 1) * c)
            cl = slice(ch * 128, (ch + 1) * 128)
            r0 = t * tt + ch * c
            for pr in range(RW_PAIRS):
                u = t * RW_PAIRS + pr
                mt = mt_all[u][:, cl] * bd + eye2 * e_tot[r0:r0 + 1, pr * 128:(pr + 1) * 128]
                ht = ht_all[u][:, cl] * bd
                if chain:
                    st = sts[pr]
                else:
                    sq = t * nc + ch
                    st = block_diag(s0_ref[sq, 2 * pr], s0_ref[sq, 2 * pr + 1])
                stb = st.astype(BF16)
                y_parts[u].append(_dot_nt(rpb[u][rs], stb) + y0[u][rs])
                st = _dot(stb, mt.astype(BF16)) + ht
                if chain:
                    sts[pr] = st
                else:
                    so_ref[sq, 2 * pr] = st[:RW_N, :RW_N]
                    so_ref[sq, 2 * pr + 1] = st[RW_N:, RW_N:]
    y_tiles = [jnp.concatenate([jnp.concatenate(y_parts[t * RW_PAIRS + pr], axis=0) for pr in range(RW_PAIRS)],
                               axis=1) for t in range(ntile)]
    y = y_tiles[0] if ntile == 1 else jnp.concatenate(y_tiles, axis=0)

    if chain:
        for pr in range(RW_PAIRS):
            st_ref[pr] = sts[pr]

        @pl.when(j == nt - 1)
        def _():
            for pr in range(RW_PAIRS):
                st = st_ref[pr]
                so_ref[0, 2 * pr] = st[:RW_N, :RW_N]
                so_ref[0, 2 * pr + 1] = st[RW_N:, RW_N:]

    inv_n = 1.0 / RW_N
    mean = gsum(y) * inv_n
    yc = y - mean
    var = gsum(yc * yc) * inv_n
    yn = yc * lax.rsqrt(var + RW_GN_EPS) * lnw_ref[...] + lnb_ref[...]
    bonus = gsum(r * k2 * rk_ref[...]) * v
    o_ref[...] = (yn + bonus) * gate


def _rwkv(proj_rw, prev, s0, wts, *, batch, seq, chain):
    n = proj_rw.shape[0]
    if chain:
        rows = _pick_tile(seq, RW_ROWS_CHAIN)
        nseq, c, nt = 1, RW_CHUNK, seq // rows
        prev_spec = pl.BlockSpec((1, 1, RW_COLS), lambda b, j: (b, 0, 0))
    else:
        assert seq & (seq - 1) == 0 and RW_TILE % seq == 0
        rows = _pick_tile(batch * seq, RW_ROWS_SHORT)
        nseq, c, nt = rows // seq, seq, 1
        prev_spec = pl.BlockSpec((nseq, RW_COLS), lambda b, j: (b, 0))
    ri = lax.broadcasted_iota(I32, (rows, rows), 0)
    ci = lax.broadcasted_iota(I32, (rows, rows), 1)
    same = (ri // c) == (ci // c)
    trics = jnp.concatenate([same & (ci <= ri), same], axis=0).astype(BF16)
    gi = lax.broadcasted_iota(I32, (128, 128), 0) // RW_N
    gj = lax.broadcasted_iota(I32, (128, 128), 1) // RW_N
    g128 = (gi == gj).astype(BF16)
    esel = (lax.broadcasted_iota(I32, (rows, nseq), 0) // seq
            == lax.broadcasted_iota(I32, (rows, nseq), 1)).astype(BF16)
    (mu, w0, a0, wwa, g2, kkw, kaw, rk, lnw, lnb) = wts

    def full(arr):
        return pl.BlockSpec(arr.shape, lambda b, j: (0,) * arr.ndim)

    consts = [mu, w0, a0, wwa, g2, kkw, kaw, rk, lnw, lnb, g128, trics, esel]
    in_specs = [pl.BlockSpec((rows, RW_COLS), lambda b, j: (b * nt + j, 0)),
                prev_spec,
                pl.BlockSpec((nseq, RW_H, RW_N, RW_N), lambda b, j: (b, 0, 0, 0))]
    in_specs += [full(t) for t in consts]
    args = [proj_rw, prev, s0] + consts
    kern = functools.partial(_rwkv_kernel, c=c, chain=chain, seq=seq, rows=rows)
    return pl.pallas_call(
        kern,
        grid=(batch // nseq, nt),
        in_specs=in_specs,
        out_specs=[pl.BlockSpec((rows, RW_W), lambda b, j: (b * nt + j, 0)),
                   pl.BlockSpec((nseq, RW_H, RW_N, RW_N), lambda b, j: (b, 0, 0, 0))],
        out_shape=[jax.ShapeDtypeStruct((n, RW_W), F32),
                   jax.ShapeDtypeStruct((batch, RW_H, RW_N, RW_N), F32)],
        scratch_shapes=[pltpu.VMEM((RW_PAIRS, 128, 128), F32), pltpu.VMEM((1, RW_COLS), F32)],
        compiler_params=_cparams(("parallel", "arbitrary")),
        name="rwkv7_mix",
    )(*args)


def _store_row_tiles(ref, x):
    m = x.shape[0]
    for s in range(D_MODEL // 128):
        ref[pl.ds(s, m, stride=8), :] = x[:, s * 128:(s + 1) * 128]


def _load_row_tiles(ref, m):
    return jnp.concatenate([ref[pl.ds(s, m, stride=8), :] for s in range(D_MODEL // 128)], axis=1)


def _outproj_router_kernel(ohg_ref, orw_ref, x_ref, wo_ref, gf_ref, wr_ref, br_ref, tri_ref, cin_ref,
                           x1_ref, hf_ref, idx_ref, gate_ref, pos_ref, cnt_ref, carry_ref):
    i = pl.program_id(0)

    @pl.when(i == 0)
    def _():
        carry_ref[...] = cin_ref[...].astype(F32)

    half = HG_H * HG_DV
    mixed = _dot(ohg_ref[...].astype(BF16), wo_ref[:half, :]) + _dot(orw_ref[...].astype(BF16), wo_ref[half:, :])
    x1 = x_ref[...] + mixed
    x1_ref[...] = x1
    hf = _rms(x1, gf_ref[...])
    _store_row_tiles(hf_ref, hf)

    h1, h2, _ = _split3(hf)
    w1, w2, _ = _split3(wr_ref[...])
    logits = _dot(h1, w1) + _dot(h1, w2) + _dot(h2, w1) + br_ref[...]

    tm = logits.shape[0]
    lane = lax.broadcasted_iota(I32, (tm, N_EXPERTS), 1).astype(F32)
    lane4 = lax.broadcasted_iota(I32, (tm, TOP_K), 1)
    vals = logits
    maskf = jnp.zeros((tm, N_EXPERTS), F32)
    sels, tops = [], []
    idx_out = jnp.zeros((tm, TOP_K), F32)
    for kq in range(TOP_K):
        m = jnp.max(vals, axis=-1, keepdims=True)
        idx = jnp.min(jnp.where(vals == m, lane, float(N_EXPERTS)), axis=-1, keepdims=True)
        sel = lane == idx
        sels.append(sel)
        tops.append(m)
        idx_out = jnp.where(lane4 == kq, idx, idx_out)
        vals = jnp.where(sel, -jnp.inf, vals)
        maskf = maskf + jnp.where(sel, 1.0, 0.0)
    es = [jnp.exp(t - tops[0]) for t in tops]
    denom = es[0] + es[1] + es[2] + es[3]
    gate_out = jnp.zeros((tm, TOP_K), F32)
    for kq in range(TOP_K):
        gate_out = jnp.where(lane4 == kq, es[kq] / denom, gate_out)

    pos = _dot(tri_ref[...], maskf.astype(BF16)) + carry_ref[...]
    pos_out = jnp.zeros((tm, TOP_K), F32)
    for kq in range(TOP_K):
        pk = jnp.sum(jnp.where(sels[kq], pos, 0.0), axis=-1, keepdims=True)
        pos_out = jnp.where(lane4 == kq, pk, pos_out)
    carry_ref[...] = carry_ref[...] + jnp.sum(maskf, axis=0, keepdims=True)
    idx_ref[...] = idx_out.astype(I32)
    gate_ref[...] = gate_out
    pos_ref[...] = pos_out.astype(I32)
    cnt_ref[...] = carry_ref[...].astype(I32)


def _outproj_router(o_hg, o_rw, x, wo_bf16, gf, w_router, b_router, cnt_in, tm):
    n = x.shape[0]
    tri = (lax.broadcasted_iota(I32, (tm, tm), 0) > lax.broadcasted_iota(I32, (tm, tm), 1)).astype(BF16)
    half = HG_H * HG_DV
    row = lambda i: (i, 0)
    fix = lambda i: (0, 0)
    return pl.pallas_call(
        _outproj_router_kernel,
        grid=(n // tm,),
        in_specs=[pl.BlockSpec((tm, half), row), pl.BlockSpec((tm, RW_W), row), pl.BlockSpec((tm, D_MODEL), row),
                  pl.BlockSpec((half + RW_W, D_MODEL), fix), pl.BlockSpec((1, D_MODEL), fix),
                  pl.BlockSpec((D_MODEL, N_EXPERTS), fix), pl.BlockSpec((1, N_EXPERTS), fix),
                  pl.BlockSpec((tm, tm), fix), pl.BlockSpec((1, N_EXPERTS), fix)],
        out_specs=[pl.BlockSpec((tm, D_MODEL), row), pl.BlockSpec((8 * tm, 128), row),
                   pl.BlockSpec((tm, TOP_K), row), pl.BlockSpec((tm, TOP_K), row), pl.BlockSpec((tm, TOP_K), row),
                   pl.BlockSpec((1, N_EXPERTS), fix)],
        out_shape=[jax.ShapeDtypeStruct((n, D_MODEL), F32), jax.ShapeDtypeStruct((8 * n, 128), F32),
                   jax.ShapeDtypeStruct((n, TOP_K), I32), jax.ShapeDtypeStruct((n, TOP_K), F32),
                   jax.ShapeDtypeStruct((n, TOP_K), I32), jax.ShapeDtypeStruct((1, N_EXPERTS), I32)],
        scratch_shapes=[pltpu.VMEM((1, N_EXPERTS), F32)],
        compiler_params=_cparams(("arbitrary",)),
        name="out_proj_router",
    )(o_hg, o_rw, x, wo_bf16, gf, w_router, b_router, tri, cnt_in)


def _plan_kernel(cnt_ref, idx_ref, pos_ref, dest_ref, bexp_ref, nblk_ref, *, n_blocks):
    cnt = cnt_ref[...].astype(F32)
    padded = jnp.floor((cnt + (MOE_BLOCK - 1)) * (1.0 / MOE_BLOCK)) * MOE_BLOCK
    ei = lax.broadcasted_iota(I32, (N_EXPERTS, N_EXPERTS), 0)
    ej = lax.broadcasted_iota(I32, (N_EXPERTS, N_EXPERTS), 1)
    upper = (ei < ej).astype(BF16)
    upper_incl = (ei <= ej).astype(BF16)
    start = _dot_sel_r(padded, upper)
    ends = _dot_sel_r(padded, upper_incl)
    idx = idx_ref[...]
    dest = pos_ref[...]
    for e in range(N_EXPERTS):
        dest = dest + jnp.where(idx == e, start[:, e:e + 1].astype(I32), 0)
    dest_ref[...] = dest
    bstart = (lax.broadcasted_iota(I32, (n_blocks, 1), 0) * MOE_BLOCK).astype(F32)
    be = jnp.sum(jnp.where(ends <= bstart, 1.0, 0.0), axis=-1, keepdims=True)
    bexp_ref[...] = jnp.minimum(be, N_EXPERTS - 1.0).astype(I32)
    nblk_ref[...] = (ends[:, N_EXPERTS - 1:] * (1.0 / MOE_BLOCK)).astype(I32)


def _plan(cnt, idx, pos, n_blocks, tm):
    n = idx.shape[0]
    row = lambda i: (i, 0)
    fix = lambda i: (0, 0)
    return pl.pallas_call(
        functools.partial(_plan_kernel, n_blocks=n_blocks),
        grid=(n // tm,),
        in_specs=[pl.BlockSpec((1, N_EXPERTS), fix), pl.BlockSpec((tm, TOP_K), row), pl.BlockSpec((tm, TOP_K), row)],
        out_specs=[pl.BlockSpec((tm, TOP_K), row), pl.BlockSpec((n_blocks, 1), fix), pl.BlockSpec((1, 1), fix)],
        out_shape=[jax.ShapeDtypeStruct((n, TOP_K), I32), jax.ShapeDtypeStruct((n_blocks, 1), I32),
                   jax.ShapeDtypeStruct((1, 1), I32)],
        compiler_params=_cparams(("arbitrary",)),
        name="moe_plan",
    )(cnt, idx, pos)


def _row_tile(ref, r):
    return ref.at[pl.ds(pl.multiple_of(r * 8, 8), 8)]


def _scatter_kernel(cnt_ref, nblk_ref, dest_a_ref, dest_b_ref, hf_a_ref, hf_b_ref, xs_ref, zbuf, sem, sem_pad, sem_blk,
                    *, ts, nt_a, n_blocks):
    i = pl.program_id(0)
    blk_rows = 8 * MOE_BLOCK
    shift = MOE_BLOCK.bit_length() - 1

    @pl.when(i == 0)
    def _():
        zbuf[...] = jnp.zeros_like(zbuf)

        def pad_expert(e, start):
            c = cnt_ref[e]
            padded = ((c + (MOE_BLOCK - 1)) >> shift) << shift

            def zrow(r, carry):
                pltpu.make_async_copy(_row_tile(zbuf, 0), _row_tile(xs_ref, start + r), sem_pad).start()
                return carry

            def wrow(r, carry):
                pltpu.make_async_copy(_row_tile(zbuf, 0), _row_tile(xs_ref, 0), sem_pad).wait()
                return carry

            lax.fori_loop(c, padded, zrow, 0)
            lax.fori_loop(c, padded, wrow, 0)
            return start + padded

        lax.fori_loop(0, N_EXPERTS, pad_expert, 0)

        def zblk(b, carry):
            pltpu.make_async_copy(zbuf, xs_ref.at[pl.ds(pl.multiple_of(b * blk_rows, blk_rows), blk_rows)],
                                  sem_blk).start()
            return carry

        def wblk(b, carry):
            pltpu.make_async_copy(zbuf, xs_ref.at[pl.ds(0, blk_rows)], sem_blk).wait()
            return carry

        lax.fori_loop(nblk_ref[0], n_blocks, zblk, 0)
        lax.fori_loop(nblk_ref[0], n_blocks, wblk, 0)

    def scatter_tile(dest_ref, hf_ref):
        def issue(t, carry):
            for kq in range(TOP_K):
                d = dest_ref[t * TOP_K + kq]
                pltpu.make_async_copy(_row_tile(hf_ref, t), _row_tile(xs_ref, d), sem).start()
            return carry

        def drain(t, carry):
            for kq in range(TOP_K):
                pltpu.make_async_copy(_row_tile(hf_ref, 0), _row_tile(xs_ref, 0), sem).wait()
            return carry

        lax.fori_loop(0, ts, issue, 0)
        lax.fori_loop(0, ts, drain, 0)

    @pl.when(i < nt_a)
    def _():
        scatter_tile(dest_a_ref, hf_a_ref)

    @pl.when(i >= nt_a)
    def _():
        scatter_tile(dest_b_ref, hf_b_ref)


def _scatter(cnt, nblk, dest_a, dest_b, hf_a, hf_b, n_blocks, ts):
    assert MOE_BLOCK & (MOE_BLOCK - 1) == 0
    nt_a = hf_a.shape[0] // (8 * ts)
    nt_b = hf_b.shape[0] // (8 * ts)
    amap = lambda i: (jnp.minimum(i, nt_a - 1),)
    bmap = lambda i: (jnp.maximum(i - nt_a, 0),)
    smem = pl.BlockSpec(memory_space=pltpu.SMEM)
    return pl.pallas_call(
        functools.partial(_scatter_kernel, ts=ts, nt_a=nt_a, n_blocks=n_blocks),
        grid=(nt_a + nt_b,),
        in_specs=[smem, smem,
                  pl.BlockSpec((ts * TOP_K,), amap, memory_space=pltpu.SMEM),
                  pl.BlockSpec((ts * TOP_K,), bmap, memory_space=pltpu.SMEM),
                  pl.BlockSpec((8 * ts, 128), lambda i: (jnp.minimum(i, nt_a - 1), 0)),
                  pl.BlockSpec((8 * ts, 128), lambda i: (jnp.maximum(i - nt_a, 0), 0))],
        out_specs=pl.BlockSpec(memory_space=pl.ANY),
        out_shape=jax.ShapeDtypeStruct((8 * n_blocks * MOE_BLOCK, 128), F32),
        scratch_shapes=[pltpu.VMEM((8 * MOE_BLOCK, 128), F32), pltpu.SemaphoreType.DMA(()),
                        pltpu.SemaphoreType.DMA(()), pltpu.SemaphoreType.DMA(())],
        compiler_params=_cparams(("arbitrary",)),
        name="moe_scatter",
    )(cnt, nblk, dest_a, dest_b, hf_a, hf_b)


def _ffn_kernel(bexp_ref, nblk_ref, xs_ref, wgu_ref, bgu_ref, wd_ref, bd_ref, yb_ref, wgu_bf, wd_bf):
    b = pl.program_id(0)
    valid = b < nblk_ref[0]
    prev_e = bexp_ref[jnp.maximum(b, 1) - 1]
    fresh = (b == 0) | (bexp_ref[b] != prev_e)

    @pl.when(valid & fresh)
    def _():
        wgu_bf[...] = wgu_ref[0].astype(BF16)
        wd_bf[...] = wd_ref[0].astype(BF16)

    @pl.when(valid)
    def _():
        x = _load_row_tiles(xs_ref, MOE_BLOCK).astype(BF16)
        gu = _dot(x, wgu_bf[...]) + bgu_ref[0]
        gate = jnp.minimum(gu[:, :D_FF], SWIGLU_LIMIT)
        up = jnp.clip(gu[:, D_FF:], -SWIGLU_LIMIT, SWIGLU_LIMIT)
        glu = gate * _sigmoid(SWIGLU_ALPHA * gate)
        act = ((up + 1.0) * glu).astype(BF16)
        _store_row_tiles(yb_ref, _dot(act, wd_bf[...]) + bd_ref[0])

    @pl.when(jnp.logical_not(valid))
    def _():
        yb_ref[...] = jnp.zeros_like(yb_ref)


def _ffn(bexp, nblk, xs, w_gate_up, b_gate_up, w_down, b_down):
    n_rows = xs.shape[0] // 8
    n_blocks = n_rows // MOE_BLOCK

    def blk(b, be, nb):
        return (jnp.minimum(b, jnp.maximum(nb[0], 1) - 1), 0)

    def wmap(b, be, nb):
        return (be[jnp.minimum(b, jnp.maximum(nb[0], 1) - 1)], 0, 0)

    grid_spec = pltpu.PrefetchScalarGridSpec(
        num_scalar_prefetch=2,
        grid=(n_blocks,),
        in_specs=[pl.BlockSpec((8 * MOE_BLOCK, 128), blk),
                  pl.BlockSpec((1, D_MODEL, 2 * D_FF), wmap),
                  pl.BlockSpec((1, 1, 2 * D_FF), wmap),
                  pl.BlockSpec((1, D_FF, D_MODEL), wmap),
                  pl.BlockSpec((1, 1, D_MODEL), wmap)],
        out_specs=pl.BlockSpec((8 * MOE_BLOCK, 128), lambda b, be, nb: (b, 0)),
        scratch_shapes=[pltpu.VMEM((D_MODEL, 2 * D_FF), BF16), pltpu.VMEM((D_FF, D_MODEL), BF16)],
    )
    return pl.pallas_call(
        _ffn_kernel,
        grid_spec=grid_spec,
        out_shape=jax.ShapeDtypeStruct((8 * n_rows, 128), F32),
        compiler_params=_cparams(("arbitrary",)),
        name="moe_ffn",
    )(bexp, nblk, xs, w_gate_up, b_gate_up, w_down, b_down)


def _combine_kernel(dest_ref, dest_nx_ref, gate_ref, x1_ref, gn_ref, yb_ref, y_ref, buf, sems, *, tc):
    i = pl.program_id(0)

    def fetch(d_ref, off, s):
        def issue(t, carry):
            for kq in range(TOP_K):
                d = d_ref[off + t * TOP_K + kq]
                pltpu.make_async_copy(_row_tile(yb_ref, d), _row_tile(buf.at[s, kq], t), sems.at[s]).start()
            return carry

        lax.fori_loop(0, tc, issue, 0)

    def drain(s):
        def wait(t, carry):
            for kq in range(TOP_K):
                pltpu.make_async_copy(_row_tile(yb_ref, 0), _row_tile(buf.at[s, 0], 0), sems.at[s]).wait()
            return carry

        lax.fori_loop(0, tc, wait, 0)

    def mix(s, rows):
        gate = gate_ref[rows, :]
        acc = x1_ref[rows, :]
        for kq in range(TOP_K):
            acc = acc + gate[:, kq:kq + 1] * _load_row_tiles(buf.at[s, kq], tc)
        y_ref[rows, :] = _rms(acc, gn_ref[...])

    @pl.when(i == 0)
    def _():
        fetch(dest_ref, 0, 0)

    fetch(dest_ref, tc * TOP_K, 1)
    drain(0)
    mix(0, slice(0, tc))

    @pl.when(i + 1 < pl.num_programs(0))
    def _():
        fetch(dest_nx_ref, 0, 0)

    drain(1)
    mix(1, slice(tc, 2 * tc))


def _combine(dest_flat, gate, x1, gn, yb, tc):
    n = x1.shape[0]
    ns = n // (2 * tc)
    return pl.pallas_call(
        functools.partial(_combine_kernel, tc=tc),
        grid=(ns,),
        in_specs=[pl.BlockSpec((2 * tc * TOP_K,), lambda i: (i,), memory_space=pltpu.SMEM),
                  pl.BlockSpec((2 * tc * TOP_K,), lambda i: (jnp.minimum(i + 1, ns - 1),), memory_space=pltpu.SMEM),
                  pl.BlockSpec((2 * tc, TOP_K), lambda i: (i, 0)),
                  pl.BlockSpec((2 * tc, D_MODEL), lambda i: (i, 0)),
                  pl.BlockSpec((1, D_MODEL), lambda i: (0, 0)),
                  pl.BlockSpec(memory_space=pl.ANY)],
        out_specs=pl.BlockSpec((2 * tc, D_MODEL), lambda i: (i, 0)),
        out_shape=jax.ShapeDtypeStruct((n, D_MODEL), F32),
        scratch_shapes=[pltpu.VMEM((2, TOP_K, 8 * tc, 128), F32), pltpu.SemaphoreType.DMA((2,))],
        compiler_params=_cparams(("arbitrary",)),
        name="moe_combine",
    )(dest_flat, dest_flat, gate, x1, gn, yb)


def _pick_tile(n, pref):
    t = pref
    while n % t:
        t //= 2
    return t


def kernel(x_prompt, x_sample, state_hgrn, state_rwkv, state_shift, norm_mix, w_in, lb_logits, hg_norm_w, rw_mu,
           rw_w0, rw_w2, rw_a0, rw_a2, rw_g2, rw_k_k, rw_k_a, rw_r_k, rw_lnx_w, rw_lnx_b, w_out, norm_ffn,
           w_router, b_router, w_gate_up, b_gate_up, w_down, b_down, norm_final):
    bp, tp, d = x_prompt.shape
    bs, tsq, _ = x_sample.shape
    assert norm_mix.shape[0] == 1 and d == D_MODEL
    n_p, n_s = bp * tp, bs * tsq
    n = n_p + n_s
    xp = x_prompt.reshape(n_p, d)
    xs_ = x_sample.reshape(n_s, d)

    w_in_b = w_in[0].astype(BF16)
    splits = (HG_COLS, RW_COLS)
    hgp_p, rwp_p = _proj(xp, norm_mix, w_in_b, splits, True, _pick_tile(n_p, 512))
    hgp_s, rwp_s = _proj(xs_, norm_mix, w_in_b, splits, True, _pick_tile(n_s, 256))
    ones_d = jnp.ones((1, d), F32)
    (prev_s,) = _proj(state_shift[0], ones_d, w_in_b[:, HG_COLS:], (RW_COLS,), False, _pick_tile(bs, 128))
    prev_p = jnp.zeros((bp, 1, RW_COLS), F32)
    x_last = jnp.concatenate([x_prompt[:, -1, :], x_sample[:, -1, :]], axis=0)
    shift = _rmsnorm_rows(x_last, norm_mix)

    zero_hg = jnp.zeros((bp, HG_H, HG_DK, HG_DV), F32)
    ohg_p, hg_p = _hgrn(hgp_p, lb_logits, hg_norm_w, zero_hg, batch=bp, seq=tp,
                        nseq=1, tb=_pick_tile(tp, 256), c=HG_CHUNK, sub=HG_SUB)
    ohg_s, hg_s = _hgrn(hgp_s, lb_logits, hg_norm_w, state_hgrn[0], batch=bs, seq=tsq,
                        nseq=_pick_tile(bs, 16), tb=tsq, c=tsq, sub=tsq)

    zpad = jnp.zeros((64, RW_W), F32)
    wwa = jnp.concatenate([jnp.concatenate([rw_w2[0], zpad], axis=1),
                           jnp.concatenate([zpad, rw_a2[0]], axis=1)], axis=0).astype(BF16)
    wts = (rw_mu, rw_w0, rw_a0, wwa, rw_g2[0].astype(BF16), rw_k_k, rw_k_a, rw_r_k.reshape(1, RW_W),
           rw_lnx_w, rw_lnx_b)
    zero_rw = jnp.zeros((bp, RW_H, RW_N, RW_N), F32)
    orw_p, rw_p = _rwkv(rwp_p, prev_p, zero_rw, wts, batch=bp, seq=tp, chain=True)
    orw_s, rw_s = _rwkv(rwp_s, prev_s, state_rwkv[0], wts, batch=bs, seq=tsq, chain=False)

    wo_b = w_out[0].astype(BF16)
    tm_p, tm_s = _pick_tile(n_p, 512), _pick_tile(n_s, 512)
    cnt0 = jnp.zeros((1, N_EXPERTS), I32)
    x1_p, hf_p, idx_p, gate_p, pos_p, cnt_p = _outproj_router(ohg_p, orw_p, xp, wo_b, norm_ffn, w_router[0],
                                                              b_router, cnt0, tm_p)
    x1_s, hf_s, idx_s, gate_s, pos_s, cnt = _outproj_router(ohg_s, orw_s, xs_, wo_b, norm_ffn, w_router[0],
                                                            b_router, cnt_p, tm_s)

    n_blocks = -(-(n * TOP_K) // MOE_BLOCK) + N_EXPERTS
    dest_p, bexp, nblk = _plan(cnt, idx_p, pos_p, n_blocks, tm_p)
    dest_s, _, _ = _plan(cnt, idx_s, pos_s, n_blocks, tm_s)
    dest_p = dest_p.reshape(n_p * TOP_K)
    dest_s = dest_s.reshape(n_s * TOP_K)
    nblk = nblk.reshape(1)
    ts = _pick_tile(math.gcd(n_p, n_s), 512)
    xs = _scatter(cnt.reshape(N_EXPERTS), nblk, dest_p, dest_s, hf_p, hf_s, n_blocks, ts)
    yb = _ffn(bexp.reshape(n_blocks), nblk, xs, w_gate_up[0], b_gate_up[0].reshape(N_EXPERTS, 1, 2 * D_FF),
              w_down[0], b_down[0].reshape(N_EXPERTS, 1, d))
    gn = norm_final.reshape(1, d)
    y_p = _combine(dest_p, gate_p, x1_p, gn, yb, _pick_tile(n_p // 2, 256))
    y_s = _combine(dest_s, gate_s, x1_s, gn, yb, _pick_tile(n_s // 2, 256))

    return (y_p.reshape(bp, tp, d), y_s.reshape(bs, tsq, d), hg_p[None], rw_p[None], shift[:bp][None],
            hg_s[None], rw_s[None], shift[bp:][None])
```

```python
import functools
import math

import jax
import jax.numpy as jnp
from jax import lax
from jax.experimental import pallas as pl
from jax.experimental.pallas import tpu as pltpu

F32 = jnp.float32
BF16 = jnp.bfloat16
I32 = jnp.int32

D_MODEL = 1024
HG_H, HG_DK, HG_DV = 4, 128, 128
HG_CHUNK = 64
HG_SUB = 8
RW_H, RW_N = 8, 64
RW_W = RW_H * RW_N
RW_PAIRS = RW_H // 2
RW_TILE = 64
RW_CHUNK = 32
RW_ROWS_CHAIN = 512
RW_ROWS_SHORT = 128
HG_COLS = 2 * HG_H * HG_DK + 2 * HG_H * HG_DV
RW_COLS = 3 * RW_W + 64 + 64 + 128
N_EXPERTS = 32
TOP_K = 4
D_FF = D_MODEL
SWIGLU_LIMIT = 7.0
SWIGLU_ALPHA = 1.702
NORM_EPS = 1e-6
RW_GN_EPS = 64e-5
MOE_BLOCK = 256
VMEM_LIMIT = 56 * 1024 * 1024


def _dot(a, b):
    return jnp.dot(a, b, preferred_element_type=F32)


def _dot_nt(a, b):
    return lax.dot_general(a, b, (((1,), (1,)), ((), ())), preferred_element_type=F32)


def _dot_tn(a, b):
    return lax.dot_general(a, b, (((0,), (0,)), ((), ())), preferred_element_type=F32)


def _split3(x):
    x1 = x.astype(BF16)
    r1 = x - x1.astype(F32)
    x2 = r1.astype(BF16)
    x3 = (r1 - x2.astype(F32)).astype(BF16)
    return x1, x2, x3


def _dot_sel_l(m_bf16, x):
    x1, x2, x3 = _split3(x)
    return _dot(m_bf16, x1) + _dot(m_bf16, x2) + _dot(m_bf16, x3)


def _dot_sel_r(x, m_bf16):
    x1, x2, x3 = _split3(x)
    return _dot(x1, m_bf16) + _dot(x2, m_bf16) + _dot(x3, m_bf16)


def _sigmoid(x):
    return 1.0 / (1.0 + jnp.exp(-x))


def _rms(x, g):
    return x * lax.rsqrt(jnp.mean(x * x, axis=-1, keepdims=True) + NORM_EPS) * g


def _cparams(sem):
    return pltpu.CompilerParams(dimension_semantics=sem, vmem_limit_bytes=VMEM_LIMIT)


def _proj_kernel(x_ref, g_ref, w_ref, *o_refs, normalize, splits):
    x = x_ref[...]
    h = _rms(x, g_ref[...]) if normalize else x
    hb = h.astype(BF16)
    c0 = 0
    for o_ref, width in zip(o_refs, splits):
        o_ref[...] = _dot(hb, w_ref[:, c0:c0 + width])
        c0 += width


def _proj(x, g, w_bf16, splits, normalize, tm):
    n = x.shape[0]
    kern = functools.partial(_proj_kernel, normalize=normalize, splits=splits)
    return pl.pallas_call(
        kern,
        grid=(n // tm,),
        in_specs=[pl.BlockSpec((tm, D_MODEL), lambda i: (i, 0)),
                  pl.BlockSpec((1, D_MODEL), lambda i: (0, 0)),
                  pl.BlockSpec((D_MODEL, sum(splits)), lambda i: (0, 0))],
        out_specs=[pl.BlockSpec((tm, s), lambda i: (i, 0)) for s in splits],
        out_shape=[jax.ShapeDtypeStruct((n, s), F32) for s in splits],
        compiler_params=_cparams(("parallel",)),
        name="norm_in_proj" if normalize else "shift_proj",
    )(x, g, w_bf16)


def _rmsnorm_rows_kernel(x_ref, g_ref, o_ref):
    o_ref[...] = _rms(x_ref[...], g_ref[...])


def _rmsnorm_rows(x, g):
    return pl.pallas_call(
        _rmsnorm_rows_kernel,
        out_shape=jax.ShapeDtypeStruct(x.shape, F32),
        name="shift_norm",
    )(x, g)


def _hgrn_masks(c, sub):
    ri = lax.broadcasted_iota(I32, (c, c), 0)
    ci = lax.broadcasted_iota(I32, (c, c), 1)
    masks = []
    h = sub
    while h < c:
        keep = ((ri // (2 * h)) == (ci // (2 * h))) & ((ri % (2 * h)) >= h) & ((ci % (2 * h)) < h)
        masks.append(keep.astype(F32))
        h *= 2
    return masks


def _hgrn_chunk(qs, fps, ivs, lbs, sts, tri, masks, c, sub):
    nh = len(qs)
    hs = range(nh)
    qf = [q * _sigmoid(q) for q in qs]
    f = [lb + (1.0 - lb) * _sigmoid(fp) for lb, fp in zip(lbs, fps)]
    lf = [jnp.log(z) for z in f]
    kc = [1.0 - z for z in f]
    g = [_dot_sel_l(tri, z) for z in lf]
    o_inter = [_dot_nt((qf[h] * jnp.exp(g[h])).astype(BF16), sts[h].astype(BF16)) for h in hs]
    ivb = [z.astype(BF16) for z in ivs]
    lane = lax.broadcasted_iota(I32, (sub, c), 1)
    row = lax.broadcasted_iota(I32, (sub, c), 0)
    a_rows = [[] for _ in hs]
    for i in range(c // sub):
        lo = i * sub
        blk = slice(lo, lo + sub)
        a = [jnp.zeros((sub, c), F32) for _ in hs]
        for s in range(sub):
            for h in hs:
                gi = g[h][blk]
                e = jnp.exp(jnp.minimum(gi - gi[s:s + 1, :], 0.0))
                col = jnp.sum(qf[h][blk] * (kc[h][lo + s:lo + s + 1, :] * e), axis=-1, keepdims=True)
                a[h] = jnp.where(lane == lo + s, col, a[h])
        for h in hs:
            a_rows[h].append(jnp.where(row + lo >= lane, a[h], 0.0))
    a_tot = [z[0] if len(z) == 1 else jnp.concatenate(z, axis=0) for z in a_rows]
    rowc = lax.broadcasted_iota(I32, (c, 1), 0)
    half = sub
    for mask in masks:
        gref = []
        for h in hs:
            ref = g[h][half - 1:half, :]
            for b in range(1, c // (2 * half)):
                r0 = b * 2 * half + half - 1
                ref = jnp.where(rowc >= b * 2 * half, g[h][r0:r0 + 1, :], ref)
            gref.append(ref)
        ql = [(qf[h] * jnp.exp(jnp.minimum(g[h] - gref[h], 0.0))).astype(BF16) for h in hs]
        kl = [(kc[h] * jnp.exp(jnp.minimum(gref[h] - g[h], 0.0))).astype(BF16) for h in hs]
        a_tot = [a_tot[h] + _dot_nt(ql[h], kl[h]) * mask for h in hs]
        half *= 2
    o = [o_inter[h] + _dot(a_tot[h].astype(BF16), ivb[h]) for h in hs]
    gl = [g[h][c - 1:c, :] for h in hs]
    kd = [(kc[h] * jnp.exp(gl[h] - g[h])).astype(BF16) for h in hs]
    st_new = [sts[h] * jnp.exp(gl[h]) + _dot_tn(ivb[h], kd[h]) for h in hs]
    return o, st_new


def _hgrn_kernel(q_ref, f_ref, i_ref, g_ref, lbl_ref, gw_ref, tri_ref, s0_ref, o_ref, so_ref, st_ref,
                 *, nseq, tb, c, sub):
    j = pl.program_id(1)
    nt = pl.num_programs(1)

    @pl.when(j == 0)
    def _():
        for s in range(nseq):
            for h in range(HG_H):
                st_ref[s * HG_H + h] = s0_ref[s, h].T

    lbl = lbl_ref[...]
    ex = jnp.exp(lbl - jnp.max(lbl, axis=0, keepdims=True))
    lb = ex[0:1, :] / jnp.sum(ex, axis=0, keepdims=True)
    gw = gw_ref[...]
    tri = tri_ref[...]
    masks = _hgrn_masks(c, sub)
    n_chunks = tb // c

    def body(it, carry):
        s = it // n_chunks
        r0 = pl.multiple_of(it * c, c)
        rows = pl.ds(r0, c)
        hsl = [slice(h * HG_DK, (h + 1) * HG_DK) for h in range(HG_H)]
        o, st_new = _hgrn_chunk([q_ref[rows, z] for z in hsl], [f_ref[rows, z] for z in hsl],
                                [i_ref[rows, z] for z in hsl], [lb[:, z] for z in hsl],
                                [st_ref[s * HG_H + h] for h in range(HG_H)], tri, masks, c, sub)
        for h in range(HG_H):
            st_ref[s * HG_H + h] = st_new[h]
            on = o[h] * lax.rsqrt(jnp.mean(o[h] * o[h], axis=-1, keepdims=True) + NORM_EPS) * gw
            gin = g_ref[rows, hsl[h]]
            o_ref[rows, hsl[h]] = on * (gin * _sigmoid(gin))
        return carry

    lax.fori_loop(0, nseq * n_chunks, body, 0)

    @pl.when(j == nt - 1)
    def _():
        for s in range(nseq):
            for h in range(HG_H):
                so_ref[s, h] = st_ref[s * HG_H + h].T


def _hgrn(proj_hg, lb_logits, gn_w, s0, *, batch, seq, nseq, tb, c, sub):
    n = proj_hg.shape[0]
    rows = nseq * tb
    nt = seq // tb
    hw = HG_H * HG_DK
    tri = (lax.broadcasted_iota(I32, (c, c), 0) >= lax.broadcasted_iota(I32, (c, c), 1)).astype(BF16)

    def rmap(col):
        return lambda b, j: (b * nt + j, col)

    kern = functools.partial(_hgrn_kernel, nseq=nseq, tb=tb, c=c, sub=sub)
    in_specs = [pl.BlockSpec((rows, hw), rmap(0)), pl.BlockSpec((rows, hw), rmap(1)),
                pl.BlockSpec((rows, hw), rmap(2)), pl.BlockSpec((rows, hw), rmap(3)),
                pl.BlockSpec((lb_logits.shape[0], hw), lambda b, j: (0, 0)),
                pl.BlockSpec((1, HG_DV), lambda b, j: (0, 0)),
                pl.BlockSpec((c, c), lambda b, j: (0, 0)),
                pl.BlockSpec((nseq, HG_H, HG_DK, HG_DV), lambda b, j: (b, 0, 0, 0))]
    args = [proj_hg, proj_hg, proj_hg, proj_hg, lb_logits, gn_w, tri, s0]
    return pl.pallas_call(
        kern,
        grid=(batch // nseq, nt),
        in_specs=in_specs,
        out_specs=[pl.BlockSpec((rows, hw), lambda b, j: (b * nt + j, 0)),
                   pl.BlockSpec((nseq, HG_H, HG_DK, HG_DV), lambda b, j: (b, 0, 0, 0))],
        out_shape=[jax.ShapeDtypeStruct((n, HG_H * HG_DV), F32),
                   jax.ShapeDtypeStruct((batch, HG_H, HG_DK, HG_DV), F32)],
        scratch_shapes=[pltpu.VMEM((nseq * HG_H, HG_DV, HG_DK), F32)],
        compiler_params=_cparams(("parallel", "arbitrary")),
        name="hgrn2_mix",
    )(*args)


def _softplus(z):
    return jnp.maximum(z, 0.0) + jnp.log(1.0 + jnp.exp(-jnp.abs(z)))


def _rwkv_kernel(p_ref, prev_ref, s0_ref, mu_ref, w0_ref, a0_ref, wwa_ref, g2_ref, kkw_ref, kaw_ref, rk_ref,
                 lnw_ref, lnb_ref, g128_ref, trics_ref, esel_ref, o_ref, so_ref, st_ref, carry_ref,
                 *, c, chain, seq, rows):
    tt = RW_TILE
    ntile = rows // tt
    nc = tt // c
    j = pl.program_id(1)
    nt = pl.num_programs(1)
    lane128 = lax.broadcasted_iota(I32, (1, 128), 1)
    lo_half = lane128 < RW_N

    def block_diag(se, so):
        z = jnp.zeros((RW_N, RW_N), F32)
        return jnp.concatenate([jnp.concatenate([se, z], axis=1), jnp.concatenate([z, so], axis=1)], axis=0)

    def split2(z):
        hi = z.astype(BF16)
        return hi, (z - hi.astype(F32)).astype(BF16)

    p = p_ref[...]
    rolled = pltpu.roll(p, 1, 0)
    rowi = lax.broadcasted_iota(I32, (rows, 1), 0)
    if chain:
        @pl.when(j == 0)
        def _():
            carry_ref[...] = prev_ref[0]
            for pr in range(RW_PAIRS):
                st_ref[pr] = block_diag(s0_ref[0, 2 * pr], s0_ref[0, 2 * pr + 1])

        ps = jnp.where(rowi == 0, carry_ref[...], rolled)
        carry_ref[...] = p[rows - 1:rows, :]
    else:
        first = _dot_sel_l(esel_ref[...], prev_ref[...])
        ps = jnp.where((rowi & (seq - 1)) == 0, first, rolled)
    pm = p + (ps - p) * mu_ref[...]

    r = pm[:, 0:RW_W]
    k = pm[:, RW_W:2 * RW_W]
    v = pm[:, 2 * RW_W:3 * RW_W]
    wa = pm[:, 3 * RW_W:3 * RW_W + 128]
    g_lo = pm[:, 3 * RW_W + 128:3 * RW_W + 256]
    wa = jnp.where(lo_half, jnp.tanh(wa), wa)
    xwa = _dot(wa.astype(BF16), wwa_ref[...])
    log_w = -_softplus(-(w0_ref[...] + xwa[:, :RW_W])) - 0.5
    ld = -jnp.exp(log_w)
    a = _sigmoid(a0_ref[...] + xwa[:, RW_W:])
    gate = _dot(_sigmoid(g_lo).astype(BF16), g2_ref[...])
    g128 = g128_ref[...]

    def gsum(z):
        zs = jnp.concatenate([z[:, i * 128:(i + 1) * 128] for i in range(RW_PAIRS)], axis=0)
        hi, lo = split2(zs)
        s = _dot(hi, g128) + _dot(lo, g128)
        return jnp.concatenate([s[i * rows:(i + 1) * rows] for i in range(RW_PAIRS)], axis=1)

    kkv = k * kkw_ref[...]
    kk = kkv / jnp.maximum(jnp.sqrt(gsum(kkv * kkv)), 1e-12)
    k2 = k * (1.0 + (a - 1.0) * kaw_ref[...])
    beta = kk * a

    ld_hi, ld_lo = split2(ld)
    cs = _dot(trics_ref[...], ld_hi) + _dot(trics_ref[...], ld_lo)
    gc = cs[:rows]
    gtot = cs[rows:]
    e_g = jnp.exp(gc)
    e_ng = jnp.exp(-gc)
    e_l = jnp.exp(gtot - gc)
    ah = -kk * jnp.exp(gc - ld)
    rh = r * e_g
    bh = beta * e_ng
    kh = k2 * e_ng
    bt = beta * e_l
    kt = k2 * e_l
    e_tot = jnp.exp(gtot)

    ri = lax.broadcasted_iota(I32, (2 * tt, 2 * tt), 0)
    ci = lax.broadcasted_iota(I32, (2 * tt, 2 * tt), 1)
    same = (ri // c) == (ci // c)
    m_strict = jnp.where(same & (ci < ri), 1.0, 0.0)
    m_incl = jnp.where(same & (ci <= ri), 1.0, 0.0)
    eye2 = jnp.where(ri == ci, 1.0, 0.0)
    bd = jnp.where((ri // RW_N) == (ci // RW_N), 1.0, 0.0)

    def stack(z):
        return jnp.concatenate([jnp.where(lo_half, z, 0.0), jnp.where(lo_half, 0.0, z)], axis=0)

    def fold(z):
        return z[:tt] + z[tt:]

    rowc = lax.broadcasted_iota(I32, (tt, 1), 0) // c

    def chunk_expand(z):
        return jnp.concatenate([jnp.where(rowc == ch, z, 0.0) for ch in range(nc)], axis=1).astype(BF16)

    units = [(t, pr) for t in range(ntile) for pr in range(RW_PAIRS)]
    hw = 2 * tt

    def usl(z, u):
        return z[u[0] * tt:(u[0] + 1) * tt, u[1] * 128:(u[1] + 1) * 128]

    def bf(zs):
        return [z.astype(BF16) for z in zs]

    def rows2(a, b):
        return jnp.concatenate([a, b], axis=0)

    xr = [stack(usl(rh, u)) for u in units]
    xab, xrb, xbb, xkb, xvb = (bf([stack(usl(z, u)) for u in units]) for z in (ah, rh, bh, kh, v))
    lhs_ar = [rows2(a_, r_) for a_, r_ in zip(xab, xrb)]
    ab = [_dot_nt(l_, b_) for l_, b_ in zip(lhs_ar, xbb)]
    ak = [_dot_nt(l_, k_) for l_, k_ in zip(lhs_ar, xkb)]
    a_ab = [z[:hw] * m_strict for z in ab]
    a_rb = bf([z[hw:] * m_incl for z in ab])
    a_ak = bf([z[:hw] * m_strict for z in ak])
    a_rk = bf([z[hw:] * m_incl for z in ak])
    tm = [eye2 + z for z in a_ab]
    if c > 2:
        pw = [_dot(z, z) for z in bf(a_ab)]
    n = 2
    while n < c:
        pwb = bf(pw)
        if 2 * n < c:
            both = [_dot(rows2(t_.astype(BF16), p_), p_) for t_, p_ in zip(tm, pwb)]
            tm = [t_ + z[:hw] for t_, z in zip(tm, both)]
            pw = [z[hw:] for z in both]
        else:
            tm = [t_ + _dot(t_.astype(BF16), p_) for t_, p_ in zip(tm, pwb)]
        n *= 2
    tmb = bf(tm)
    akv = [_dot(rows2(p_, q_), x_) for p_, q_, x_ in zip(a_ak, a_rk, xvb)]
    tav = [_dot(t_, jnp.concatenate([x_, z[:hw].astype(BF16)], axis=1))
           for t_, x_, z in zip(tmb, xab, akv)]
    rbt = [_dot(r_, z) for r_, z in zip(a_rb, bf(tav))]
    rpb = bf([fold(x_ + z[:, :128]) for x_, z in zip(xr, rbt)])
    y0 = [fold(z[:, 128:] + w[hw:]) for z, w in zip(rbt, akv)]
    ap = bf([fold(z[:, :128]) for z in tav])
    vp = bf([fold(z[:, 128:]) for z in tav])
    bexp = [chunk_expand(usl(bt, u)) for u in units]
    kexp = [chunk_expand(usl(kt, u)) for u in units]
    vpl = bf([usl(v, u) for u in units])
    mt_all = [_dot_tn(a_, b_) for a_, b_ in zip(ap, bexp)]
    ht_all = [_dot_tn(rows2(p_, q_), rows2(b_, k_)) for p_, q_, b_, k_ in zip(vp, vpl, bexp, kexp)]

    sts = [st_ref[pr] for pr in range(RW_PAIRS)] if chain else None
    y_parts = [[] for _ in units]
    for t in range(ntile):
        for ch in range(nc):
            rs = slice(ch * c, (ch + 1) * c)
            cl = slice(ch * 128, (ch + 1) * 128)
            r0 = t * tt + ch * c
            for pr in range(RW_PAIRS):
                u = t * RW_PAIRS + pr
                mt = mt_all[u][:, cl] * bd + eye2 * e_tot[r0:r0 + 1, pr * 128:(pr + 1) * 128]
                ht = ht_all[u][:, cl] * bd
                if chain:
                    st = sts[pr]
                else:
                    sq = t * nc + ch
                    st = block_diag(s0_ref[sq, 2 * pr], s0_ref[sq, 2 * pr + 1])
                stb = st.astype(BF16)
                y_parts[u].append(_dot_nt(rpb[u][rs], stb) + y0[u][rs])
                st = _dot(stb, mt.astype(BF16)) + ht
                if chain:
                    sts[pr] = st
                else:
                    so_ref[sq, 2 * pr] = st[:RW_N, :RW_N]
                    so_ref[sq, 2 * pr + 1] = st[RW_N:, RW_N:]
    y_tiles = [jnp.concatenate([jnp.concatenate(y_parts[t * RW_PAIRS + pr], axis=0) for pr in range(RW_PAIRS)],
                               axis=1) for t in range(ntile)]
    y = y_tiles[0] if ntile == 1 else jnp.concatenate(y_tiles, axis=0)

    if chain:
        for pr in range(RW_PAIRS):
            st_ref[pr] = sts[pr]

        @pl.when(j == nt - 1)
        def _():
            for pr in range(RW_PAIRS):
                st = st_ref[pr]
                so_ref[0, 2 * pr] = st[:RW_N, :RW_N]
                so_ref[0, 2 * pr + 1] = st[RW_N:, RW_N:]

    inv_n = 1.0 / RW_N
    mean = gsum(y) * inv_n
    yc = y - mean
    var = gsum(yc * yc) * inv_n
    yn = yc * lax.rsqrt(var + RW_GN_EPS) * lnw_ref[...] + lnb_ref[...]
    bonus = gsum(r * k2 * rk_ref[...]) * v
    o_ref[...] = (yn + bonus) * gate


def _rwkv(proj_rw, prev, s0, wts, *, batch, seq, chain):
    n = proj_rw.shape[0]
    if chain:
        rows = _pick_tile(seq, RW_ROWS_CHAIN)
        nseq, c, nt = 1, RW_CHUNK, seq // rows
        prev_spec = pl.BlockSpec((1, 1, RW_COLS), lambda b, j: (b, 0, 0))
    else:
        assert seq & (seq - 1) == 0 and RW_TILE % seq == 0
        rows = _pick_tile(batch * seq, RW_ROWS_SHORT)
        nseq, c, nt = rows // seq, seq, 1
        prev_spec = pl.BlockSpec((nseq, RW_COLS), lambda b, j: (b, 0))
    ri = lax.broadcasted_iota(I32, (rows, rows), 0)
    ci = lax.broadcasted_iota(I32, (rows, rows), 1)
    same = (ri // c) == (ci // c)
    trics = jnp.concatenate([same & (ci <= ri), same], axis=0).astype(BF16)
    gi = lax.broadcasted_iota(I32, (128, 128), 0) // RW_N
    gj = lax.broadcasted_iota(I32, (128, 128), 1) // RW_N
    g128 = (gi == gj).astype(BF16)
    esel = (lax.broadcasted_iota(I32, (rows, nseq), 0) // seq
            == lax.broadcasted_iota(I32, (rows, nseq), 1)).astype(BF16)
    (mu, w0, a0, wwa, g2, kkw, kaw, rk, lnw, lnb) = wts

    def full(arr):
        return pl.BlockSpec(arr.shape, lambda b, j: (0,) * arr.ndim)

    consts = [mu, w0, a0, wwa, g2, kkw, kaw, rk, lnw, lnb, g128, trics, esel]
    in_specs = [pl.BlockSpec((rows, RW_COLS), lambda b, j: (b * nt + j, 0)),
                prev_spec,
                pl.BlockSpec((nseq, RW_H, RW_N, RW_N), lambda b, j: (b, 0, 0, 0))]
    in_specs += [full(t) for t in consts]
    args = [proj_rw, prev, s0] + consts
    kern = functools.partial(_rwkv_kernel, c=c, chain=chain, seq=seq, rows=rows)
    return pl.pallas_call(
        kern,
        grid=(batch // nseq, nt),
        in_specs=in_specs,
        out_specs=[pl.BlockSpec((rows, RW_W), lambda b, j: (b * nt + j, 0)),
                   pl.BlockSpec((nseq, RW_H, RW_N, RW_N), lambda b, j: (b, 0, 0, 0))],
        out_shape=[jax.ShapeDtypeStruct((n, RW_W), F32),
                   jax.ShapeDtypeStruct((batch, RW_H, RW_N, RW_N), F32)],
        scratch_shapes=[pltpu.VMEM((RW_PAIRS, 128, 128), F32), pltpu.VMEM((1, RW_COLS), F32)],
        compiler_params=_cparams(("parallel", "arbitrary")),
        name="rwkv7_mix",
    )(*args)


def _store_row_tiles(ref, x):
    m = x.shape[0]
    for s in range(D_MODEL // 128):
        ref[pl.ds(s, m, stride=8), :] = x[:, s * 128:(s + 1) * 128]


def _load_row_tiles(ref, m):
    return jnp.concatenate([ref[pl.ds(s, m, stride=8), :] for s in range(D_MODEL // 128)], axis=1)


def _outproj_router_kernel(ohg_ref, orw_ref, x_ref, wo_ref, gf_ref, wr_ref, br_ref, tri_ref, cin_ref,
                           x1_ref, hf_ref, idx_ref, gate_ref, pos_ref, cnt_ref, carry_ref):
    i = pl.program_id(0)

    @pl.when(i == 0)
    def _():
        carry_ref[...] = cin_ref[...].astype(F32)

    half = HG_H * HG_DV
    mixed = _dot(ohg_ref[...].astype(BF16), wo_ref[:half, :]) + _dot(orw_ref[...].astype(BF16), wo_ref[half:, :])
    x1 = x_ref[...] + mixed
    x1_ref[...] = x1
    hf = _rms(x1, gf_ref[...])
    _store_row_tiles(hf_ref, hf)

    h1, h2, _ = _split3(hf)
    w1, w2, _ = _split3(wr_ref[...])
    logits = _dot(h1, w1) + _dot(h1, w2) + _dot(h2, w1) + br_ref[...]

    tm = logits.shape[0]
    lane = lax.broadcasted_iota(I32, (tm, N_EXPERTS), 1).astype(F32)
    lane4 = lax.broadcasted_iota(I32, (tm, TOP_K), 1)
    vals = logits
    maskf = jnp.zeros((tm, N_EXPERTS), F32)
    sels, tops = [], []
    idx_out = jnp.zeros((tm, TOP_K), F32)
    for kq in range(TOP_K):
        m = jnp.max(vals, axis=-1, keepdims=True)
        idx = jnp.min(jnp.where(vals == m, lane, float(N_EXPERTS)), axis=-1, keepdims=True)
        sel = lane == idx
        sels.append(sel)
        tops.append(m)
        idx_out = jnp.where(lane4 == kq, idx, idx_out)
        vals = jnp.where(sel, -jnp.inf, vals)
        maskf = maskf + jnp.where(sel, 1.0, 0.0)
    es = [jnp.exp(t - tops[0]) for t in tops]
    denom = es[0] + es[1] + es[2] + es[3]
    gate_out = jnp.zeros((tm, TOP_K), F32)
    for kq in range(TOP_K):
        gate_out = jnp.where(lane4 == kq, es[kq] / denom, gate_out)

    pos = _dot(tri_ref[...], maskf.astype(BF16)) + carry_ref[...]
    pos_out = jnp.zeros((tm, TOP_K), F32)
    for kq in range(TOP_K):
        pk = jnp.sum(jnp.where(sels[kq], pos, 0.0), axis=-1, keepdims=True)
        pos_out = jnp.where(lane4 == kq, pk, pos_out)
    carry_ref[...] = carry_ref[...] + jnp.sum(maskf, axis=0, keepdims=True)
    idx_ref[...] = idx_out.astype(I32)
    gate_ref[...] = gate_out
    pos_ref[...] = pos_out.astype(I32)
    cnt_ref[...] = carry_ref[...].astype(I32)


def _outproj_router(o_hg, o_rw, x, wo_bf16, gf, w_router, b_router, cnt_in, tm):
    n = x.shape[0]
    tri = (lax.broadcasted_iota(I32, (tm, tm), 0) > lax.broadcasted_iota(I32, (tm, tm), 1)).astype(BF16)
    half = HG_H * HG_DV
    row = lambda i: (i, 0)
    fix = lambda i: (0, 0)
    return pl.pallas_call(
        _outproj_router_kernel,
        grid=(n // tm,),
        in_specs=[pl.BlockSpec((tm, half), row), pl.BlockSpec((tm, RW_W), row), pl.BlockSpec((tm, D_MODEL), row),
                  pl.BlockSpec((half + RW_W, D_MODEL), fix), pl.BlockSpec((1, D_MODEL), fix),
                  pl.BlockSpec((D_MODEL, N_EXPERTS), fix), pl.BlockSpec((1, N_EXPERTS), fix),
                  pl.BlockSpec((tm, tm), fix), pl.BlockSpec((1, N_EXPERTS), fix)],
        out_specs=[pl.BlockSpec((tm, D_MODEL), row), pl.BlockSpec((8 * tm, 128), row),
                   pl.BlockSpec((tm, TOP_K), row), pl.BlockSpec((tm, TOP_K), row), pl.BlockSpec((tm, TOP_K), row),
                   pl.BlockSpec((1, N_EXPERTS), fix)],
        out_shape=[jax.ShapeDtypeStruct((n, D_MODEL), F32), jax.ShapeDtypeStruct((8 * n, 128), F32),
                   jax.ShapeDtypeStruct((n, TOP_K), I32), jax.ShapeDtypeStruct((n, TOP_K), F32),
                   jax.ShapeDtypeStruct((n, TOP_K), I32), jax.ShapeDtypeStruct((1, N_EXPERTS), I32)],
        scratch_shapes=[pltpu.VMEM((1, N_EXPERTS), F32)],
        compiler_params=_cparams(("arbitrary",)),
        name="out_proj_router",
    )(o_hg, o_rw, x, wo_bf16, gf, w_router, b_router, tri, cnt_in)


def _plan_kernel(cnt_ref, idx_ref, pos_ref, dest_ref, bexp_ref, nblk_ref, *, n_blocks):
    cnt = cnt_ref[...].astype(F32)
    padded = jnp.floor((cnt + (MOE_BLOCK - 1)) * (1.0 / MOE_BLOCK)) * MOE_BLOCK
    ei = lax.broadcasted_iota(I32, (N_EXPERTS, N_EXPERTS), 0)
    ej = lax.broadcasted_iota(I32, (N_EXPERTS, N_EXPERTS), 1)
    upper = (ei < ej).astype(BF16)
    upper_incl = (ei <= ej).astype(BF16)
    start = _dot_sel_r(padded, upper)
    ends = _dot_sel_r(padded, upper_incl)
    idx = idx_ref[...]
    dest = pos_ref[...]
    for e in range(N_EXPERTS):
        dest = dest + jnp.where(idx == e, start[:, e:e + 1].astype(I32), 0)
    dest_ref[...] = dest
    bstart = (lax.broadcasted_iota(I32, (n_blocks, 1), 0) * MOE_BLOCK).astype(F32)
    be = jnp.sum(jnp.where(ends <= bstart, 1.0, 0.0), axis=-1, keepdims=True)
    bexp_ref[...] = jnp.minimum(be, N_EXPERTS - 1.0).astype(I32)
    nblk_ref[...] = (ends[:, N_EXPERTS - 1:] * (1.0 / MOE_BLOCK)).astype(I32)


def _plan(cnt, idx, pos, n_blocks, tm):
    n = idx.shape[0]
    row = lambda i: (i, 0)
    fix = lambda i: (0, 0)
    return pl.pallas_call(
        functools.partial(_plan_kernel, n_blocks=n_blocks),
        grid=(n // tm,),
        in_specs=[pl.BlockSpec((1, N_EXPERTS), fix), pl.BlockSpec((tm, TOP_K), row), pl.BlockSpec((tm, TOP_K), row)],
        out_specs=[pl.BlockSpec((tm, TOP_K), row), pl.BlockSpec((n_blocks, 1), fix), pl.BlockSpec((1, 1), fix)],
        out_shape=[jax.ShapeDtypeStruct((n, TOP_K), I32), jax.ShapeDtypeStruct((n_blocks, 1), I32),
                   jax.ShapeDtypeStruct((1, 1), I32)],
        compiler_params=_cparams(("arbitrary",)),
        name="moe_plan",
    )(cnt, idx, pos)


def _row_tile(ref, r):
    return ref.at[pl.ds(pl.multiple_of(r * 8, 8), 8)]


def _scatter_kernel(cnt_ref, nblk_ref, dest_a_ref, dest_b_ref, hf_a_ref, hf_b_ref, xs_ref, zbuf, sem, sem_pad, sem_blk,
                    *, ts, nt_a, n_blocks):
    i = pl.program_id(0)
    blk_rows = 8 * MOE_BLOCK
    shift = MOE_BLOCK.bit_length() - 1

    @pl.when(i == 0)
    def _():
        zbuf[...] = jnp.zeros_like(zbuf)

        def pad_expert(e, start):
            c = cnt_ref[e]
            padded = ((c + (MOE_BLOCK - 1)) >> shift) << shift

            def zrow(r, carry):
                pltpu.make_async_copy(_row_tile(zbuf, 0), _row_tile(xs_ref, start + r), sem_pad).start()
                return carry

            def wrow(r, carry):
                pltpu.make_async_copy(_row_tile(zbuf, 0), _row_tile(xs_ref, 0), sem_pad).wait()
                return carry

            lax.fori_loop(c, padded, zrow, 0)
            lax.fori_loop(c, padded, wrow, 0)
            return start + padded

        lax.fori_loop(0, N_EXPERTS, pad_expert, 0)

        def zblk(b, carry):
            pltpu.make_async_copy(zbuf, xs_ref.at[pl.ds(pl.multiple_of(b * blk_rows, blk_rows), blk_rows)],
                                  sem_blk).start()
            return carry

        def wblk(b, carry):
            pltpu.make_async_copy(zbuf, xs_ref.at[pl.ds(0, blk_rows)], sem_blk).wait()
            return carry

        lax.fori_loop(nblk_ref[0], n_blocks, zblk, 0)
        lax.fori_loop(nblk_ref[0], n_blocks, wblk, 0)

    def scatter_tile(dest_ref, hf_ref):
        def issue(t, carry):
            for kq in range(TOP_K):
                d = dest_ref[t * TOP_K + kq]
                pltpu.make_async_copy(_row_tile(hf_ref, t), _row_tile(xs_ref, d), sem).start()
            return carry

        def drain(t, carry):
            for kq in range(TOP_K):
                pltpu.make_async_copy(_row_tile(hf_ref, 0), _row_tile(xs_ref, 0), sem).wait()
            return carry

        lax.fori_loop(0, ts, issue, 0)
        lax.fori_loop(0, ts, drain, 0)

    @pl.when(i < nt_a)
    def _():
        scatter_tile(dest_a_ref, hf_a_ref)

    @pl.when(i >= nt_a)
    def _():
        scatter_tile(dest_b_ref, hf_b_ref)


def _scatter(cnt, nblk, dest_a, dest_b, hf_a, hf_b, n_blocks, ts):
    assert MOE_BLOCK & (MOE_BLOCK - 1) == 0
    nt_a = hf_a.shape[0] // (8 * ts)
    nt_b = hf_b.shape[0] // (8 * ts)
    amap = lambda i: (jnp.minimum(i, nt_a - 1),)
    bmap = lambda i: (jnp.maximum(i - nt_a, 0),)
    smem = pl.BlockSpec(memory_space=pltpu.SMEM)
    return pl.pallas_call(
        functools.partial(_scatter_kernel, ts=ts, nt_a=nt_a, n_blocks=n_blocks),
        grid=(nt_a + nt_b,),
        in_specs=[smem, smem,
                  pl.BlockSpec((ts * TOP_K,), amap, memory_space=pltpu.SMEM),
                  pl.BlockSpec((ts * TOP_K,), bmap, memory_space=pltpu.SMEM),
                  pl.BlockSpec((8 * ts, 128), lambda i: (jnp.minimum(i, nt_a - 1), 0)),
                  pl.BlockSpec((8 * ts, 128), lambda i: (jnp.maximum(i - nt_a, 0), 0))],
        out_specs=pl.BlockSpec(memory_space=pl.ANY),
        out_shape=jax.ShapeDtypeStruct((8 * n_blocks * MOE_BLOCK, 128), F32),
        scratch_shapes=[pltpu.VMEM((8 * MOE_BLOCK, 128), F32), pltpu.SemaphoreType.DMA(()),
                        pltpu.SemaphoreType.DMA(()), pltpu.SemaphoreType.DMA(())],
        compiler_params=_cparams(("arbitrary",)),
        name="moe_scatter",
    )(cnt, nblk, dest_a, dest_b, hf_a, hf_b)


def _ffn_kernel(bexp_ref, nblk_ref, xs_ref, wgu_ref, bgu_ref, wd_ref, bd_ref, yb_ref, wgu_bf, wd_bf):
    b = pl.program_id(0)
    valid = b < nblk_ref[0]
    prev_e = bexp_ref[jnp.maximum(b, 1) - 1]
    fresh = (b == 0) | (bexp_ref[b] != prev_e)

    @pl.when(valid & fresh)
    def _():
        wgu_bf[...] = wgu_ref[0].astype(BF16)
        wd_bf[...] = wd_ref[0].astype(BF16)

    @pl.when(valid)
    def _():
        x = _load_row_tiles(xs_ref, MOE_BLOCK).astype(BF16)
        gu = _dot(x, wgu_bf[...]) + bgu_ref[0]
        gate = jnp.minimum(gu[:, :D_FF], SWIGLU_LIMIT)
        up = jnp.clip(gu[:, D_FF:], -SWIGLU_LIMIT, SWIGLU_LIMIT)
        glu = gate * _sigmoid(SWIGLU_ALPHA * gate)
        act = ((up + 1.0) * glu).astype(BF16)
        _store_row_tiles(yb_ref, _dot(act, wd_bf[...]) + bd_ref[0])

    @pl.when(jnp.logical_not(valid))
    def _():
        yb_ref[...] = jnp.zeros_like(yb_ref)


def _ffn(bexp, nblk, xs, w_gate_up, b_gate_up, w_down, b_down):
    n_rows = xs.shape[0] // 8
    n_blocks = n_rows // MOE_BLOCK

    def blk(b, be, nb):
        return (jnp.minimum(b, jnp.maximum(nb[0], 1) - 1), 0)

    def wmap(b, be, nb):
        return (be[jnp.minimum(b, jnp.maximum(nb[0], 1) - 1)], 0, 0)

    grid_spec = pltpu.PrefetchScalarGridSpec(
        num_scalar_prefetch=2,
        grid=(n_blocks,),
        in_specs=[pl.BlockSpec((8 * MOE_BLOCK, 128), blk),
                  pl.BlockSpec((1, D_MODEL, 2 * D_FF), wmap),
                  pl.BlockSpec((1, 1, 2 * D_FF), wmap),
                  pl.BlockSpec((1, D_FF, D_MODEL), wmap),
                  pl.BlockSpec((1, 1, D_MODEL), wmap)],
        out_specs=pl.BlockSpec((8 * MOE_BLOCK, 128), lambda b, be, nb: (b, 0)),
        scratch_shapes=[pltpu.VMEM((D_MODEL, 2 * D_FF), BF16), pltpu.VMEM((D_FF, D_MODEL), BF16)],
    )
    return pl.pallas_call(
        _ffn_kernel,
        grid_spec=grid_spec,
        out_shape=jax.ShapeDtypeStruct((8 * n_rows, 128), F32),
        compiler_params=_cparams(("arbitrary",)),
        name="moe_ffn",
    )(bexp, nblk, xs, w_gate_up, b_gate_up, w_down, b_down)


def _combine_kernel(dest_ref, dest_nx_ref, gate_ref, x1_ref, gn_ref, yb_ref, y_ref, buf, sems, *, tc):
    i = pl.program_id(0)

    def fetch(d_ref, off, s):
        def issue(t, carry):
            for kq in range(TOP_K):
                d = d_ref[off + t * TOP_K + kq]
                pltpu.make_async_copy(_row_tile(yb_ref, d), _row_tile(buf.at[s, kq], t), sems.at[s]).start()
            return carry

        lax.fori_loop(0, tc, issue, 0)

    def drain(s):
        def wait(t, carry):
            for kq in range(TOP_K):
                pltpu.make_async_copy(_row_tile(yb_ref, 0), _row_tile(buf.at[s, 0], 0), sems.at[s]).wait()
            return carry

        lax.fori_loop(0, tc, wait, 0)

    def mix(s, rows):
        gate = gate_ref[rows, :]
        acc = x1_ref[rows, :]
        for kq in range(TOP_K):
            acc = acc + gate[:, kq:kq + 1] * _load_row_tiles(buf.at[s, kq], tc)
        y_ref[rows, :] = _rms(acc, gn_ref[...])

    @pl.when(i == 0)
    def _():
        fetch(dest_ref, 0, 0)

    fetch(dest_ref, tc * TOP_K, 1)
    drain(0)
    mix(0, slice(0, tc))

    @pl.when(i + 1 < pl.num_programs(0))
    def _():
        fetch(dest_nx_ref, 0, 0)

    drain(1)
    mix(1, slice(tc, 2 * tc))


def _combine(dest_flat, gate, x1, gn, yb, tc):
    n = x1.shape[0]
    ns = n // (2 * tc)
    return pl.pallas_call(
        functools.partial(_combine_kernel, tc=tc),
        grid=(ns,),
        in_specs=[pl.BlockSpec((2 * tc * TOP_K,), lambda i: (i,), memory_space=pltpu.SMEM),
                  pl.BlockSpec((2 * tc * TOP_K,), lambda i: (jnp.minimum(i + 1, ns - 1),), memory_space=pltpu.SMEM),
                  pl.BlockSpec((2 * tc, TOP_K), lambda i: (i, 0)),
                  pl.BlockSpec((2 * tc, D_MODEL), lambda i: (i, 0)),
                  pl.BlockSpec((1, D_MODEL), lambda i: (0, 0)),
                  pl.BlockSpec(memory_space=pl.ANY)],
        out_specs=pl.BlockSpec((2 * tc, D_MODEL), lambda i: (i, 0)),
        out_shape=jax.ShapeDtypeStruct((n, D_MODEL), F32),
        scratch_shapes=[pltpu.VMEM((2, TOP_K, 8 * tc, 128), F32), pltpu.SemaphoreType.DMA((2,))],
        compiler_params=_cparams(("arbitrary",)),
        name="moe_combine",
    )(dest_flat, dest_flat, gate, x1, gn, yb)


def _pick_tile(n, pref):
    t = pref
    while n % t:
        t //= 2
    return t


def kernel(x_prompt, x_sample, state_hgrn, state_rwkv, state_shift, norm_mix, w_in, lb_logits, hg_norm_w, rw_mu,
           rw_w0, rw_w2, rw_a0, rw_a2, rw_g2, rw_k_k, rw_k_a, rw_r_k, rw_lnx_w, rw_lnx_b, w_out, norm_ffn,
           w_router, b_router, w_gate_up, b_gate_up, w_down, b_down, norm_final):
    bp, tp, d = x_prompt.shape
    bs, tsq, _ = x_sample.shape
    assert norm_mix.shape[0] == 1 and d == D_MODEL
    n_p, n_s = bp * tp, bs * tsq
    n = n_p + n_s
    xp = x_prompt.reshape(n_p, d)
    xs_ = x_sample.reshape(n_s, d)

    w_in_b = w_in[0].astype(BF16)
    splits = (HG_COLS, RW_COLS)
    hgp_p, rwp_p = _proj(xp, norm_mix, w_in_b, splits, True, _pick_tile(n_p, 512))
    hgp_s, rwp_s = _proj(xs_, norm_mix, w_in_b, splits, True, _pick_tile(n_s, 256))
    ones_d = jnp.ones((1, d), F32)
    (prev_s,) = _proj(state_shift[0], ones_d, w_in_b[:, HG_COLS:], (RW_COLS,), False, _pick_tile(bs, 128))
    prev_p = jnp.zeros((bp, 1, RW_COLS), F32)
    x_last = jnp.concatenate([x_prompt[:, -1, :], x_sample[:, -1, :]], axis=0)
    shift = _rmsnorm_rows(x_last, norm_mix)

    zero_hg = jnp.zeros((bp, HG_H, HG_DK, HG_DV), F32)
    ohg_p, hg_p = _hgrn(hgp_p, lb_logits, hg_norm_w, zero_hg, batch=bp, seq=tp,
                        nseq=1, tb=_pick_tile(tp, 256), c=HG_CHUNK, sub=HG_SUB)
    ohg_s, hg_s = _hgrn(hgp_s, lb_logits, hg_norm_w, state_hgrn[0], batch=bs, seq=tsq,
                        nseq=_pick_tile(bs, 16), tb=tsq, c=tsq, sub=tsq)

    zpad = jnp.zeros((64, RW_W), F32)
    wwa = jnp.concatenate([jnp.concatenate([rw_w2[0], zpad], axis=1),
                           jnp.concatenate([zpad, rw_a2[0]], axis=1)], axis=0).astype(BF16)
    wts = (rw_mu, rw_w0, rw_a0, wwa, rw_g2[0].astype(BF16), rw_k_k, rw_k_a, rw_r_k.reshape(1, RW_W),
           rw_lnx_w, rw_lnx_b)
    zero_rw = jnp.zeros((bp, RW_H, RW_N, RW_N), F32)
    orw_p, rw_p = _rwkv(rwp_p, prev_p, zero_rw, wts, batch=bp, seq=tp, chain=True)
    orw_s, rw_s = _rwkv(rwp_s, prev_s, state_rwkv[0], wts, batch=bs, seq=tsq, chain=False)

    wo_b = w_out[0].astype(BF16)
    tm_p, tm_s = _pick_tile(n_p, 512), _pick_tile(n_s, 512)
    cnt0 = jnp.zeros((1, N_EXPERTS), I32)
    x1_p, hf_p, idx_p, gate_p, pos_p, cnt_p = _outproj_router(ohg_p, orw_p, xp, wo_b, norm_ffn, w_router[0],
                                                              b_router, cnt0, tm_p)
    x1_s, hf_s, idx_s, gate_s, pos_s, cnt = _outproj_router(ohg_s, orw_s, xs_, wo_b, norm_ffn, w_router[0],
                                                            b_router, cnt_p, tm_s)

    n_blocks = -(-(n * TOP_K) // MOE_BLOCK) + N_EXPERTS
    dest_p, bexp, nblk = _plan(cnt, idx_p, pos_p, n_blocks, tm_p)
    dest_s, _, _ = _plan(cnt, idx_s, pos_s, n_blocks, tm_s)
    dest_p = dest_p.reshape(n_p * TOP_K)
    dest_s = dest_s.reshape(n_s * TOP_K)
    nblk = nblk.reshape(1)
    ts = _pick_tile(math.gcd(n_p, n_s), 512)
    xs = _scatter(cnt.reshape(N_EXPERTS), nblk, dest_p, dest_s, hf_p, hf_s, n_blocks, ts)
    yb = _ffn(bexp.reshape(n_blocks), nblk, xs, w_gate_up[0], b_gate_up[0].reshape(N_EXPERTS, 1, 2 * D_FF),
              w_down[0], b_down[0].reshape(N_EXPERTS, 1, d))
    gn = norm_final.reshape(1, d)
    y_p = _combine(dest_p, gate_p, x1_p, gn, yb, _pick_tile(n_p // 2, 256))
    y_s = _combine(dest_s, gate_s, x1_s, gn, yb, _pick_tile(n_s // 2, 256))

    return (y_p.reshape(bp, tp, d), y_s.reshape(bs, tsq, d), hg_p[None], rw_p[None], shift[:bp][None],
            hg_s[None], rw_s[None], shift[bp:][None])
```

```python
import functools
import math

import jax
import jax.numpy as jnp
from jax import lax
from jax.experimental import pallas as pl
from jax.experimental.pallas import tpu as pltpu

F32 = jnp.float32
BF16 = jnp.bfloat16
I32 = jnp.int32

D_MODEL = 1024
HG_H, HG_DK, HG_DV = 4, 128, 128
HG_CHUNK = 64
HG_SUB = 8
RW_H, RW_N = 8, 64
RW_W = RW_H * RW_N
RW_PAIRS = RW_H // 2
RW_TILE = 64
RW_CHUNK = 32
RW_ROWS_CHAIN = 512
RW_ROWS_SHORT = 128
HG_COLS = 2 * HG_H * HG_DK + 2 * HG_H * HG_DV
RW_COLS = 3 * RW_W + 64 + 64 + 128
N_EXPERTS = 32
TOP_K = 4
D_FF = D_MODEL
SWIGLU_LIMIT = 7.0
SWIGLU_ALPHA = 1.702
NORM_EPS = 1e-6
RW_GN_EPS = 64e-5
MOE_BLOCK = 256
VMEM_LIMIT = 56 * 1024 * 1024


def _dot(a, b):
    return jnp.dot(a, b, preferred_element_type=F32)


def _dot_nt(a, b):
    return lax.dot_general(a, b, (((1,), (1,)), ((), ())), preferred_element_type=F32)


def _dot_tn(a, b):
    return lax.dot_general(a, b, (((0,), (0,)), ((), ())), preferred_element_type=F32)


def _split3(x):
    x1 = x.astype(BF16)
    r1 = x - x1.astype(F32)
    x2 = r1.astype(BF16)
    x3 = (r1 - x2.astype(F32)).astype(BF16)
    return x1, x2, x3


def _dot_sel_l(m_bf16, x):
    x1, x2, x3 = _split3(x)
    return _dot(m_bf16, x1) + _dot(m_bf16, x2) + _dot(m_bf16, x3)


def _dot_sel_r(x, m_bf16):
    x1, x2, x3 = _split3(x)
    return _dot(x1, m_bf16) + _dot(x2, m_bf16) + _dot(x3, m_bf16)


def _sigmoid(x):
    return 1.0 / (1.0 + jnp.exp(-x))


def _rms(x, g):
    return x * lax.rsqrt(jnp.mean(x * x, axis=-1, keepdims=True) + NORM_EPS) * g


def _cparams(sem):
    return pltpu.CompilerParams(dimension_semantics=sem, vmem_limit_bytes=VMEM_LIMIT)


def _proj_kernel(x_ref, g_ref, w_ref, *o_refs, normalize, splits):
    x = x_ref[...]
    h = _rms(x, g_ref[...]) if normalize else x
    hb = h.astype(BF16)
    c0 = 0
    for o_ref, width in zip(o_refs, splits):
        o_ref[...] = _dot(hb, w_ref[:, c0:c0 + width])
        c0 += width


def _proj(x, g, w_bf16, splits, normalize, tm):
    n = x.shape[0]
    kern = functools.partial(_proj_kernel, normalize=normalize, splits=splits)
    return pl.pallas_call(
        kern,
        grid=(n // tm,),
        in_specs=[pl.BlockSpec((tm, D_MODEL), lambda i: (i, 0)),
                  pl.BlockSpec((1, D_MODEL), lambda i: (0, 0)),
                  pl.BlockSpec((D_MODEL, sum(splits)), lambda i: (0, 0))],
        out_specs=[pl.BlockSpec((tm, s), lambda i: (i, 0)) for s in splits],
        out_shape=[jax.ShapeDtypeStruct((n, s), F32) for s in splits],
        compiler_params=_cparams(("parallel",)),
        name="norm_in_proj" if normalize else "shift_proj",
    )(x, g, w_bf16)


def _rmsnorm_rows_kernel(x_ref, g_ref, o_ref):
    o_ref[...] = _rms(x_ref[...], g_ref[...])


def _rmsnorm_rows(x, g):
    return pl.pallas_call(
        _rmsnorm_rows_kernel,
        out_shape=jax.ShapeDtypeStruct(x.shape, F32),
        name="shift_norm",
    )(x, g)


def _hgrn_masks(c, sub):
    ri = lax.broadcasted_iota(I32, (c, c), 0)
    ci = lax.broadcasted_iota(I32, (c, c), 1)
    masks = []
    h = sub
    while h < c:
        keep = ((ri // (2 * h)) == (ci // (2 * h))) & ((ri % (2 * h)) >= h) & ((ci % (2 * h)) < h)
        masks.append(keep.astype(F32))
        h *= 2
    return masks


def _hgrn_chunk(qs, fps, ivs, lbs, sts, tri, masks, c, sub):
    nh = len(qs)
    hs = range(nh)
    qf = [q * _sigmoid(q) for q in qs]
    f = [lb + (1.0 - lb) * _sigmoid(fp) for lb, fp in zip(lbs, fps)]
    lf = [jnp.log(z) for z in f]
    kc = [1.0 - z for z in f]
    g = [_dot_sel_l(tri, z) for z in lf]
    o_inter = [_dot_nt((qf[h] * jnp.exp(g[h])).astype(BF16), sts[h].astype(BF16)) for h in hs]
    ivb = [z.astype(BF16) for z in ivs]
    lane = lax.broadcasted_iota(I32, (sub, c), 1)
    row = lax.broadcasted_iota(I32, (sub, c), 0)
    a_rows = [[] for _ in hs]
    for i in range(c // sub):
        lo = i * sub
        blk = slice(lo, lo + sub)
        a = [jnp.zeros((sub, c), F32) for _ in hs]
        for s in range(sub):
            for h in hs:
                gi = g[h][blk]
                e = jnp.exp(jnp.minimum(gi - gi[s:s + 1, :], 0.0))
                col = jnp.sum(qf[h][blk] * (kc[h][lo + s:lo + s + 1, :] * e), axis=-1, keepdims=True)
                a[h] = jnp.where(lane == lo + s, col, a[h])
        for h in hs:
            a_rows[h].append(jnp.where(row + lo >= lane, a[h], 0.0))
    a_tot = [z[0] if len(z) == 1 else jnp.concatenate(z, axis=0) for z in a_rows]
    rowc = lax.broadcasted_iota(I32, (c, 1), 0)
    half = sub
    for mask in masks:
        gref = []
        for h in hs:
            ref = g[h][half - 1:half, :]
            for b in range(1, c // (2 * half)):
                r0 = b * 2 * half + half - 1
                ref = jnp.where(rowc >= b * 2 * half, g[h][r0:r0 + 1, :], ref)
            gref.append(ref)
        ql = [(qf[h] * jnp.exp(jnp.minimum(g[h] - gref[h], 0.0))).astype(BF16) for h in hs]
        kl = [(kc[h] * jnp.exp(jnp.minimum(gref[h] - g[h], 0.0))).astype(BF16) for h in hs]
        a_tot = [a_tot[h] + _dot_nt(ql[h], kl[h]) * mask for h in hs]
        half *= 2
    o = [o_inter[h] + _dot(a_tot[h].astype(BF16), ivb[h]) for h in hs]
    gl = [g[h][c - 1:c, :] for h in hs]
    kd = [(kc[h] * jnp.exp(gl[h] - g[h])).astype(BF16) for h in hs]
    st_new = [sts[h] * jnp.exp(gl[h]) + _dot_tn(ivb[h], kd[h]) for h in hs]
    return o, st_new


def _hgrn_kernel(q_ref, f_ref, i_ref, g_ref, lbl_ref, gw_ref, tri_ref, s0_ref, o_ref, so_ref, st_ref,
                 *, nseq, tb, c, sub):
    j = pl.program_id(1)
    nt = pl.num_programs(1)

    @pl.when(j == 0)
    def _():
        for s in range(nseq):
            for h in range(HG_H):
                st_ref[s * HG_H + h] = s0_ref[s, h].T

    lbl = lbl_ref[...]
    ex = jnp.exp(lbl - jnp.max(lbl, axis=0, keepdims=True))
    lb = ex[0:1, :] / jnp.sum(ex, axis=0, keepdims=True)
    gw = gw_ref[...]
    tri = tri_ref[...]
    masks = _hgrn_masks(c, sub)
    n_chunks = tb // c

    def body(it, carry):
        s = it // n_chunks
        r0 = pl.multiple_of(it * c, c)
        rows = pl.ds(r0, c)
        hsl = [slice(h * HG_DK, (h + 1) * HG_DK) for h in range(HG_H)]
        o, st_new = _hgrn_chunk([q_ref[rows, z] for z in hsl], [f_ref[rows, z] for z in hsl],
                                [i_ref[rows, z] for z in hsl], [lb[:, z] for z in hsl],
                                [st_ref[s * HG_H + h] for h in range(HG_H)], tri, masks, c, sub)
        for h in range(HG_H):
            st_ref[s * HG_H + h] = st_new[h]
            on = o[h] * lax.rsqrt(jnp.mean(o[h] * o[h], axis=-1, keepdims=True) + NORM_EPS) * gw
            gin = g_ref[rows, hsl[h]]
            o_ref[rows, hsl[h]] = on * (gin * _sigmoid(gin))
        return carry

    lax.fori_loop(0, nseq * n_chunks, body, 0)

    @pl.when(j == nt - 1)
    def _():
        for s in range(nseq):
            for h in range(HG_H):
                so_ref[s, h] = st_ref[s * HG_H + h].T


def _hgrn(proj_hg, lb_logits, gn_w, s0, *, batch, seq, nseq, tb, c, sub):
    n = proj_hg.shape[0]
    rows = nseq * tb
    nt = seq // tb
    hw = HG_H * HG_DK
    tri = (lax.broadcasted_iota(I32, (c, c), 0) >= lax.broadcasted_iota(I32, (c, c), 1)).astype(BF16)

    def rmap(col):
        return lambda b, j: (b * nt + j, col)

    kern = functools.partial(_hgrn_kernel, nseq=nseq, tb=tb, c=c, sub=sub)
    in_specs = [pl.BlockSpec((rows, hw), rmap(0)), pl.BlockSpec((rows, hw), rmap(1)),
                pl.BlockSpec((rows, hw), rmap(2)), pl.BlockSpec((rows, hw), rmap(3)),
                pl.BlockSpec((lb_logits.shape[0], hw), lambda b, j: (0, 0)),
                pl.BlockSpec((1, HG_DV), lambda b, j: (0, 0)),
                pl.BlockSpec((c, c), lambda b, j: (0, 0)),
                pl.BlockSpec((nseq, HG_H, HG_DK, HG_DV), lambda b, j: (b, 0, 0, 0))]
    args = [proj_hg, proj_hg, proj_hg, proj_hg, lb_logits, gn_w, tri, s0]
    return pl.pallas_call(
        kern,
        grid=(batch // nseq, nt),
        in_specs=in_specs,
        out_specs=[pl.BlockSpec((rows, hw), lambda b, j: (b * nt + j, 0)),
                   pl.BlockSpec((nseq, HG_H, HG_DK, HG_DV), lambda b, j: (b, 0, 0, 0))],
        out_shape=[jax.ShapeDtypeStruct((n, HG_H * HG_DV), F32),
                   jax.ShapeDtypeStruct((batch, HG_H, HG_DK, HG_DV), F32)],
        scratch_shapes=[pltpu.VMEM((nseq * HG_H, HG_DV, HG_DK), F32)],
        compiler_params=_cparams(("parallel", "arbitrary")),
        name="hgrn2_mix",
    )(*args)


def _softplus(z):
    return jnp.maximum(z, 0.0) + jnp.log(1.0 + jnp.exp(-jnp.abs(z)))


def _rwkv_kernel(p_ref, prev_ref, s0_ref, mu_ref, w0_ref, a0_ref, wwa_ref, g2_ref, kkw_ref, kaw_ref, rk_ref,
                 lnw_ref, lnb_ref, g128_ref, trics_ref, esel_ref, o_ref, so_ref, st_ref, carry_ref,
                 *, c, chain, seq, rows):
    tt = RW_TILE
    ntile = rows // tt
    nc = tt // c
    j = pl.program_id(1)
    nt = pl.num_programs(1)
    lane128 = lax.broadcasted_iota(I32, (1, 128), 1)
    lo_half = lane128 < RW_N

    def block_diag(se, so):
        z = jnp.zeros((RW_N, RW_N), F32)
        return jnp.concatenate([jnp.concatenate([se, z], axis=1), jnp.concatenate([z, so], axis=1)], axis=0)

    def split2(z):
        hi = z.astype(BF16)
        return hi, (z - hi.astype(F32)).astype(BF16)

    p = p_ref[...]
    rolled = pltpu.roll(p, 1, 0)
    rowi = lax.broadcasted_iota(I32, (rows, 1), 0)
    if chain:
        @pl.when(j == 0)
        def _():
            carry_ref[...] = prev_ref[0]
            for pr in range(RW_PAIRS):
                st_ref[pr] = block_diag(s0_ref[0, 2 * pr], s0_ref[0, 2 * pr + 1])

        ps = jnp.where(rowi == 0, carry_ref[...], rolled)
        carry_ref[...] = p[rows - 1:rows, :]
    else:
        first = _dot_sel_l(esel_ref[...], prev_ref[...])
        ps = jnp.where((rowi & (seq - 1)) == 0, first, rolled)
    pm = p + (ps - p) * mu_ref[...]

    r = pm[:, 0:RW_W]
    k = pm[:, RW_W:2 * RW_W]
    v = pm[:, 2 * RW_W:3 * RW_W]
    wa = pm[:, 3 * RW_W:3 * RW_W + 128]
    g_lo = pm[:, 3 * RW_W + 128:3 * RW_W + 256]
    wa = jnp.where(lo_half, jnp.tanh(wa), wa)
    xwa = _dot(wa.astype(BF16), wwa_ref[...])
    log_w = -_softplus(-(w0_ref[...] + xwa[:, :RW_W])) - 0.5
    ld = -jnp.exp(log_w)
    a = _sigmoid(a0_ref[...] + xwa[:, RW_W:])
    gate = _dot(_sigmoid(g_lo).astype(BF16), g2_ref[...])
    g128 = g128_ref[...]

    def gsum(z):
        zs = jnp.concatenate([z[:, i * 128:(i + 1) * 128] for i in range(RW_PAIRS)], axis=0)
        hi, lo = split2(zs)
        s = _dot(hi, g128) + _dot(lo, g128)
        return jnp.concatenate([s[i * rows:(i + 1) * rows] for i in range(RW_PAIRS)], axis=1)

    kkv = k * kkw_ref[...]
    kk = kkv / jnp.maximum(jnp.sqrt(gsum(kkv * kkv)), 1e-12)
    k2 = k * (1.0 + (a - 1.0) * kaw_ref[...])
    beta = kk * a

    ld_hi, ld_lo = split2(ld)
    cs = _dot(trics_ref[...], ld_hi) + _dot(trics_ref[...], ld_lo)
    gc = cs[:rows]
    gtot = cs[rows:]
    e_g = jnp.exp(gc)
    e_ng = jnp.exp(-gc)
    e_l = jnp.exp(gtot - gc)
    ah = -kk * jnp.exp(gc - ld)
    rh = r * e_g
    bh = beta * e_ng
    kh = k2 * e_ng
    bt = beta * e_l
    kt = k2 * e_l
    e_tot = jnp.exp(gtot)

    ri = lax.broadcasted_iota(I32, (2 * tt, 2 * tt), 0)
    ci = lax.broadcasted_iota(I32, (2 * tt, 2 * tt), 1)
    same = (ri // c) == (ci // c)
    m_strict = jnp.where(same & (ci < ri), 1.0, 0.0)
    m_incl = jnp.where(same & (ci <= ri), 1.0, 0.0)
    eye2 = jnp.where(ri == ci, 1.0, 0.0)
    bd = jnp.where((ri // RW_N) == (ci // RW_N), 1.0, 0.0)

    def stack(z):
        return jnp.concatenate([jnp.where(lo_half, z, 0.0), jnp.where(lo_half, 0.0, z)], axis=0)

    def fold(z):
        return z[:tt] + z[tt:]

    rowc = lax.broadcasted_iota(I32, (tt, 1), 0) // c

    def chunk_expand(z):
        return jnp.concatenate([jnp.where(rowc == ch, z, 0.0) for ch in range(nc)], axis=1).astype(BF16)

    units = [(t, pr) for t in range(ntile) for pr in range(RW_PAIRS)]
    hw = 2 * tt

    def usl(z, u):
        return z[u[0] * tt:(u[0] + 1) * tt, u[1] * 128:(u[1] + 1) * 128]

    def bf(zs):
        return [z.astype(BF16) for z in zs]

    def rows2(a, b):
        return jnp.concatenate([a, b], axis=0)

    xr = [stack(usl(rh, u)) for u in units]
    xab, xrb, xbb, xkb, xvb = (bf([stack(usl(z, u)) for u in units]) for z in (ah, rh, bh, kh, v))
    lhs_ar = [rows2(a_, r_) for a_, r_ in zip(xab, xrb)]
    ab = [_dot_nt(l_, b_) for l_, b_ in zip(lhs_ar, xbb)]
    ak = [_dot_nt(l_, k_) for l_, k_ in zip(lhs_ar, xkb)]
    a_ab = [z[:hw] * m_strict for z in ab]
    a_rb = bf([z[hw:] * m_incl for z in ab])
    a_ak = bf([z[:hw] * m_strict for z in ak])
    a_rk = bf([z[hw:] * m_incl for z in ak])
    tm = [eye2 + z for z in a_ab]
    if c > 2:
        pw = [_dot(z, z) for z in bf(a_ab)]
    n = 2
    while n < c:
        pwb = bf(pw)
        if 2 * n < c:
            both = [_dot(rows2(t_.astype(BF16), p_), p_) for t_, p_ in zip(tm, pwb)]
            tm = [t_ + z[:hw] for t_, z in zip(tm, both)]
            pw = [z[hw:] for z in both]
        else:
            tm = [t_ + _dot(t_.astype(BF16), p_) for t_, p_ in zip(tm, pwb)]
        n *= 2
    tmb = bf(tm)
    akv = [_dot(rows2(p_, q_), x_) for p_, q_, x_ in zip(a_ak, a_rk, xvb)]
    tav = [_dot(t_, jnp.concatenate([x_, z[:hw].astype(BF16)], axis=1))
           for t_, x_, z in zip(tmb, xab, akv)]
    rbt = [_dot(r_, z) for r_, z in zip(a_rb, bf(tav))]
    rpb = bf([fold(x_ + z[:, :128]) for x_, z in zip(xr, rbt)])
    y0 = [fold(z[:, 128:] + w[hw:]) for z, w in zip(rbt, akv)]
    ap = bf([fold(z[:, :128]) for z in tav])
    vp = bf([fold(z[:, 128:]) for z in tav])
    bexp = [chunk_expand(usl(bt, u)) for u in units]
    kexp = [chunk_expand(usl(kt, u)) for u in units]
    vpl = bf([usl(v, u) for u in units])
    mt_all = [_dot_tn(a_, b_) for a_, b_ in zip(ap, bexp)]
    ht_all = [_dot_tn(rows2(p_, q_), rows2(b_, k_)) for p_, q_, b_, k_ in zip(vp, vpl, bexp, kexp)]

    sts = [st_ref[pr] for pr in range(RW_PAIRS)] if chain else None
    y_parts = [[] for _ in units]
    for t in range(ntile):
        for ch in range(nc):
            rs = slice(ch * c, (ch + 1) * c)
            cl = slice(ch * 128, (ch + 1) * 128)
            r0 = t * tt + ch * c
            for pr in range(RW_PAIRS):
                u = t * RW_PAIRS + pr
                mt = mt_all[u][:, cl] * bd + eye2 * e_tot[r0:r0 + 1, pr * 128:(pr + 1) * 128]
                ht = ht_all[u][:, cl] * bd
                if chain:
                    st = sts[pr]
                else:
                    sq = t * nc + ch
                    st = block_diag(s0_ref[sq, 2 * pr], s0_ref[sq, 2 * pr + 1])
                stb = st.astype(BF16)
                y_parts[u].append(_dot_nt(rpb[u][rs], stb) + y0[u][rs])
                st = _dot(stb, mt.astype(BF16)) + ht
                if chain:
                    sts[pr] = st
                else:
                    so_ref[sq, 2 * pr] = st[:RW_N, :RW_N]
                    so_ref[sq, 2 * pr + 1] = st[RW_N:, RW_N:]
    y_tiles = [jnp.concatenate([jnp.concatenate(y_parts[t * RW_PAIRS + pr], axis=0) for pr in range(RW_PAIRS)],
                               axis=1) for t in range(ntile)]
    y = y_tiles[0] if ntile == 1 else jnp.concatenate(y_tiles, axis=0)

    if chain:
        for pr in range(RW_PAIRS):
            st_ref[pr] = sts[pr]

        @pl.when(j == nt - 1)
        def _():
            for pr in range(RW_PAIRS):
                st = st_ref[pr]
                so_ref[0, 2 * pr] = st[:RW_N, :RW_N]
                so_ref[0, 2 * pr + 1] = st[RW_N:, RW_N:]

    inv_n = 1.0 / RW_N
    mean = gsum(y) * inv_n
    yc = y - mean
    var = gsum(yc * yc) * inv_n
    yn = yc * lax.rsqrt(var + RW_GN_EPS) * lnw_ref[...] + lnb_ref[...]
    bonus = gsum(r * k2 * rk_ref[...]) * v
    o_ref[...] = (yn + bonus) * gate


def _rwkv(proj_rw, prev, s0, wts, *, batch, seq, chain):
    n = proj_rw.shape[0]
    if chain:
        rows = _pick_tile(seq, RW_ROWS_CHAIN)
        nseq, c, nt = 1, RW_CHUNK, seq // rows
        prev_spec = pl.BlockSpec((1, 1, RW_COLS), lambda b, j: (b, 0, 0))
    else:
        assert seq & (seq - 1) == 0 and RW_TILE % seq == 0
        rows = _pick_tile(batch * seq, RW_ROWS_SHORT)
        nseq, c, nt = rows // seq, seq, 1
        prev_spec = pl.BlockSpec((nseq, RW_COLS), lambda b, j: (b, 0))
    ri = lax.broadcasted_iota(I32, (rows, rows), 0)
    ci = lax.broadcasted_iota(I32, (rows, rows), 1)
    same = (ri // c) == (ci // c)
    trics = jnp.concatenate([same & (ci <= ri), same], axis=0).astype(BF16)
    gi = lax.broadcasted_iota(I32, (128, 128), 0) // RW_N
    gj = lax.broadcasted_iota(I32, (128, 128), 1) // RW_N
    g128 = (gi == gj).astype(BF16)
    esel = (lax.broadcasted_iota(I32, (rows, nseq), 0) // seq
            == lax.broadcasted_iota(I32, (rows, nseq), 1)).astype(BF16)
    (mu, w0, a0, wwa, g2, kkw, kaw, rk, lnw, lnb) = wts

    def full(arr):
        return pl.BlockSpec(arr.shape, lambda b, j: (0,) * arr.ndim)

    consts = [mu, w0, a0, wwa, g2, kkw, kaw, rk, lnw, lnb, g128, trics, esel]
    in_specs = [pl.BlockSpec((rows, RW_COLS), lambda b, j: (b * nt + j, 0)),
                prev_spec,
                pl.BlockSpec((nseq, RW_H, RW_N, RW_N), lambda b, j: (b, 0, 0, 0))]
    in_specs += [full(t) for t in consts]
    args = [proj_rw, prev, s0] + consts
    kern = functools.partial(_rwkv_kernel, c=c, chain=chain, seq=seq, rows=rows)
    return pl.pallas_call(
        kern,
        grid=(batch // nseq, nt),
        in_specs=in_specs,
        out_specs=[pl.BlockSpec((rows, RW_W), lambda b, j: (b * nt + j, 0)),
                   pl.BlockSpec((nseq, RW_H, RW_N, RW_N), lambda b, j: (b, 0, 0, 0))],
        out_shape=[jax.ShapeDtypeStruct((n, RW_W), F32),
                   jax.ShapeDtypeStruct((batch, RW_H, RW_N, RW_N), F32)],
        scratch_shapes=[pltpu.VMEM((RW_PAIRS, 128, 128), F32), pltpu.VMEM((1, RW_COLS), F32)],
        compiler_params=_cparams(("parallel", "arbitrary")),
        name="rwkv7_mix",
    )(*args)


def _store_row_tiles(ref, x):
    m = x.shape[0]
    for s in range(D_MODEL // 128):
        ref[pl.ds(s, m, stride=8), :] = x[:, s * 128:(s + 1) * 128]


def _load_row_tiles(ref, m):
    return jnp.concatenate([ref[pl.ds(s, m, stride=8), :] for s in range(D_MODEL // 128)], axis=1)


def _outproj_router_kernel(ohg_ref, orw_ref, x_ref, wo_ref, gf_ref, wr_ref, br_ref, tri_ref, cin_ref,
                           x1_ref, hf_ref, idx_ref, gate_ref, pos_ref, cnt_ref, carry_ref):
    i = pl.program_id(0)

    @pl.when(i == 0)
    def _():
        carry_ref[...] = cin_ref[...].astype(F32)

    half = HG_H * HG_DV
    mixed = _dot(ohg_ref[...].astype(BF16), wo_ref[:half, :]) + _dot(orw_ref[...].astype(BF16), wo_ref[half:, :])
    x1 = x_ref[...] + mixed
    x1_ref[...] = x1
    hf = _rms(x1, gf_ref[...])
    _store_row_tiles(hf_ref, hf)

    h1, h2, _ = _split3(hf)
    w1, w2, _ = _split3(wr_ref[...])
    logits = _dot(h1, w1) + _dot(h1, w2) + _dot(h2, w1) + br_ref[...]

    tm = logits.shape[0]
    lane = lax.broadcasted_iota(I32, (tm, N_EXPERTS), 1).astype(F32)
    lane4 = lax.broadcasted_iota(I32, (tm, TOP_K), 1)
    vals = logits
    maskf = jnp.zeros((tm, N_EXPERTS), F32)
    sels, tops = [], []
    idx_out = jnp.zeros((tm, TOP_K), F32)
    for kq in range(TOP_K):
        m = jnp.max(vals, axis=-1, keepdims=True)
        idx = jnp.min(jnp.where(vals == m, lane, float(N_EXPERTS)), axis=-1, keepdims=True)
        sel = lane == idx
        sels.append(sel)
        tops.append(m)
        idx_out = jnp.where(lane4 == kq, idx, idx_out)
        vals = jnp.where(sel, -jnp.inf, vals)
        maskf = maskf + jnp.where(sel, 1.0, 0.0)
    es = [jnp.exp(t - tops[0]) for t in tops]
    denom = es[0] + es[1] + es[2] + es[3]
    gate_out = jnp.zeros((tm, TOP_K), F32)
    for kq in range(TOP_K):
        gate_out = jnp.where(lane4 == kq, es[kq] / denom, gate_out)

    pos = _dot(tri_ref[...], maskf.astype(BF16)) + carry_ref[...]
    pos_out = jnp.zeros((tm, TOP_K), F32)
    for kq in range(TOP_K):
        pk = jnp.sum(jnp.where(sels[kq], pos, 0.0), axis=-1, keepdims=True)
        pos_out = jnp.where(lane4 == kq, pk, pos_out)
    carry_ref[...] = carry_ref[...] + jnp.sum(maskf, axis=0, keepdims=True)
    idx_ref[...] = idx_out.astype(I32)
    gate_ref[...] = gate_out
    pos_ref[...] = pos_out.astype(I32)
    cnt_ref[...] = carry_ref[...].astype(I32)


def _outproj_router(o_hg, o_rw, x, wo_bf16, gf, w_router, b_router, cnt_in, tm):
    n = x.shape[0]
    tri = (lax.broadcasted_iota(I32, (tm, tm), 0) > lax.broadcasted_iota(I32, (tm, tm), 1)).astype(BF16)
    half = HG_H * HG_DV
    row = lambda i: (i, 0)
    fix = lambda i: (0, 0)
    return pl.pallas_call(
        _outproj_router_kernel,
        grid=(n // tm,),
        in_specs=[pl.BlockSpec((tm, half), row), pl.BlockSpec((tm, RW_W), row), pl.BlockSpec((tm, D_MODEL), row),
                  pl.BlockSpec((half + RW_W, D_MODEL), fix), pl.BlockSpec((1, D_MODEL), fix),
                  pl.BlockSpec((D_MODEL, N_EXPERTS), fix), pl.BlockSpec((1, N_EXPERTS), fix),
                  pl.BlockSpec((tm, tm), fix), pl.BlockSpec((1, N_EXPERTS), fix)],
        out_specs=[pl.BlockSpec((tm, D_MODEL), row), pl.BlockSpec((8 * tm, 128), row),
                   pl.BlockSpec((tm, TOP_K), row), pl.BlockSpec((tm, TOP_K), row), pl.BlockSpec((tm, TOP_K), row),
                   pl.BlockSpec((1, N_EXPERTS), fix)],
        out_shape=[jax.ShapeDtypeStruct((n, D_MODEL), F32), jax.ShapeDtypeStruct((8 * n, 128), F32),
                   jax.ShapeDtypeStruct((n, TOP_K), I32), jax.ShapeDtypeStruct((n, TOP_K), F32),
                   jax.ShapeDtypeStruct((n, TOP_K), I32), jax.ShapeDtypeStruct((1, N_EXPERTS), I32)],
        scratch_shapes=[pltpu.VMEM((1, N_EXPERTS), F32)],
        compiler_params=_cparams(("arbitrary",)),
        name="out_proj_router",
    )(o_hg, o_rw, x, wo_bf16, gf, w_router, b_router, tri, cnt_in)


def _plan_kernel(cnt_ref, idx_ref, pos_ref, dest_ref, bexp_ref, nblk_ref, *, n_blocks):
    cnt = cnt_ref[...].astype(F32)
    padded = jnp.floor((cnt + (MOE_BLOCK - 1)) * (1.0 / MOE_BLOCK)) * MOE_BLOCK
    ei = lax.broadcasted_iota(I32, (N_EXPERTS, N_EXPERTS), 0)
    ej = lax.broadcasted_iota(I32, (N_EXPERTS, N_EXPERTS), 1)
    upper = (ei < ej).astype(BF16)
    upper_incl = (ei <= ej).astype(BF16)
    start = _dot_sel_r(padded, upper)
    ends = _dot_sel_r(padded, upper_incl)
    idx = idx_ref[...]
    dest = pos_ref[...]
    for e in range(N_EXPERTS):
        dest = dest + jnp.where(idx == e, start[:, e:e + 1].astype(I32), 0)
    dest_ref[...] = dest
    bstart = (lax.broadcasted_iota(I32, (n_blocks, 1), 0) * MOE_BLOCK).astype(F32)
    be = jnp.sum(jnp.where(ends <= bstart, 1.0, 0.0), axis=-1, keepdims=True)
    bexp_ref[...] = jnp.minimum(be, N_EXPERTS - 1.0).astype(I32)
    nblk_ref[...] = (ends[:, N_EXPERTS - 1:] * (1.0 / MOE_BLOCK)).astype(I32)


def _plan(cnt, idx, pos, n_blocks, tm):
    n = idx.shape[0]
    row = lambda i: (i, 0)
    fix = lambda i: (0, 0)
    return pl.pallas_call(
        functools.partial(_plan_kernel, n_blocks=n_blocks),
        grid=(n // tm,),
        in_specs=[pl.BlockSpec((1, N_EXPERTS), fix), pl.BlockSpec((tm, TOP_K), row), pl.BlockSpec((tm, TOP_K), row)],
        out_specs=[pl.BlockSpec((tm, TOP_K), row), pl.BlockSpec((n_blocks, 1), fix), pl.BlockSpec((1, 1), fix)],
        out_shape=[jax.ShapeDtypeStruct((n, TOP_K), I32), jax.ShapeDtypeStruct((n_blocks, 1), I32),
                   jax.ShapeDtypeStruct((1, 1), I32)],
        compiler_params=_cparams(("arbitrary",)),
        name="moe_plan",
    )(cnt, idx, pos)


def _row_tile(ref, r):
    return ref.at[pl.ds(pl.multiple_of(r * 8, 8), 8)]


def _scatter_kernel(cnt_ref, nblk_ref, dest_a_ref, dest_b_ref, hf_a_ref, hf_b_ref, xs_ref, zbuf, sem, sem_pad, sem_blk,
                    *, ts, nt_a, n_blocks):
    i = pl.program_id(0)
    blk_rows = 8 * MOE_BLOCK
    shift = MOE_BLOCK.bit_length() - 1

    @pl.when(i == 0)
    def _():
        zbuf[...] = jnp.zeros_like(zbuf)

        def pad_expert(e, start):
            c = cnt_ref[e]
            padded = ((c + (MOE_BLOCK - 1)) >> shift) << shift

            def zrow(r, carry):
                pltpu.make_async_copy(_row_tile(zbuf, 0), _row_tile(xs_ref, start + r), sem_pad).start()
                return carry

            def wrow(r, carry):
                pltpu.make_async_copy(_row_tile(zbuf, 0), _row_tile(xs_ref, 0), sem_pad).wait()
                return carry

            lax.fori_loop(c, padded, zrow, 0)
            lax.fori_loop(c, padded, wrow, 0)
            return start + padded

        lax.fori_loop(0, N_EXPERTS, pad_expert, 0)

        def zblk(b, carry):
            pltpu.make_async_copy(zbuf, xs_ref.at[pl.ds(pl.multiple_of(b * blk_rows, blk_rows), blk_rows)],
                                  sem_blk).start()
            return carry

        def wblk(b, carry):
            pltpu.make_async_copy(zbuf, xs_ref.at[pl.ds(0, blk_rows)], sem_blk).wait()
            return carry

        lax.fori_loop(nblk_ref[0], n_blocks, zblk, 0)
        lax.fori_loop(nblk_ref[0], n_blocks, wblk, 0)

    def scatter_tile(dest_ref, hf_ref):
        def issue(t, carry):
            for kq in range(TOP_K):
                d = dest_ref[t * TOP_K + kq]
                pltpu.make_async_copy(_row_tile(hf_ref, t), _row_tile(xs_ref, d), sem).start(priority=kq % 2)
            return carry

        def drain(t, carry):
            for kq in range(TOP_K):
                pltpu.make_async_copy(_row_tile(hf_ref, 0), _row_tile(xs_ref, 0), sem).wait()
            return carry

        lax.fori_loop(0, ts, issue, 0)
        lax.fori_loop(0, ts, drain, 0)

    @pl.when(i < nt_a)
    def _():
        scatter_tile(dest_a_ref, hf_a_ref)

    @pl.when(i >= nt_a)
    def _():
        scatter_tile(dest_b_ref, hf_b_ref)


def _scatter(cnt, nblk, dest_a, dest_b, hf_a, hf_b, n_blocks, ts):
    assert MOE_BLOCK & (MOE_BLOCK - 1) == 0
    nt_a = hf_a.shape[0] // (8 * ts)
    nt_b = hf_b.shape[0] // (8 * ts)
    amap = lambda i: (jnp.minimum(i, nt_a - 1),)
    bmap = lambda i: (jnp.maximum(i - nt_a, 0),)
    smem = pl.BlockSpec(memory_space=pltpu.SMEM)
    return pl.pallas_call(
        functools.partial(_scatter_kernel, ts=ts, nt_a=nt_a, n_blocks=n_blocks),
        grid=(nt_a + nt_b,),
        in_specs=[smem, smem,
                  pl.BlockSpec((ts * TOP_K,), amap, memory_space=pltpu.SMEM),
                  pl.BlockSpec((ts * TOP_K,), bmap, memory_space=pltpu.SMEM),
                  pl.BlockSpec((8 * ts, 128), lambda i: (jnp.minimum(i, nt_a - 1), 0)),
                  pl.BlockSpec((8 * ts, 128), lambda i: (jnp.maximum(i - nt_a, 0), 0))],
        out_specs=pl.BlockSpec(memory_space=pl.ANY),
        out_shape=jax.ShapeDtypeStruct((8 * n_blocks * MOE_BLOCK, 128), F32),
        scratch_shapes=[pltpu.VMEM((8 * MOE_BLOCK, 128), F32), pltpu.SemaphoreType.DMA(()),
                        pltpu.SemaphoreType.DMA(()), pltpu.SemaphoreType.DMA(())],
        compiler_params=_cparams(("arbitrary",)),
        name="moe_scatter",
    )(cnt, nblk, dest_a, dest_b, hf_a, hf_b)


def _ffn_kernel(bexp_ref, nblk_ref, xs_ref, wgu_ref, bgu_ref, wd_ref, bd_ref, yb_ref, wgu_bf, wd_bf):
    b = pl.program_id(0)
    valid = b < nblk_ref[0]
    prev_e = bexp_ref[jnp.maximum(b, 1) - 1]
    fresh = (b == 0) | (bexp_ref[b] != prev_e)

    @pl.when(valid & fresh)
    def _():
        wgu_bf[...] = wgu_ref[0].astype(BF16)
        wd_bf[...] = wd_ref[0].astype(BF16)

    @pl.when(valid)
    def _():
        x = _load_row_tiles(xs_ref, MOE_BLOCK).astype(BF16)
        gu = _dot(x, wgu_bf[...]) + bgu_ref[0]
        gate = jnp.minimum(gu[:, :D_FF], SWIGLU_LIMIT)
        up = jnp.clip(gu[:, D_FF:], -SWIGLU_LIMIT, SWIGLU_LIMIT)
        glu = gate * _sigmoid(SWIGLU_ALPHA * gate)
        act = ((up + 1.0) * glu).astype(BF16)
        _store_row_tiles(yb_ref, _dot(act, wd_bf[...]) + bd_ref[0])

    @pl.when(jnp.logical_not(valid))
    def _():
        yb_ref[...] = jnp.zeros_like(yb_ref)


def _ffn(bexp, nblk, xs, w_gate_up, b_gate_up, w_down, b_down):
    n_rows = xs.shape[0] // 8
    n_blocks = n_rows // MOE_BLOCK

    def blk(b, be, nb):
        return (jnp.minimum(b, jnp.maximum(nb[0], 1) - 1), 0)

    def wmap(b, be, nb):
        return (be[jnp.minimum(b, jnp.maximum(nb[0], 1) - 1)], 0, 0)

    grid_spec = pltpu.PrefetchScalarGridSpec(
        num_scalar_prefetch=2,
        grid=(n_blocks,),
        in_specs=[pl.BlockSpec((8 * MOE_BLOCK, 128), blk),
                  pl.BlockSpec((1, D_MODEL, 2 * D_FF), wmap),
                  pl.BlockSpec((1, 1, 2 * D_FF), wmap),
                  pl.BlockSpec((1, D_FF, D_MODEL), wmap),
                  pl.BlockSpec((1, 1, D_MODEL), wmap)],
        out_specs=pl.BlockSpec((8 * MOE_BLOCK, 128), lambda b, be, nb: (b, 0)),
        scratch_shapes=[pltpu.VMEM((D_MODEL, 2 * D_FF), BF16), pltpu.VMEM((D_FF, D_MODEL), BF16)],
    )
    return pl.pallas_call(
        _ffn_kernel,
        grid_spec=grid_spec,
        out_shape=jax.ShapeDtypeStruct((8 * n_rows, 128), F32),
        compiler_params=_cparams(("arbitrary",)),
        name="moe_ffn",
    )(bexp, nblk, xs, w_gate_up, b_gate_up, w_down, b_down)


def _combine_kernel(dest_ref, dest_nx_ref, gate_ref, x1_ref, gn_ref, yb_ref, y_ref, buf, sems, *, tc):
    i = pl.program_id(0)

    def fetch(d_ref, off, s):
        def issue(t, carry):
            for kq in range(TOP_K):
                d = d_ref[off + t * TOP_K + kq]
                pltpu.make_async_copy(_row_tile(yb_ref, d), _row_tile(buf.at[s, kq], t),
                                      sems.at[s]).start(priority=kq % 2)
            return carry

        lax.fori_loop(0, tc, issue, 0)

    def drain(s):
        def wait(t, carry):
            for kq in range(TOP_K):
                pltpu.make_async_copy(_row_tile(yb_ref, 0), _row_tile(buf.at[s, 0], 0), sems.at[s]).wait()
            return carry

        lax.fori_loop(0, tc, wait, 0)

    def mix(s, rows):
        gate = gate_ref[rows, :]
        acc = x1_ref[rows, :]
        for kq in range(TOP_K):
            acc = acc + gate[:, kq:kq + 1] * _load_row_tiles(buf.at[s, kq], tc)
        y_ref[rows, :] = _rms(acc, gn_ref[...])

    @pl.when(i == 0)
    def _():
        fetch(dest_ref, 0, 0)

    fetch(dest_ref, tc * TOP_K, 1)
    drain(0)
    mix(0, slice(0, tc))

    @pl.when(i + 1 < pl.num_programs(0))
    def _():
        fetch(dest_nx_ref, 0, 0)

    drain(1)
    mix(1, slice(tc, 2 * tc))


def _combine(dest_flat, gate, x1, gn, yb, tc):
    n = x1.shape[0]
    ns = n // (2 * tc)
    return pl.pallas_call(
        functools.partial(_combine_kernel, tc=tc),
        grid=(ns,),
        in_specs=[pl.BlockSpec((2 * tc * TOP_K,), lambda i: (i,), memory_space=pltpu.SMEM),
                  pl.BlockSpec((2 * tc * TOP_K,), lambda i: (jnp.minimum(i + 1, ns - 1),), memory_space=pltpu.SMEM),
                  pl.BlockSpec((2 * tc, TOP_K), lambda i: (i, 0)),
                  pl.BlockSpec((2 * tc, D_MODEL), lambda i: (i, 0)),
                  pl.BlockSpec((1, D_MODEL), lambda i: (0, 0)),
                  pl.BlockSpec(memory_space=pl.ANY)],
        out_specs=pl.BlockSpec((2 * tc, D_MODEL), lambda i: (i, 0)),
        out_shape=jax.ShapeDtypeStruct((n, D_MODEL), F32),
        scratch_shapes=[pltpu.VMEM((2, TOP_K, 8 * tc, 128), F32), pltpu.SemaphoreType.DMA((2,))],
        compiler_params=_cparams(("arbitrary",)),
        name="moe_combine",
    )(dest_flat, dest_flat, gate, x1, gn, yb)


def _pick_tile(n, pref):
    t = pref
    while n % t:
        t //= 2
    return t


def kernel(x_prompt, x_sample, state_hgrn, state_rwkv, state_shift, norm_mix, w_in, lb_logits, hg_norm_w, rw_mu,
           rw_w0, rw_w2, rw_a0, rw_a2, rw_g2, rw_k_k, rw_k_a, rw_r_k, rw_lnx_w, rw_lnx_b, w_out, norm_ffn,
           w_router, b_router, w_gate_up, b_gate_up, w_down, b_down, norm_final):
    bp, tp, d = x_prompt.shape
    bs, tsq, _ = x_sample.shape
    assert norm_mix.shape[0] == 1 and d == D_MODEL
    n_p, n_s = bp * tp, bs * tsq
    n = n_p + n_s
    xp = x_prompt.reshape(n_p, d)
    xs_ = x_sample.reshape(n_s, d)

    w_in_b = w_in[0].astype(BF16)
    splits = (HG_COLS, RW_COLS)
    hgp_p, rwp_p = _proj(xp, norm_mix, w_in_b, splits, True, _pick_tile(n_p, 512))
    hgp_s, rwp_s = _proj(xs_, norm_mix, w_in_b, splits, True, _pick_tile(n_s, 256))
    ones_d = jnp.ones((1, d), F32)
    (prev_s,) = _proj(state_shift[0], ones_d, w_in_b[:, HG_COLS:], (RW_COLS,), False, _pick_tile(bs, 128))
    prev_p = jnp.zeros((bp, 1, RW_COLS), F32)
    x_last = jnp.concatenate([x_prompt[:, -1, :], x_sample[:, -1, :]], axis=0)
    shift = _rmsnorm_rows(x_last, norm_mix)

    zero_hg = jnp.zeros((bp, HG_H, HG_DK, HG_DV), F32)
    ohg_p, hg_p = _hgrn(hgp_p, lb_logits, hg_norm_w, zero_hg, batch=bp, seq=tp,
                        nseq=1, tb=_pick_tile(tp, 256), c=HG_CHUNK, sub=HG_SUB)
    ohg_s, hg_s = _hgrn(hgp_s, lb_logits, hg_norm_w, state_hgrn[0], batch=bs, seq=tsq,
                        nseq=_pick_tile(bs, 16), tb=tsq, c=tsq, sub=tsq)

    zpad = jnp.zeros((64, RW_W), F32)
    wwa = jnp.concatenate([jnp.concatenate([rw_w2[0], zpad], axis=1),
                           jnp.concatenate([zpad, rw_a2[0]], axis=1)], axis=0).astype(BF16)
    wts = (rw_mu, rw_w0, rw_a0, wwa, rw_g2[0].astype(BF16), rw_k_k, rw_k_a, rw_r_k.reshape(1, RW_W),
           rw_lnx_w, rw_lnx_b)
    zero_rw = jnp.zeros((bp, RW_H, RW_N, RW_N), F32)
    orw_p, rw_p = _rwkv(rwp_p, prev_p, zero_rw, wts, batch=bp, seq=tp, chain=True)
    orw_s, rw_s = _rwkv(rwp_s, prev_s, state_rwkv[0], wts, batch=bs, seq=tsq, chain=False)

    wo_b = w_out[0].astype(BF16)
    tm_p, tm_s = _pick_tile(n_p, 512), _pick_tile(n_s, 512)
    cnt0 = jnp.zeros((1, N_EXPERTS), I32)
    x1_p, hf_p, idx_p, gate_p, pos_p, cnt_p = _outproj_router(ohg_p, orw_p, xp, wo_b, norm_ffn, w_router[0],
                                                              b_router, cnt0, tm_p)
    x1_s, hf_s, idx_s, gate_s, pos_s, cnt = _outproj_router(ohg_s, orw_s, xs_, wo_b, norm_ffn, w_router[0],
                                                            b_router, cnt_p, tm_s)

    n_blocks = -(-(n * TOP_K) // MOE_BLOCK) + N_EXPERTS
    dest_p, bexp, nblk = _plan(cnt, idx_p, pos_p, n_blocks, tm_p)
    dest_s, _, _ = _plan(cnt, idx_s, pos_s, n_blocks, tm_s)
    dest_p = dest_p.reshape(n_p * TOP_K)
    dest_s = dest_s.reshape(n_s * TOP_K)
    nblk = nblk.reshape(1)
    ts = _pick_tile(math.gcd(n_p, n_s), 512)
    xs = _scatter(cnt.reshape(N_EXPERTS), nblk, dest_p, dest_s, hf_p, hf_s, n_blocks, ts)
    yb = _ffn(bexp.reshape(n_blocks), nblk, xs, w_gate_up[0], b_gate_up[0].reshape(N_EXPERTS, 1, 2 * D_FF),
              w_down[0], b_down[0].reshape(N_EXPERTS, 1, d))
    gn = norm_final.reshape(1, d)
    y_p = _combine(dest_p, gate_p, x1_p, gn, yb, _pick_tile(n_p // 2, 256))
    y_s = _combine(dest_s, gate_s, x1_s, gn, yb, _pick_tile(n_s // 2, 256))

    return (y_p.reshape(bp, tp, d), y_s.reshape(bs, tsq, d), hg_p[None], rw_p[None], shift[:bp][None],
            hg_s[None], rw_s[None], shift[bp:][None])
```
